```python
import jax, jax.numpy as jnp
from jax import lax
import numpy as np

D_MODEL = 1024
BATCH = 4
SEQ = 4096
DEPTH = 2

GRID_W = 64
N_MEM = 256
HEAD_DIM = 64
NORM_EPS = 1e-6
MEM_HEADS = 4
MEM_DIM = MEM_HEADS * HEAD_DIM
MIX_DIM = D_MODEL - MEM_DIM
NA_HEADS = MIX_DIM // HEAD_DIM
NA_WIN_H = 8
NA_WIN_W = 16
NA_QBLK_W = 16
NA_KBLK_W = NA_QBLK_W + NA_WIN_W
NA_IN = 3 * MIX_DIM + MEM_DIM
GLA_HEADS = 4
GLA_DK = MIX_DIM // 2 // GLA_HEADS
GLA_DV = MIX_DIM // GLA_HEADS
GLA_GATE_RANK = 16
GLA_TAU = 16.0
GLA_CHUNK = 64
GLA_IN = 2 * GLA_HEADS * GLA_DK + 2 * MIX_DIM + 2 * GLA_GATE_RANK + MEM_DIM
N_GROUPS = 4
EXPERTS_PER_GROUP = 8
N_EXPERTS = N_GROUPS * EXPERTS_PER_GROUP
TOP_K = 2
EXPERT_FF = 512
MOE_BLOCK = 256
N_NA_LAYERS = (DEPTH + 1) // 2
N_GLA_LAYERS = DEPTH // 2

kernel_name = "hybrid_na_gla_hmoe_encoder"


def rms_norm(x, g):
    xf = x.astype(jnp.float32)
    y = xf * lax.rsqrt(jnp.mean(xf * xf, axis=-1, keepdims=True) + NORM_EPS)
    return (y * g.astype(jnp.float32)).astype(x.dtype)


def neighbourhood_attention(q, k, v, rpb):
    B, T, H, Dh = q.shape
    rows = T // GRID_W
    kh = min(NA_WIN_H, rows)
    ncb = GRID_W // NA_QBLK_W
    n_keys = kh * NA_KBLK_W
    scale = HEAD_DIM ** -0.5
    qcol = np.arange(GRID_W).reshape(ncb, NA_QBLK_W)
    kcol = (np.clip(np.arange(ncb) * NA_QBLK_W - NA_WIN_W // 2, 0, GRID_W - NA_KBLK_W)[:, None]
            + np.arange(NA_KBLK_W))
    cstart = np.clip(qcol - NA_WIN_W // 2, 0, GRID_W - NA_WIN_W)
    col_in = ((kcol[:, None, :] >= cstart[:, :, None])
              & (kcol[:, None, :] < cstart[:, :, None] + NA_WIN_W))
    dcol = np.clip(kcol[:, None, :] - qcol[:, :, None], 1 - NA_WIN_W, NA_WIN_W - 1) + NA_WIN_W - 1
    rstart = np.clip(np.arange(rows) - kh // 2, 0, rows - kh)
    drow = rstart[:, None] + np.arange(kh)[None, :] - np.arange(rows)[:, None] + NA_WIN_H - 1
    bias = rpb.astype(jnp.float32)[:, drow[:, :, None, None, None], dcol[None, None]]
    bias = jnp.where(col_in[None, None, None], bias, -jnp.inf)
    bias = bias.transpose(1, 0, 3, 4, 2, 5).reshape(rows, H, ncb, NA_QBLK_W, n_keys)
    kg = k.reshape(B, rows, GRID_W, H, Dh)
    vg = v.reshape(B, rows, GRID_W, H, Dh)
    qg = q.reshape(B, rows, GRID_W, H, Dh).transpose(1, 0, 2, 3, 4)

    def row_block(args):
        r0, q_row, b_row = args

        def gather(a):
            a = lax.dynamic_slice_in_dim(a, r0, kh, axis=1)[:, :, kcol]
            return a.transpose(0, 2, 1, 3, 4, 5).reshape(B, ncb, n_keys, H, Dh)

        k_blk, v_blk = gather(kg), gather(vg)
        q_blk = q_row.reshape(B, ncb, NA_QBLK_W, H, Dh)
        s = jnp.einsum('bjqhd,bjnhd->bhjqn', q_blk, k_blk,
                       preferred_element_type=jnp.float32) * scale + b_row
        p = jax.nn.softmax(s, axis=-1).astype(v.dtype)
        o = jnp.einsum('bhjqn,bjnhd->bjqhd', p, v_blk)
        return o.reshape(B, GRID_W, H, Dh)

    out = lax.map(row_block, (jnp.asarray(rstart, jnp.int32), qg, bias))
    return out.transpose(1, 0, 2, 3, 4).reshape(B, T, H * Dh)


def gla_chunked(q, k, v, log_a):
    B, T, H, dk = q.shape
    dv = v.shape[-1]
    c = GLA_CHUNK
    n = T // c
    chunk = lambda a: a.reshape(B, n, c, H, a.shape[-1]).transpose(0, 1, 3, 2, 4)
    q, k, v, log_a = chunk(q), chunk(k), chunk(v), chunk(log_a)
    b = jnp.cumsum(log_a, axis=3)
    b_mid = b[:, :, :, c // 2 - 1:c // 2]
    b_last = b[:, :, :, -1:]
    a = jnp.einsum('bnhid,bnhjd->bnhij', q * jnp.exp(b - b_mid), k * jnp.exp(b_mid - b))
    a = jnp.where(jnp.tril(jnp.ones((c, c), dtype=bool)), a, 0.0)
    o_intra = jnp.einsum('bnhij,bnhjv->bnhiv', a, v)
    q_st = jnp.moveaxis(q * jnp.exp(b), 1, 0)
    k_st = jnp.moveaxis(k * jnp.exp(b_last - b), 1, 0)
    v_st = jnp.moveaxis(v, 1, 0)
    dec = jnp.moveaxis(jnp.exp(b_last[:, :, :, 0]), 1, 0)

    def step(S, xs):
        qs, ks, vs, d = xs
        o = jnp.einsum('bhid,bhdv->bhiv', qs, S)
        S = S * d[..., None] + jnp.einsum('bhjd,bhjv->bhdv', ks, vs)
        return S, o

    S0 = jnp.zeros((B, H, dk, dv), jnp.float32)
    _, o_inter = lax.scan(step, S0, (q_st, k_st, v_st, dec))
    o = o_intra + jnp.moveaxis(o_inter, 0, 1)
    return o.transpose(0, 1, 3, 2, 4).reshape(B, T, H, dv)


def gla_mixer(u, gate_up, gate_bias, out_norm):
    B, T, _ = u.shape
    kd = GLA_HEADS * GLA_DK
    q, k, v, g, r_f, r_b, mq = jnp.split(
        u, np.cumsum([kd, kd, MIX_DIM, MIX_DIM, GLA_GATE_RANK, GLA_GATE_RANK]), axis=-1)
    heads = lambda a, d: a.astype(jnp.float32).reshape(B, T, GLA_HEADS, d)
    q = heads(q, GLA_DK) * (GLA_DK ** -0.5)
    k = heads(k, GLA_DK)
    v = heads(v, GLA_DV)

    def log_decay(r, d):
        z = (r @ gate_up[d]).astype(jnp.float32) + gate_bias[d].astype(jnp.float32)
        return heads(jax.nn.log_sigmoid(z) / GLA_TAU, GLA_DK)

    flip = lambda a: a[:, ::-1]
    o_fwd = gla_chunked(q, k, v, log_decay(r_f, 0))
    o_bwd = flip(gla_chunked(flip(q), flip(k), flip(v), flip(log_decay(r_b, 1))))
    o = rms_norm(o_fwd + o_bwd, out_norm).astype(u.dtype).reshape(B, T, MIX_DIM)
    return o * jax.nn.silu(g), mq


def memory_attention(mq, mkv):
    B, T, _ = mq.shape
    q = mq.reshape(B, T, MEM_HEADS, HEAD_DIM)
    k, v = jnp.split(mkv, 2, axis=-1)
    k = k.reshape(B, -1, MEM_HEADS, HEAD_DIM)
    v = v.reshape(B, -1, MEM_HEADS, HEAD_DIM)
    s = jnp.einsum('bthd,bmhd->bhtm', q, k, preferred_element_type=jnp.float32) * (HEAD_DIM ** -0.5)
    p = jax.nn.softmax(s, axis=-1).astype(v.dtype)
    return jnp.einsum('bhtm,bmhd->bthd', p, v).reshape(B, T, MEM_DIM)


def hierarchical_moe(x, w_group, b_group, w_router, b_router, w1, w3, w2):
    B, T, D = x.shape
    xt = x.reshape(-1, D)
    n_tok = xt.shape[0]
    g_prob = jax.nn.softmax((xt @ w_group).astype(jnp.float32) + b_group.astype(jnp.float32), axis=-1)
    g_gate, g_idx = lax.top_k(g_prob, 1)
    e_logits = ((xt @ w_router).astype(jnp.float32) + b_router.astype(jnp.float32)
                ).reshape(n_tok, N_GROUPS, EXPERTS_PER_GROUP)
    e_logits = jnp.take_along_axis(e_logits, g_idx[:, :, None], axis=1)[:, 0]
    top_l, top_i = lax.top_k(e_logits, TOP_K)
    gate = jax.nn.softmax(top_l, axis=-1) * g_gate
    expert = g_idx * EXPERTS_PER_GROUP + top_i
    n_asg = n_tok * TOP_K
    a_exp = expert.reshape(-1)
    a_tok = jnp.repeat(jnp.arange(n_tok, dtype=jnp.int32), TOP_K)
    a_w = gate.reshape(-1)
    order = jnp.argsort(a_exp)
    s_exp = a_exp[order]
    counts = jnp.zeros((N_EXPERTS,), jnp.int32).at[a_exp].add(1)
    padded = (counts + MOE_BLOCK - 1) // MOE_BLOCK * MOE_BLOCK
    start = jnp.cumsum(counts) - counts
    pend = jnp.cumsum(padded)
    pstart = pend - padded
    dest = pstart[s_exp] + jnp.arange(n_asg, dtype=jnp.int32) - start[s_exp]
    n_blocks = -(-n_asg // MOE_BLOCK) + N_EXPERTS
    n_rows = n_blocks * MOE_BLOCK
    row_tok = jnp.full((n_rows,), n_tok, jnp.int32).at[dest].set(a_tok[order])
    row_w = jnp.zeros((n_rows,), x.dtype).at[dest].set(a_w[order].astype(x.dtype))
    block_start = jnp.arange(n_blocks, dtype=jnp.int32) * MOE_BLOCK
    block_exp = jnp.minimum(jnp.sum(block_start[:, None] >= pend[None, :], axis=1), N_EXPERTS - 1)
    x_pad = jnp.concatenate([xt, jnp.zeros((1, D), xt.dtype)], axis=0)
    xb = x_pad[row_tok].reshape(n_blocks, MOE_BLOCK, D)

    def expert_block(args):
        xi, e = args
        return (jax.nn.silu(xi @ w1[e]) * (xi @ w3[e])) @ w2[e]

    yb = lax.map(expert_block, (xb, block_exp)).reshape(n_rows, D)
    y = jnp.zeros((n_tok + 1, D), x.dtype).at[row_tok].add(yb * row_w[:, None])[:n_tok]
    return y.reshape(B, T, D)


def setup_inputs(seed: int = 0) -> dict:
    key = jax.random.key(seed)
    ks = jax.random.split(key, 21)
    nrm = lambda k, shape, s: jax.random.normal(k, shape, jnp.float32) * s
    D = D_MODEL
    return {
        "x": nrm(ks[0], (BATCH, SEQ, D), 1.0),
        "mem": nrm(ks[1], (BATCH, N_MEM, D), 1.0),
        "mem_norm": 1.0 + nrm(ks[2], (D,), 0.02),
        "final_norm": 1.0 + nrm(ks[3], (D,), 0.02),
        "norm_mix": 1.0 + nrm(ks[4], (DEPTH, D), 0.02),
        "norm_ffn": 1.0 + nrm(ks[5], (DEPTH, D), 0.02),
        "w_mem_kv": nrm(ks[6], (DEPTH, D, 2 * MEM_DIM), D ** -0.5),
        "w_out": nrm(ks[7], (DEPTH, D, D), D ** -0.5),
        "na_w_in": nrm(ks[8], (N_NA_LAYERS, D, NA_IN), D ** -0.5),
        "na_rpb": nrm(ks[9], (N_NA_LAYERS, NA_HEADS, 2 * NA_WIN_H - 1, 2 * NA_WIN_W - 1), 0.1),
        "gla_w_in": nrm(ks[10], (N_GLA_LAYERS, D, GLA_IN), D ** -0.5),
        "gla_gate_up": nrm(ks[11], (N_GLA_LAYERS, 2, GLA_GATE_RANK, GLA_HEADS * GLA_DK), GLA_GATE_RANK ** -0.5),
        "gla_gate_bias": nrm(ks[12], (N_GLA_LAYERS, 2, GLA_HEADS * GLA_DK), 0.1),
        "gla_out_norm": 1.0 + nrm(ks[13], (N_GLA_LAYERS, GLA_DV), 0.02),
        "moe_w_group": nrm(ks[14], (DEPTH, D, N_GROUPS), D ** -0.5),
        "moe_b_group": nrm(ks[15], (DEPTH, N_GROUPS), 0.01),
        "moe_w_router": nrm(ks[16], (DEPTH, D, N_EXPERTS), D ** -0.5),
        "moe_b_router": nrm(ks[17], (DEPTH, N_EXPERTS), 0.01),
        "moe_w1": nrm(ks[18], (DEPTH, N_EXPERTS, D, EXPERT_FF), D ** -0.5),
        "moe_w3": nrm(ks[19], (DEPTH, N_EXPERTS, D, EXPERT_FF), D ** -0.5),
        "moe_w2": nrm(ks[20], (DEPTH, N_EXPERTS, EXPERT_FF, D), EXPERT_FF ** -0.5),
    }


def reference(x, mem, mem_norm, final_norm, norm_mix, norm_ffn, w_mem_kv, w_out,
              na_w_in, na_rpb, gla_w_in, gla_gate_up, gla_gate_bias, gla_out_norm,
              moe_w_group, moe_b_group, moe_w_router, moe_b_router, moe_w1, moe_w3, moe_w2):
    B, T, _ = x.shape
    mem_n = rms_norm(mem, mem_norm)
    for i in range(DEPTH):
        h = rms_norm(x, norm_mix[i])
        mkv = mem_n @ w_mem_kv[i]
        j = i // 2
        if i % 2 == 0:
            u = h @ na_w_in[j]
            q, k, v, mq = jnp.split(u, [MIX_DIM, 2 * MIX_DIM, 3 * MIX_DIM], axis=-1)
            hd = lambda a: a.reshape(B, T, NA_HEADS, HEAD_DIM)
            mix = neighbourhood_attention(hd(q), hd(k), hd(v), na_rpb[j])
        else:
            mix, mq = gla_mixer(h @ gla_w_in[j], gla_gate_up[j], gla_gate_bias[j], gla_out_norm[j])
        att = memory_attention(mq, mkv)
        x = x + jnp.concatenate([mix, att], axis=-1) @ w_out[i]
        x = x + hierarchical_moe(rms_norm(x, norm_ffn[i]), moe_w_group[i], moe_b_group[i],
                                 moe_w_router[i], moe_b_router[i], moe_w1[i], moe_w3[i], moe_w2[i])
    return rms_norm(x, final_norm)
```

```python
import functools

import numpy as np
import jax
import jax.numpy as jnp
from jax import lax
from jax.experimental import pallas as pl
from jax.experimental.pallas import tpu as pltpu

F32 = jnp.float32
BF16 = jnp.bfloat16

D_MODEL = 1024
GRID_W = 64
HEAD_DIM = 64
NORM_EPS = 1e-6
MEM_HEADS = 4
MEM_DIM = MEM_HEADS * HEAD_DIM
MIX_DIM = D_MODEL - MEM_DIM
NA_HEADS = MIX_DIM // HEAD_DIM
NA_WIN_H = 8
NA_WIN_W = 16
GLA_HEADS = 4
GLA_DK = MIX_DIM // 2 // GLA_HEADS
GLA_DV = MIX_DIM // GLA_HEADS
GLA_GATE_RANK = 16
GLA_TAU = 16.0
GLA_CHUNK = 64
N_GROUPS = 4
EXPERTS_PER_GROUP = 8
N_EXPERTS = N_GROUPS * EXPERTS_PER_GROUP
TOP_K = 2
EXPERT_FF = 512
MOE_BLOCK = 256

V7X_LANES = 128
V7X_MXU_DIM = 256
V7X_VMEM_LIMIT_BYTES = 56 * 1024 * 1024

GLA_DK_PAD = 128
GLA_DV_PAD = 256
GLA_Q_OFF = 0
GLA_K_OFF = GLA_Q_OFF + GLA_HEADS * GLA_DK_PAD
GLA_V_OFF = GLA_K_OFF + GLA_HEADS * GLA_DK_PAD
GLA_G_OFF = GLA_V_OFF + GLA_HEADS * GLA_DV_PAD
GLA_R_OFF = GLA_G_OFF + GLA_HEADS * GLA_DV_PAD
GLA_R_PAD = 256
GLA_MQ_OFF = GLA_R_OFF + GLA_R_PAD
GLA_IN_PAD = GLA_MQ_OFF + MEM_DIM
GLA_MIX_PAD = GLA_HEADS * GLA_DV_PAD
NA_MQ_OFF = 3 * MIX_DIM


def _cparams(semantics):
    return pltpu.CompilerParams(dimension_semantics=semantics, vmem_limit_bytes=V7X_VMEM_LIMIT_BYTES)


def _norm_matmul_kernel(x_ref, g_ref, w_ref, o_ref, *, col_chunk):
    x = x_ref[...]
    ms = jnp.mean(x * x, axis=-1, keepdims=True)
    y = (x * lax.rsqrt(ms + NORM_EPS) * g_ref[...]).astype(BF16)
    n_out = o_ref.shape[1]
    for c in range(0, n_out, col_chunk):
        o_ref[:, c:c + col_chunk] = jnp.dot(
            y, w_ref[:, c:c + col_chunk], preferred_element_type=F32).astype(o_ref.dtype)


def _norm_matmul(x, g, w, *, tm, out_dtype, name):
    n, d = x.shape
    n_out = w.shape[1]
    col_chunk = 512 if n_out % 512 == 0 else n_out
    return pl.pallas_call(
        functools.partial(_norm_matmul_kernel, col_chunk=col_chunk),
        grid=(n // tm,),
        in_specs=[pl.BlockSpec((tm, d), lambda i: (i, 0)),
                  pl.BlockSpec((1, d), lambda i: (0, 0)),
                  pl.BlockSpec((d, n_out), lambda i: (0, 0))],
        out_specs=pl.BlockSpec((tm, n_out), lambda i: (i, 0)),
        out_shape=jax.ShapeDtypeStruct((n, n_out), out_dtype),
        compiler_params=_cparams(("parallel",)),
        name=name,
    )(x, g.reshape(1, d), w)


def _na_kernel(q_ref, k_ref, v_ref, bias_ref, o_ref):
    r = pl.program_id(1)
    rows = k_ref.shape[1] // GRID_W
    rs = jnp.clip(r - NA_WIN_H // 2, 0, rows - NA_WIN_H)
    start = pl.multiple_of(rs * GRID_W, GRID_W)
    n_keys = NA_WIN_H * GRID_W
    heads_per_slab = V7X_MXU_DIM // HEAD_DIM
    scale = HEAD_DIM ** -0.5
    lane_head = lax.broadcasted_iota(jnp.int32, (GRID_W, V7X_MXU_DIM), 1) // HEAD_DIM
    for s in range(MIX_DIM // V7X_MXU_DIM):
        cs = slice(s * V7X_MXU_DIM, (s + 1) * V7X_MXU_DIM)
        qq = q_ref[0, :, cs] * scale
        kw = k_ref[0, pl.ds(start, n_keys), cs]
        vw = v_ref[0, pl.ds(start, n_keys), cs]
        lhs = jnp.concatenate(
            [jnp.where(lane_head == i, qq, jnp.zeros_like(qq)) for i in range(heads_per_slab)], axis=0)
        sc = lax.dot_general(lhs, kw, (((1,), (1,)), ((), ())), preferred_element_type=F32)
        sc = sc + bias_ref[0, s * V7X_MXU_DIM:(s + 1) * V7X_MXU_DIM, :]
        m = jnp.max(sc, axis=-1, keepdims=True)
        p = jnp.exp(sc - m)
        l = jnp.sum(p, axis=-1, keepdims=True)
        o = jnp.dot(p.astype(BF16), vw, preferred_element_type=F32)
        o = o * (1.0 / l)
        acc = jnp.zeros((GRID_W, V7X_MXU_DIM), F32)
        for i in range(heads_per_slab):
            acc = acc + jnp.where(lane_head == i, o[i * GRID_W:(i + 1) * GRID_W], 0.0)
        o_ref[0, :, cs] = acc.astype(o_ref.dtype)


def _na_bias_table(rpb):
    qc = np.arange(GRID_W)[:, None]
    kc = np.arange(GRID_W)[None, :]
    cstart = np.clip(qc - NA_WIN_W // 2, 0, GRID_W - NA_WIN_W)
    col_in = (kc >= cstart) & (kc < cstart + NA_WIN_W)
    dcol = np.clip(kc - qc, 1 - NA_WIN_W, NA_WIN_W - 1) + NA_WIN_W - 1
    drow = np.arange(NA_WIN_H)[:, None] + np.arange(NA_WIN_H)[None, :]
    tbl = rpb.astype(F32)[:, drow[:, :, None, None], dcol[None, None]]
    tbl = jnp.where(col_in[None, None, None], tbl, -jnp.inf)
    tbl = tbl.transpose(1, 0, 3, 2, 4)
    return tbl.reshape(NA_WIN_H, NA_HEADS * GRID_W, NA_WIN_H * GRID_W)


def _na_attention(u, rpb):
    b, t, _ = u.shape
    rows = t // GRID_W
    bias = _na_bias_table(rpb)

    def bias_idx(bi, r):
        return (jnp.clip(r - NA_WIN_H // 2, 0, rows - NA_WIN_H) - r + NA_WIN_H - 1, 0, 0)

    return pl.pallas_call(
        _na_kernel,
        grid=(b, rows),
        in_specs=[pl.BlockSpec((1, GRID_W, MIX_DIM), lambda bi, r: (bi, r, 0)),
                  pl.BlockSpec((1, t, MIX_DIM), lambda bi, r: (bi, 0, 1)),
                  pl.BlockSpec((1, t, MIX_DIM), lambda bi, r: (bi, 0, 2)),
                  pl.BlockSpec((1, NA_HEADS * GRID_W, NA_WIN_H * GRID_W), bias_idx)],
        out_specs=pl.BlockSpec((1, GRID_W, MIX_DIM), lambda bi, r: (bi, r, 0)),
        out_shape=jax.ShapeDtypeStruct((b, t, MIX_DIM), BF16),
        compiler_params=_cparams(("parallel", "arbitrary")),
        name="na_attention",
    )(u, u, u, bias)


def _gla_kernel(*refs, reverse, final):
    if final:
        (q_ref, k_ref, v_ref, r_ref, gu_ref, gb_ref, ofwd_ref, g_ref, onorm_ref, o_ref, st_ref) = refs
    else:
        (q_ref, k_ref, v_ref, r_ref, gu_ref, gb_ref, o_ref, st_ref) = refs
    c = GLA_CHUNK

    @pl.when(pl.program_id(1) == 0)
    def _():
        st_ref[...] = jnp.zeros_like(st_ref)

    z = jnp.dot(r_ref[0], gu_ref[...], preferred_element_type=F32) + gb_ref[...]
    la = (jnp.minimum(z, 0.0) - jnp.log(1.0 + jnp.exp(-jnp.abs(z)))) * (1.0 / GLA_TAU)
    ri = lax.broadcasted_iota(jnp.int32, (c, c), 0)
    ci = lax.broadcasted_iota(jnp.int32, (c, c), 1)
    tri = (ci >= ri) if reverse else (ci <= ri)
    trif = tri.astype(F32)
    mid = c // 2 if reverse else c // 2 - 1
    last = 0 if reverse else c - 1
    scale = GLA_DK ** -0.5
    n_chunks = q_ref.shape[1] // c
    order = range(n_chunks - 1, -1, -1) if reverse else range(n_chunks)
    for ch in order:
        sl = slice(ch * c, (ch + 1) * c)
        bcum = jnp.dot(trif, la[sl], precision=lax.Precision.HIGHEST, preferred_element_type=F32)
        b_mid = bcum[mid:mid + 1]
        b_last = bcum[last:last + 1]
        qc = q_ref[0, sl, :].astype(F32) * scale
        kc = k_ref[0, sl, :].astype(F32)
        vc = v_ref[0, sl, :]
        qe = (qc * jnp.exp(bcum - b_mid)).astype(BF16)
        ke = (kc * jnp.exp(b_mid - bcum)).astype(BF16)
        qs = (qc * jnp.exp(bcum)).astype(BF16)
        ks = (kc * jnp.exp(b_last - bcum)).astype(BF16)
        dec = jnp.exp(b_last)
        for h in range(GLA_HEADS):
            hs = slice(h * GLA_DK_PAD, (h + 1) * GLA_DK_PAD)
            vs = slice(h * GLA_DV_PAD, (h + 1) * GLA_DV_PAD)
            a = lax.dot_general(qe[:, hs], ke[:, hs], (((1,), (1,)), ((), ())), preferred_element_type=F32)
            a = jnp.where(tri, a, 0.0).astype(BF16)
            vh = vc[:, vs]
            st = st_ref[h]
            o = jnp.dot(a, vh, preferred_element_type=F32) + lax.dot_general(
                qs[:, hs], st.astype(BF16), (((1,), (1,)), ((), ())), preferred_element_type=F32)
            kv_t = lax.dot_general(vh, ks[:, hs], (((0,), (0,)), ((), ())), preferred_element_type=F32)
            st_ref[h] = st * dec[:, hs] + kv_t
            if final:
                tot = ofwd_ref[0, sl, vs] + o
                ms = jnp.sum(tot * tot, axis=-1, keepdims=True) * (1.0 / GLA_DV)
                y = tot * lax.rsqrt(ms + NORM_EPS) * onorm_ref[:, vs]
                g = g_ref[0, sl, vs].astype(F32)
                o_ref[0, sl, vs] = (y * (g / (1.0 + jnp.exp(-g)))).astype(o_ref.dtype)
            else:
                o_ref[0, sl, vs] = o


def _gla_direction(u, gu, gb, *, reverse, tb, ofwd=None, onorm=None):
    b, t, _ = u.shape
    nt = t // tb
    final = ofwd is not None
    tix = (lambda ti: nt - 1 - ti) if reverse else (lambda ti: ti)
    qw = GLA_HEADS * GLA_DK_PAD
    vw = GLA_HEADS * GLA_DV_PAD
    in_specs = [pl.BlockSpec((1, tb, qw), lambda bi, ti: (bi, tix(ti), GLA_Q_OFF // qw)),
                pl.BlockSpec((1, tb, qw), lambda bi, ti: (bi, tix(ti), GLA_K_OFF // qw)),
                pl.BlockSpec((1, tb, vw), lambda bi, ti: (bi, tix(ti), GLA_V_OFF // vw)),
                pl.BlockSpec((1, tb, GLA_R_PAD), lambda bi, ti: (bi, tix(ti), GLA_R_OFF // GLA_R_PAD)),
                pl.BlockSpec((GLA_R_PAD, qw), lambda bi, ti: (0, 0)),
                pl.BlockSpec((1, qw), lambda bi, ti: (0, 0))]
    args = [u, u, u, u, gu, gb]
    if final:
        in_specs += [pl.BlockSpec((1, tb, vw), lambda bi, ti: (bi, tix(ti), 0)),
                     pl.BlockSpec((1, tb, vw), lambda bi, ti: (bi, tix(ti), GLA_G_OFF // vw)),
                     pl.BlockSpec((1, vw), lambda bi, ti: (0, 0))]
        args += [ofwd, u, onorm]
    return pl.pallas_call(
        functools.partial(_gla_kernel, reverse=reverse, final=final),
        grid=(b, nt),
        in_specs=in_specs,
        out_specs=pl.BlockSpec((1, tb, vw), lambda bi, ti: (bi, tix(ti), 0)),
        out_shape=jax.ShapeDtypeStruct((b, t, vw), BF16 if final else F32),
        scratch_shapes=[pltpu.VMEM((GLA_HEADS, GLA_DV_PAD, GLA_DK_PAD), F32)],
        compiler_params=_cparams(("parallel", "arbitrary")),
        name="gla_bwd_final" if final else "gla_fwd",
    )(*args)


def _gla_weights(w_in, gate_up, gate_bias, out_norm):
    d = w_in.shape[0]
    kd = GLA_HEADS * GLA_DK
    wq, wk, wv, wg, wr, wmq = jnp.split(
        w_in, np.cumsum([kd, kd, MIX_DIM, MIX_DIM, 2 * GLA_GATE_RANK]), axis=1)

    def pad_heads(w, dh, dh_pad):
        w = w.reshape(w.shape[0], GLA_HEADS, dh)
        w = jnp.pad(w, ((0, 0), (0, 0), (0, dh_pad - dh)))
        return w.reshape(w.shape[0], GLA_HEADS * dh_pad)

    w_pad = jnp.concatenate([
        pad_heads(wq, GLA_DK, GLA_DK_PAD), pad_heads(wk, GLA_DK, GLA_DK_PAD),
        pad_heads(wv, GLA_DV, GLA_DV_PAD), pad_heads(wg, GLA_DV, GLA_DV_PAD),
        jnp.pad(wr, ((0, 0), (0, GLA_R_PAD - 2 * GLA_GATE_RANK))), wmq], axis=1)
    gu = []
    for di in range(2):
        up = pad_heads(gate_up[di], GLA_DK, GLA_DK_PAD)
        gu.append(jnp.pad(up, ((di * GLA_GATE_RANK, GLA_R_PAD - (di + 1) * GLA_GATE_RANK), (0, 0))))
    gb = [pad_heads(gate_bias[di][None, :], GLA_DK, GLA_DK_PAD) for di in range(2)]
    onorm = pad_heads(out_norm[None, :].repeat(GLA_HEADS, 0).reshape(1, MIX_DIM), GLA_DV, GLA_DV_PAD)
    return w_pad.astype(BF16), [g.astype(BF16) for g in gu], [x.astype(F32) for x in gb], onorm.astype(F32)


def _post_kernel(x_ref, mix_ref, mq_ref, kbd_ref, vbd_ref, wom_ref, woa_ref, g_ref, wrt_ref, brt_ref,
                 x1_ref, h2_ref, rt_ref):
    mq = mq_ref[0]
    s = jnp.dot(mq, kbd_ref[0], preferred_element_type=F32) * (HEAD_DIM ** -0.5)
    n_mem = kbd_ref.shape[2] // MEM_HEADS
    ps = []
    for h in range(MEM_HEADS):
        seg = s[:, h * n_mem:(h + 1) * n_mem]
        e = jnp.exp(seg - jnp.max(seg, axis=-1, keepdims=True))
        ps.append((e * (1.0 / jnp.sum(e, axis=-1, keepdims=True))).astype(BF16))
    att = jnp.dot(jnp.concatenate(ps, axis=1), vbd_ref[0], preferred_element_type=F32).astype(BF16)
    x1 = (x_ref[0] + jnp.dot(mix_ref[0], wom_ref[...], preferred_element_type=F32)
          + jnp.dot(att, woa_ref[...], preferred_element_type=F32))
    x1_ref[0] = x1
    ms = jnp.mean(x1 * x1, axis=-1, keepdims=True)
    h2 = x1 * lax.rsqrt(ms + NORM_EPS) * g_ref[...]
    h2_ref[0] = h2.astype(h2_ref.dtype)
    logits = jnp.dot(h2, wrt_ref[...], precision=lax.Precision.HIGHEST,
                     preferred_element_type=F32) + brt_ref[...]
    lane = lax.broadcasted_iota(jnp.int32, logits.shape, 1).astype(F32)
    neg = -jnp.inf
    big = 1e9
    gl = jnp.where(lane < N_GROUPS, logits, neg)
    gm = jnp.max(gl, axis=-1, keepdims=True)
    g_gate = 1.0 / jnp.sum(jnp.exp(gl - gm), axis=-1, keepdims=True)
    g_idx = jnp.min(jnp.where(gl == gm, lane, big), axis=-1, keepdims=True)
    lo = N_GROUPS + EXPERTS_PER_GROUP * g_idx
    v1 = jnp.where((lane >= lo) & (lane < lo + EXPERTS_PER_GROUP), logits, neg)
    m1 = jnp.max(v1, axis=-1, keepdims=True)
    i1 = jnp.min(jnp.where(v1 == m1, lane, big), axis=-1, keepdims=True)
    v2 = jnp.where(lane == i1, neg, v1)
    m2 = jnp.max(v2, axis=-1, keepdims=True)
    i2 = jnp.min(jnp.where(v2 == m2, lane, big), axis=-1, keepdims=True)
    t = jnp.exp(m2 - m1)
    w1 = g_gate / (1.0 + t)
    w2 = g_gate * t / (1.0 + t)
    rt = jnp.where(lane == 0, i1 - N_GROUPS,
                   jnp.where(lane == 1, i2 - N_GROUPS,
                             jnp.where(lane == 2, w1, jnp.where(lane == 3, w2, 0.0))))
    rt_ref[0] = rt


def _post_mixer(x, mix, u, mq_off, kbd, vbd, w_out, g, w_rt, b_rt, *, tm):
    b, t, d = x.shape
    mixw = mix.shape[2]
    wom = w_out[0]
    woa = w_out[1]
    return pl.pallas_call(
        _post_kernel,
        grid=(b, t // tm),
        in_specs=[pl.BlockSpec((1, tm, d), lambda bi, ti: (bi, ti, 0)),
                  pl.BlockSpec((1, tm, mixw), lambda bi, ti: (bi, ti, 0)),
                  pl.BlockSpec((1, tm, MEM_DIM), lambda bi, ti: (bi, ti, mq_off // MEM_DIM)),
                  pl.BlockSpec((1,) + kbd.shape[1:], lambda bi, ti: (bi, 0, 0)),
                  pl.BlockSpec((1,) + vbd.shape[1:], lambda bi, ti: (bi, 0, 0)),
                  pl.BlockSpec(wom.shape, lambda bi, ti: (0, 0)),
                  pl.BlockSpec(woa.shape, lambda bi, ti: (0, 0)),
                  pl.BlockSpec((1, d), lambda bi, ti: (0, 0)),
                  pl.BlockSpec(w_rt.shape, lambda bi, ti: (0, 0)),
                  pl.BlockSpec((1, V7X_LANES), lambda bi, ti: (0, 0))],
        out_specs=[pl.BlockSpec((1, tm, d), lambda bi, ti: (bi, ti, 0)),
                   pl.BlockSpec((1, tm, d), lambda bi, ti: (bi, ti, 0)),
                   pl.BlockSpec((1, tm, V7X_LANES), lambda bi, ti: (bi, ti, 0))],
        out_shape=[jax.ShapeDtypeStruct((b, t, d), F32),
                   jax.ShapeDtypeStruct((b, t, d), BF16),
                   jax.ShapeDtypeStruct((b, t, V7X_LANES), F32)],
        compiler_params=_cparams(("parallel", "parallel")),
        name="post_mixer",
    )(x, mix, u, kbd, vbd, wom, woa, g.reshape(1, d), w_rt, b_rt)


def _memory_kv_blockdiag(mkv):
    k, v = jnp.split(mkv, 2, axis=-1)
    head_of = np.arange(MEM_DIM) // HEAD_DIM
    sel = jnp.asarray(head_of[None, :] == np.arange(MEM_HEADS)[:, None], F32)
    kbd = jnp.einsum('bmd,hd->bdhm', k, sel).reshape(k.shape[0], MEM_DIM, -1)
    vbd = jnp.einsum('bmd,hd->bhmd', v, sel).reshape(v.shape[0], -1, MEM_DIM)
    return kbd.astype(BF16), vbd.astype(BF16)


def _moe_kernel(bexp_ref, bvalid_ref, bfirst_ref, xs_ref, w1_ref, w3_ref, w2_ref, y_ref, wb1, wb3, wb2):
    blk = pl.program_id(0)

    @pl.when(bvalid_ref[blk] == 0)
    def _():
        y_ref[...] = jnp.zeros_like(y_ref)

    @pl.when(bvalid_ref[blk] != 0)
    def _():
        @pl.when(bfirst_ref[blk] != 0)
        def _():
            wb1[...] = w1_ref[0].astype(BF16)
            wb3[...] = w3_ref[0].astype(BF16)
            wb2[...] = w2_ref[0].astype(BF16)

        xb = xs_ref[...]
        a = jnp.dot(xb, wb1[...], preferred_element_type=F32)
        c = jnp.dot(xb, wb3[...], preferred_element_type=F32)
        hid = (a / (1.0 + jnp.exp(-a)) * c).astype(BF16)
        y_ref[...] = jnp.dot(hid, wb2[...], preferred_element_type=F32)


def _moe_ffn(xs, block_exp, block_valid, block_first, w1, w3, w2):
    n_rows, d = xs.shape
    n_blocks = n_rows // MOE_BLOCK
    ff = w1.shape[2]
    grid_spec = pltpu.PrefetchScalarGridSpec(
        num_scalar_prefetch=3,
        grid=(n_blocks,),
        in_specs=[pl.BlockSpec((MOE_BLOCK, d), lambda i, be, bv, bf: (i, 0)),
                  pl.BlockSpec((1, d, ff), lambda i, be, bv, bf: (be[i], 0, 0)),
                  pl.BlockSpec((1, d, ff), lambda i, be, bv, bf: (be[i], 0, 0)),
                  pl.BlockSpec((1, ff, d), lambda i, be, bv, bf: (be[i], 0, 0))],
        out_specs=pl.BlockSpec((MOE_BLOCK, d), lambda i, be, bv, bf: (i, 0)),
        scratch_shapes=[pltpu.VMEM((d, ff), BF16), pltpu.VMEM((d, ff), BF16), pltpu.VMEM((ff, d), BF16)],
    )
    return pl.pallas_call(
        _moe_kernel,
        grid_spec=grid_spec,
        out_shape=jax.ShapeDtypeStruct((n_rows, d), F32),
        compiler_params=_cparams(("arbitrary",)),
        name="moe_ffn",
    )(block_exp, block_valid, block_first, xs, w1, w3, w2)


def _moe_layer(x1, h2, rt, w1, w3, w2):
    n_tok, d = x1.shape
    expert = rt[:, 0:TOP_K].astype(jnp.int32)
    gate = rt[:, TOP_K:2 * TOP_K]
    n_asg = n_tok * TOP_K
    a_exp = expert.reshape(-1)
    oh = (a_exp[:, None] == jnp.arange(N_EXPERTS, dtype=jnp.int32)[None, :]).astype(jnp.int32)
    csum = jnp.cumsum(oh, axis=0)
    rank = jnp.sum((csum - oh) * oh, axis=1)
    counts = csum[-1]
    padded = (counts + MOE_BLOCK - 1) // MOE_BLOCK * MOE_BLOCK
    pend = jnp.cumsum(padded)
    pstart = pend - padded
    dest = pstart[a_exp] + rank
    n_blocks = -(-n_asg // MOE_BLOCK) + N_EXPERTS
    n_rows = n_blocks * MOE_BLOCK
    a_tok = jnp.repeat(jnp.arange(n_tok, dtype=jnp.int32), TOP_K)
    row_tok = jnp.full((n_rows,), n_tok, jnp.int32).at[dest].set(a_tok)
    block_start = jnp.arange(n_blocks, dtype=jnp.int32) * MOE_BLOCK
    block_exp = jnp.minimum(jnp.sum(block_start[:, None] >= pend[None, :], axis=1), N_EXPERTS - 1).astype(jnp.int32)
    block_valid = (block_start < pend[-1]).astype(jnp.int32)
    block_first = jnp.concatenate([jnp.ones((1,), jnp.int32),
                                   (block_exp[1:] != block_exp[:-1]).astype(jnp.int32)])
    h2_pad = jnp.concatenate([h2, jnp.zeros((1, d), h2.dtype)], axis=0)
    xs = h2_pad[row_tok]
    yb = _moe_ffn(xs, block_exp, block_valid, block_first, w1, w3, w2)
    dest2 = dest.reshape(n_tok, TOP_K)
    y = yb[dest2[:, 0]] * gate[:, 0:1] + yb[dest2[:, 1]] * gate[:, 1:2]
    return x1 + y


def _final_norm_kernel(x_ref, g_ref, o_ref):
    x = x_ref[...]
    ms = jnp.mean(x * x, axis=-1, keepdims=True)
    o_ref[...] = x * lax.rsqrt(ms + NORM_EPS) * g_ref[...]


def _final_norm(x, g, *, tm):
    n, d = x.shape
    return pl.pallas_call(
        _final_norm_kernel,
        grid=(n // tm,),
        in_specs=[pl.BlockSpec((tm, d), lambda i: (i, 0)), pl.BlockSpec((1, d), lambda i: (0, 0))],
        out_specs=pl.BlockSpec((tm, d), lambda i: (i, 0)),
        out_shape=jax.ShapeDtypeStruct((n, d), F32),
        compiler_params=_cparams(("parallel",)),
        name="final_norm",
    )(x, g.reshape(1, d))


def kernel(x, mem, mem_norm, final_norm, norm_mix, norm_ffn, w_mem_kv, w_out, na_w_in, na_rpb, gla_w_in,
           gla_gate_up, gla_gate_bias, gla_out_norm, moe_w_group, moe_b_group, moe_w_router, moe_b_router,
           moe_w1, moe_w3, moe_w2):
    b, t, d = x.shape
    n = b * t
    depth = norm_mix.shape[0]
    n_mem = mem.shape[1]
    xf = x.reshape(n, d)
    for i in range(depth):
        j = i // 2
        mkv = _norm_matmul(mem.reshape(b * n_mem, d), mem_norm, w_mem_kv[i].astype(BF16),
                           tm=256, out_dtype=F32, name="mem_kv_proj").reshape(b, n_mem, 2 * MEM_DIM)
        kbd, vbd = _memory_kv_blockdiag(mkv)
        if i % 2 == 0:
            u = _norm_matmul(xf, norm_mix[i], na_w_in[j].astype(BF16), tm=512, out_dtype=BF16,
                             name="na_in_proj").reshape(b, t, -1)
            mix = _na_attention(u, na_rpb[j])
            mq_off = NA_MQ_OFF
            w_o = w_out[i].astype(BF16)
            w_o = (w_o[:MIX_DIM], w_o[MIX_DIM:])
        else:
            w_pad, gu, gb, onorm = _gla_weights(gla_w_in[j], gla_gate_up[j], gla_gate_bias[j], gla_out_norm[j])
            u = _norm_matmul(xf, norm_mix[i], w_pad, tm=512, out_dtype=BF16, name="gla_in_proj").reshape(b, t, -1)
            ofwd = _gla_direction(u, gu[0], gb[0], reverse=False, tb=512)
            mix = _gla_direction(u, gu[1], gb[1], reverse=True, tb=512, ofwd=ofwd, onorm=onorm)
            mq_off = GLA_MQ_OFF
            w_mix = w_out[i][:MIX_DIM].reshape(GLA_HEADS, GLA_DV, d)
            w_mix = jnp.pad(w_mix, ((0, 0), (0, GLA_DV_PAD - GLA_DV), (0, 0))).reshape(GLA_MIX_PAD, d)
            w_o = (w_mix.astype(BF16), w_out[i][MIX_DIM:].astype(BF16))
        w_rt = jnp.pad(jnp.concatenate([moe_w_group[i], moe_w_router[i]], axis=1),
                       ((0, 0), (0, V7X_LANES - N_GROUPS - N_EXPERTS)))
        b_rt = jnp.pad(jnp.concatenate([moe_b_group[i], moe_b_router[i]]),
                       (0, V7X_LANES - N_GROUPS - N_EXPERTS)).reshape(1, V7X_LANES)
        x1, h2, rt = _post_mixer(xf.reshape(b, t, d), mix, u, mq_off, kbd, vbd, w_o, norm_ffn[i],
                                 w_rt.astype(F32), b_rt.astype(F32), tm=256)
        xf = _moe_layer(x1.reshape(n, d), h2.reshape(n, d), rt.reshape(n, V7X_LANES),
                        moe_w1[i], moe_w3[i], moe_w2[i])
    return _final_norm(xf, final_norm, tm=512).reshape(b, t, d)
```

```python
import functools

import numpy as np
import jax
import jax.numpy as jnp
from jax import lax
from jax.experimental import pallas as pl
from jax.experimental.pallas import tpu as pltpu

F32 = jnp.float32
BF16 = jnp.bfloat16

D_MODEL = 1024
GRID_W = 64
HEAD_DIM = 64
NORM_EPS = 1e-6
MEM_HEADS = 4
MEM_DIM = MEM_HEADS * HEAD_DIM
MIX_DIM = D_MODEL - MEM_DIM
NA_HEADS = MIX_DIM // HEAD_DIM
NA_WIN_H = 8
NA_WIN_W = 16
GLA_HEADS = 4
GLA_DK = MIX_DIM // 2 // GLA_HEADS
GLA_DV = MIX_DIM // GLA_HEADS
GLA_GATE_RANK = 16
GLA_TAU = 16.0
GLA_CHUNK = 64
N_GROUPS = 4
EXPERTS_PER_GROUP = 8
N_EXPERTS = N_GROUPS * EXPERTS_PER_GROUP
TOP_K = 2
EXPERT_FF = 512
MOE_BLOCK = 256

V7X_LANES = 128
V7X_MXU_DIM = 256
V7X_VMEM_LIMIT_BYTES = 56 * 1024 * 1024

GLA_DK_PAD = 128
GLA_DV_PAD = 256
GLA_Q_OFF = 0
GLA_K_OFF = GLA_Q_OFF + GLA_HEADS * GLA_DK_PAD
GLA_V_OFF = GLA_K_OFF + GLA_HEADS * GLA_DK_PAD
GLA_G_OFF = GLA_V_OFF + GLA_HEADS * GLA_DV_PAD
GLA_R_OFF = GLA_G_OFF + GLA_HEADS * GLA_DV_PAD
GLA_R_PAD = 256
GLA_MQ_OFF = GLA_R_OFF + GLA_R_PAD
GLA_IN_PAD = GLA_MQ_OFF + MEM_DIM
GLA_MIX_PAD = GLA_HEADS * GLA_DV_PAD
NA_MQ_OFF = 3 * MIX_DIM


def _cparams(semantics):
    return pltpu.CompilerParams(dimension_semantics=semantics, vmem_limit_bytes=V7X_VMEM_LIMIT_BYTES)


def _norm_matmul_kernel(x_ref, g_ref, w_ref, o_ref, *, col_chunk):
    x = x_ref[...]
    ms = jnp.mean(x * x, axis=-1, keepdims=True)
    y = (x * lax.rsqrt(ms + NORM_EPS) * g_ref[...]).astype(BF16)
    n_out = o_ref.shape[1]
    for c in range(0, n_out, col_chunk):
        o_ref[:, c:c + col_chunk] = jnp.dot(
            y, w_ref[:, c:c + col_chunk], preferred_element_type=F32).astype(o_ref.dtype)


def _norm_matmul(x, g, w, *, tm, out_dtype, name):
    n, d = x.shape
    n_out = w.shape[1]
    col_chunk = 512 if n_out % 512 == 0 else n_out
    return pl.pallas_call(
        functools.partial(_norm_matmul_kernel, col_chunk=col_chunk),
        grid=(n // tm,),
        in_specs=[pl.BlockSpec((tm, d), lambda i: (i, 0)),
                  pl.BlockSpec((1, d), lambda i: (0, 0)),
                  pl.BlockSpec((d, n_out), lambda i: (0, 0))],
        out_specs=pl.BlockSpec((tm, n_out), lambda i: (i, 0)),
        out_shape=jax.ShapeDtypeStruct((n, n_out), out_dtype),
        compiler_params=_cparams(("parallel",)),
        name=name,
    )(x, g.reshape(1, d), w)


def _na_kernel(q_ref, k_ref, v_ref, bias_ref, o_ref):
    r = pl.program_id(1)
    rows = k_ref.shape[1] // GRID_W
    rs = jnp.clip(r - NA_WIN_H // 2, 0, rows - NA_WIN_H)
    start = pl.multiple_of(rs * GRID_W, GRID_W)
    n_keys = NA_WIN_H * GRID_W
    heads_per_slab = V7X_MXU_DIM // HEAD_DIM
    scale = HEAD_DIM ** -0.5
    lane_head = lax.broadcasted_iota(jnp.int32, (GRID_W, V7X_MXU_DIM), 1) // HEAD_DIM
    for s in range(MIX_DIM // V7X_MXU_DIM):
        cs = slice(s * V7X_MXU_DIM, (s + 1) * V7X_MXU_DIM)
        qq = q_ref[0, :, cs] * scale
        kw = k_ref[0, pl.ds(start, n_keys), cs]
        vw = v_ref[0, pl.ds(start, n_keys), cs]
        lhs = jnp.concatenate(
            [jnp.where(lane_head == i, qq, jnp.zeros_like(qq)) for i in range(heads_per_slab)], axis=0)
        sc = lax.dot_general(lhs, kw, (((1,), (1,)), ((), ())), preferred_element_type=F32)
        sc = sc + bias_ref[0, s * V7X_MXU_DIM:(s + 1) * V7X_MXU_DIM, :]
        m = jnp.max(sc, axis=-1, keepdims=True)
        p = jnp.exp(sc - m)
        l = jnp.sum(p, axis=-1, keepdims=True)
        o = jnp.dot(p.astype(BF16), vw, preferred_element_type=F32)
        o = o * (1.0 / l)
        acc = jnp.zeros((GRID_W, V7X_MXU_DIM), F32)
        for i in range(heads_per_slab):
            acc = acc + jnp.where(lane_head == i, o[i * GRID_W:(i + 1) * GRID_W], 0.0)
        o_ref[0, :, cs] = acc.astype(o_ref.dtype)


def _na_bias_table(rpb):
    qc = np.arange(GRID_W)[:, None]
    kc = np.arange(GRID_W)[None, :]
    cstart = np.clip(qc - NA_WIN_W // 2, 0, GRID_W - NA_WIN_W)
    col_in = (kc >= cstart) & (kc < cstart + NA_WIN_W)
    dcol = np.clip(kc - qc, 1 - NA_WIN_W, NA_WIN_W - 1) + NA_WIN_W - 1
    drow = np.arange(NA_WIN_H)[:, None] + np.arange(NA_WIN_H)[None, :]
    tbl = rpb.astype(F32)[:, drow[:, :, None, None], dcol[None, None]]
    tbl = jnp.where(col_in[None, None, None], tbl, -jnp.inf)
    tbl = tbl.transpose(1, 0, 3, 2, 4)
    return tbl.reshape(NA_WIN_H, NA_HEADS * GRID_W, NA_WIN_H * GRID_W)


def _na_attention(u, rpb):
    b, t, _ = u.shape
    rows = t // GRID_W
    bias = _na_bias_table(rpb)

    def bias_idx(bi, r):
        return (jnp.clip(r - NA_WIN_H // 2, 0, rows - NA_WIN_H) - r + NA_WIN_H - 1, 0, 0)

    return pl.pallas_call(
        _na_kernel,
        grid=(b, rows),
        in_specs=[pl.BlockSpec((1, GRID_W, MIX_DIM), lambda bi, r: (bi, r, 0)),
                  pl.BlockSpec((1, t, MIX_DIM), lambda bi, r: (bi, 0, 1)),
                  pl.BlockSpec((1, t, MIX_DIM), lambda bi, r: (bi, 0, 2)),
                  pl.BlockSpec((1, NA_HEADS * GRID_W, NA_WIN_H * GRID_W), bias_idx)],
        out_specs=pl.BlockSpec((1, GRID_W, MIX_DIM), lambda bi, r: (bi, r, 0)),
        out_shape=jax.ShapeDtypeStruct((b, t, MIX_DIM), BF16),
        compiler_params=_cparams(("parallel", "arbitrary")),
        name="na_attention",
    )(u, u, u, bias)


def _gla_kernel(*refs, reverse, final):
    if final:
        (q_ref, k_ref, v_ref, r_ref, gu_ref, gb_ref, ofwd_ref, g_ref, onorm_ref, o_ref, st_ref) = refs
    else:
        (q_ref, k_ref, v_ref, r_ref, gu_ref, gb_ref, o_ref, st_ref) = refs
    c = GLA_CHUNK

    @pl.when(pl.program_id(1) == 0)
    def _():
        st_ref[...] = jnp.zeros_like(st_ref)

    z = jnp.dot(r_ref[0], gu_ref[...], preferred_element_type=F32) + gb_ref[...]
    la = (jnp.minimum(z, 0.0) - jnp.log(1.0 + jnp.exp(-jnp.abs(z)))) * (1.0 / GLA_TAU)
    ri = lax.broadcasted_iota(jnp.int32, (c, c), 0)
    ci = lax.broadcasted_iota(jnp.int32, (c, c), 1)
    tri = (ci >= ri) if reverse else (ci <= ri)
    trif = tri.astype(F32)
    mid = c // 2 if reverse else c // 2 - 1
    last = 0 if reverse else c - 1
    scale = GLA_DK ** -0.5
    n_chunks = q_ref.shape[1] // c
    order = range(n_chunks - 1, -1, -1) if reverse else range(n_chunks)
    for ch in order:
        sl = slice(ch * c, (ch + 1) * c)
        bcum = jnp.dot(trif, la[sl], precision=lax.Precision.HIGHEST, preferred_element_type=F32)
        b_mid = bcum[mid:mid + 1]
        b_last = bcum[last:last + 1]
        qc = q_ref[0, sl, :].astype(F32) * scale
        kc = k_ref[0, sl, :].astype(F32)
        vc = v_ref[0, sl, :]
        qe = (qc * jnp.exp(bcum - b_mid)).astype(BF16)
        ke = (kc * jnp.exp(b_mid - bcum)).astype(BF16)
        qs = (qc * jnp.exp(bcum)).astype(BF16)
        ks = (kc * jnp.exp(b_last - bcum)).astype(BF16)
        dec = jnp.exp(b_last)
        for h in range(GLA_HEADS):
            hs = slice(h * GLA_DK_PAD, (h + 1) * GLA_DK_PAD)
            vs = slice(h * GLA_DV_PAD, (h + 1) * GLA_DV_PAD)
            a = lax.dot_general(qe[:, hs], ke[:, hs], (((1,), (1,)), ((), ())), preferred_element_type=F32)
            a = jnp.where(tri, a, 0.0).astype(BF16)
            vh = vc[:, vs]
            st = st_ref[h]
            o = jnp.dot(a, vh, preferred_element_type=F32) + lax.dot_general(
                qs[:, hs], st.astype(BF16), (((1,), (1,)), ((), ())), preferred_element_type=F32)
            kv_t = lax.dot_general(vh, ks[:, hs], (((0,), (0,)), ((), ())), preferred_element_type=F32)
            st_ref[h] = st * dec[:, hs] + kv_t
            if final:
                tot = ofwd_ref[0, sl, vs] + o
                ms = jnp.sum(tot * tot, axis=-1, keepdims=True) * (1.0 / GLA_DV)
                y = tot * lax.rsqrt(ms + NORM_EPS) * onorm_ref[:, vs]
                g = g_ref[0, sl, vs].astype(F32)
                o_ref[0, sl, vs] = (y * (g / (1.0 + jnp.exp(-g)))).astype(o_ref.dtype)
            else:
                o_ref[0, sl, vs] = o


def _gla_direction(u, gu, gb, *, reverse, tb, ofwd=None, onorm=None):
    b, t, _ = u.shape
    nt = t // tb
    final = ofwd is not None
    tix = (lambda ti: nt - 1 - ti) if reverse else (lambda ti: ti)
    qw = GLA_HEADS * GLA_DK_PAD
    vw = GLA_HEADS * GLA_DV_PAD
    in_specs = [pl.BlockSpec((1, tb, qw), lambda bi, ti: (bi, tix(ti), GLA_Q_OFF // qw)),
                pl.BlockSpec((1, tb, qw), lambda bi, ti: (bi, tix(ti), GLA_K_OFF // qw)),
                pl.BlockSpec((1, tb, vw), lambda bi, ti: (bi, tix(ti), GLA_V_OFF // vw)),
                pl.BlockSpec((1, tb, GLA_R_PAD), lambda bi, ti: (bi, tix(ti), GLA_R_OFF // GLA_R_PAD)),
                pl.BlockSpec((GLA_R_PAD, qw), lambda bi, ti: (0, 0)),
                pl.BlockSpec((1, qw), lambda bi, ti: (0, 0))]
    args = [u, u, u, u, gu, gb]
    if final:
        in_specs += [pl.BlockSpec((1, tb, vw), lambda bi, ti: (bi, tix(ti), 0)),
                     pl.BlockSpec((1, tb, vw), lambda bi, ti: (bi, tix(ti), GLA_G_OFF // vw)),
                     pl.BlockSpec((1, vw), lambda bi, ti: (0, 0))]
        args += [ofwd, u, onorm]
    return pl.pallas_call(
        functools.partial(_gla_kernel, reverse=reverse, final=final),
        grid=(b, nt),
        in_specs=in_specs,
        out_specs=pl.BlockSpec((1, tb, vw), lambda bi, ti: (bi, tix(ti), 0)),
        out_shape=jax.ShapeDtypeStruct((b, t, vw), BF16 if final else F32),
        scratch_shapes=[pltpu.VMEM((GLA_HEADS, GLA_DV_PAD, GLA_DK_PAD), F32)],
        compiler_params=_cparams(("parallel", "arbitrary")),
        name="gla_bwd_final" if final else "gla_fwd",
    )(*args)


def _gla_weights(w_in, gate_up, gate_bias, out_norm):
    d = w_in.shape[0]
    kd = GLA_HEADS * GLA_DK
    wq, wk, wv, wg, wr, wmq = jnp.split(
        w_in, np.cumsum([kd, kd, MIX_DIM, MIX_DIM, 2 * GLA_GATE_RANK]), axis=1)

    def pad_heads(w, dh, dh_pad):
        w = w.reshape(w.shape[0], GLA_HEADS, dh)
        w = jnp.pad(w, ((0, 0), (0, 0), (0, dh_pad - dh)))
        return w.reshape(w.shape[0], GLA_HEADS * dh_pad)

    w_pad = jnp.concatenate([
        pad_heads(wq, GLA_DK, GLA_DK_PAD), pad_heads(wk, GLA_DK, GLA_DK_PAD),
        pad_heads(wv, GLA_DV, GLA_DV_PAD), pad_heads(wg, GLA_DV, GLA_DV_PAD),
        jnp.pad(wr, ((0, 0), (0, GLA_R_PAD - 2 * GLA_GATE_RANK))), wmq], axis=1)
    gu = []
    for di in range(2):
        up = pad_heads(gate_up[di], GLA_DK, GLA_DK_PAD)
        gu.append(jnp.pad(up, ((di * GLA_GATE_RANK, GLA_R_PAD - (di + 1) * GLA_GATE_RANK), (0, 0))))
    gb = [pad_heads(gate_bias[di][None, :], GLA_DK, GLA_DK_PAD) for di in range(2)]
    onorm = pad_heads(out_norm[None, :].repeat(GLA_HEADS, 0).reshape(1, MIX_DIM), GLA_DV, GLA_DV_PAD)
    return w_pad.astype(BF16), [g.astype(BF16) for g in gu], [x.astype(F32) for x in gb], onorm.astype(F32)


def _post_kernel(x_ref, mix_ref, mq_ref, kbd_ref, vbd_ref, wom_ref, woa_ref, g_ref, wrt_ref, brt_ref,
                 x1_ref, h2_ref, rt_ref, cnt_ref, base_ref):
    @pl.when((pl.program_id(0) == 0) & (pl.program_id(1) == 0))
    def _():
        base_ref[...] = jnp.zeros_like(base_ref)

    mq = mq_ref[0]
    s = jnp.dot(mq, kbd_ref[0], preferred_element_type=F32) * (HEAD_DIM ** -0.5)
    n_mem = kbd_ref.shape[2] // MEM_HEADS
    ps = []
    for h in range(MEM_HEADS):
        seg = s[:, h * n_mem:(h + 1) * n_mem]
        e = jnp.exp(seg - jnp.max(seg, axis=-1, keepdims=True))
        ps.append((e * (1.0 / jnp.sum(e, axis=-1, keepdims=True))).astype(BF16))
    att = jnp.dot(jnp.concatenate(ps, axis=1), vbd_ref[0], preferred_element_type=F32).astype(BF16)
    x1 = (x_ref[0] + jnp.dot(mix_ref[0], wom_ref[...], preferred_element_type=F32)
          + jnp.dot(att, woa_ref[...], preferred_element_type=F32))
    x1_ref[0] = x1
    ms = jnp.mean(x1 * x1, axis=-1, keepdims=True)
    h2 = x1 * lax.rsqrt(ms + NORM_EPS) * g_ref[...]
    h2_ref[0] = h2.astype(h2_ref.dtype)
    logits = jnp.dot(h2, wrt_ref[...], precision=lax.Precision.HIGHEST,
                     preferred_element_type=F32) + brt_ref[...]
    lane = lax.broadcasted_iota(jnp.int32, logits.shape, 1).astype(F32)
    neg = -jnp.inf
    big = 1e9
    gl = jnp.where(lane < N_GROUPS, logits, neg)
    gm = jnp.max(gl, axis=-1, keepdims=True)
    g_gate = 1.0 / jnp.sum(jnp.exp(gl - gm), axis=-1, keepdims=True)
    g_idx = jnp.min(jnp.where(gl == gm, lane, big), axis=-1, keepdims=True)
    lo = N_GROUPS + EXPERTS_PER_GROUP * g_idx
    v1 = jnp.where((lane >= lo) & (lane < lo + EXPERTS_PER_GROUP), logits, neg)
    m1 = jnp.max(v1, axis=-1, keepdims=True)
    i1 = jnp.min(jnp.where(v1 == m1, lane, big), axis=-1, keepdims=True)
    v2 = jnp.where(lane == i1, neg, v1)
    m2 = jnp.max(v2, axis=-1, keepdims=True)
    i2 = jnp.min(jnp.where(v2 == m2, lane, big), axis=-1, keepdims=True)
    t = jnp.exp(m2 - m1)
    w1 = g_gate / (1.0 + t)
    w2 = g_gate * t / (1.0 + t)
    e1 = i1 - N_GROUPS
    e2 = i2 - N_GROUPS
    tm = logits.shape[0]
    onehot = jnp.where(lane == e1, 1.0, 0.0) + jnp.where(lane == e2, 1.0, 0.0)
    ri = lax.broadcasted_iota(jnp.int32, (tm, tm), 0)
    ci = lax.broadcasted_iota(jnp.int32, (tm, tm), 1)
    before = jnp.where(ci < ri, 1.0, 0.0).astype(BF16)
    ahead = jnp.dot(before, onehot.astype(BF16), preferred_element_type=F32) + base_ref[...]
    pos1 = jnp.sum(jnp.where(lane == e1, ahead, 0.0), axis=-1, keepdims=True)
    pos2 = jnp.sum(jnp.where(lane == e2, ahead, 0.0), axis=-1, keepdims=True)
    base_ref[...] = base_ref[...] + jnp.sum(onehot, axis=0, keepdims=True)
    cnt_ref[...] = base_ref[...]
    rt = jnp.where(lane == 0, e1,
                   jnp.where(lane == 1, e2,
                             jnp.where(lane == 2, w1,
                                       jnp.where(lane == 3, w2,
                                                 jnp.where(lane == 4, pos1, jnp.where(lane == 5, pos2, 0.0))))))
    rt_ref[0] = rt


def _post_mixer(x, mix, u, mq_off, kbd, vbd, w_out, g, w_rt, b_rt, *, tm):
    b, t, d = x.shape
    mixw = mix.shape[2]
    wom = w_out[0]
    woa = w_out[1]
    return pl.pallas_call(
        _post_kernel,
        grid=(b, t // tm),
        in_specs=[pl.BlockSpec((1, tm, d), lambda bi, ti: (bi, ti, 0)),
                  pl.BlockSpec((1, tm, mixw), lambda bi, ti: (bi, ti, 0)),
                  pl.BlockSpec((1, tm, MEM_DIM), lambda bi, ti: (bi, ti, mq_off // MEM_DIM)),
                  pl.BlockSpec((1,) + kbd.shape[1:], lambda bi, ti: (bi, 0, 0)),
                  pl.BlockSpec((1,) + vbd.shape[1:], lambda bi, ti: (bi, 0, 0)),
                  pl.BlockSpec(wom.shape, lambda bi, ti: (0, 0)),
                  pl.BlockSpec(woa.shape, lambda bi, ti: (0, 0)),
                  pl.BlockSpec((1, d), lambda bi, ti: (0, 0)),
                  pl.BlockSpec(w_rt.shape, lambda bi, ti: (0, 0)),
                  pl.BlockSpec((1, V7X_LANES), lambda bi, ti: (0, 0))],
        out_specs=[pl.BlockSpec((1, tm, d), lambda bi, ti: (bi, ti, 0)),
                   pl.BlockSpec((1, tm, d), lambda bi, ti: (bi, ti, 0)),
                   pl.BlockSpec((1, tm, V7X_LANES), lambda bi, ti: (bi, ti, 0)),
                   pl.BlockSpec((1, V7X_LANES), lambda bi, ti: (0, 0))],
        out_shape=[jax.ShapeDtypeStruct((b, t, d), F32),
                   jax.ShapeDtypeStruct((b, t, d), F32),
                   jax.ShapeDtypeStruct((b, t, V7X_LANES), F32),
                   jax.ShapeDtypeStruct((1, V7X_LANES), F32)],
        scratch_shapes=[pltpu.VMEM((1, V7X_LANES), F32)],
        compiler_params=_cparams(("arbitrary", "arbitrary")),
        name="post_mixer",
    )(x, mix, u, kbd, vbd, wom, woa, g.reshape(1, d), w_rt, b_rt)


def _memory_kv_blockdiag(mkv):
    k, v = jnp.split(mkv, 2, axis=-1)
    head_of = np.arange(MEM_DIM) // HEAD_DIM
    sel = jnp.asarray(head_of[None, :] == np.arange(MEM_HEADS)[:, None], F32)
    kbd = jnp.einsum('bmd,hd->bdhm', k, sel).reshape(k.shape[0], MEM_DIM, -1)
    vbd = jnp.einsum('bmd,hd->bhmd', v, sel).reshape(v.shape[0], -1, MEM_DIM)
    return kbd.astype(BF16), vbd.astype(BF16)


def _moe_kernel(bexp_ref, bvalid_ref, bfirst_ref, xs_ref, w1_ref, w3_ref, w2_ref, y_ref, wb1, wb3, wb2):
    blk = pl.program_id(0)

    @pl.when(bvalid_ref[blk] == 0)
    def _():
        y_ref[...] = jnp.zeros_like(y_ref)

    @pl.when(bvalid_ref[blk] != 0)
    def _():
        @pl.when(bfirst_ref[blk] != 0)
        def _():
            wb1[...] = w1_ref[0].astype(BF16)
            wb3[...] = w3_ref[0].astype(BF16)
            wb2[...] = w2_ref[0].astype(BF16)

        xb = xs_ref[...].astype(BF16)
        a = jnp.dot(xb, wb1[...], preferred_element_type=F32)
        c = jnp.dot(xb, wb3[...], preferred_element_type=F32)
        hid = (a / (1.0 + jnp.exp(-a)) * c).astype(BF16)
        y_ref[...] = jnp.dot(hid, wb2[...], preferred_element_type=F32)


def _moe_ffn(xs, block_exp, block_valid, block_first, w1, w3, w2):
    n_rows, d = xs.shape
    n_blocks = n_rows // MOE_BLOCK
    ff = w1.shape[2]
    grid_spec = pltpu.PrefetchScalarGridSpec(
        num_scalar_prefetch=3,
        grid=(n_blocks,),
        in_specs=[pl.BlockSpec((MOE_BLOCK, d), lambda i, be, bv, bf: (i, 0)),
                  pl.BlockSpec((1, d, ff), lambda i, be, bv, bf: (be[i], 0, 0)),
                  pl.BlockSpec((1, d, ff), lambda i, be, bv, bf: (be[i], 0, 0)),
                  pl.BlockSpec((1, ff, d), lambda i, be, bv, bf: (be[i], 0, 0))],
        out_specs=pl.BlockSpec((MOE_BLOCK, d), lambda i, be, bv, bf: (i, 0)),
        scratch_shapes=[pltpu.VMEM((d, ff), BF16), pltpu.VMEM((d, ff), BF16), pltpu.VMEM((ff, d), BF16)],
    )
    return pl.pallas_call(
        _moe_kernel,
        grid_spec=grid_spec,
        out_shape=jax.ShapeDtypeStruct((n_rows, d), F32),
        compiler_params=_cparams(("arbitrary",)),
        name="moe_ffn",
    )(block_exp, block_valid, block_first, xs, w1, w3, w2)


def _dispatch_kernel(dest_ref, h_ref, xs_in_ref, xs_ref, sem):
    del xs_in_ref
    tm = h_ref.shape[0]

    def row_copy(tok, slot):
        return pltpu.make_async_copy(h_ref.at[pl.ds(tok, 1)],
                                     xs_ref.at[pl.ds(dest_ref[0, 0, TOP_K * tok + slot], 1)], sem)

    def issue(tok, carry):
        for slot in range(TOP_K):
            row_copy(tok, slot).start()
        return carry

    def drain(tok, carry):
        for slot in range(TOP_K):
            row_copy(tok, slot).wait()
        return carry

    lax.fori_loop(0, tm, issue, 0, unroll=8)
    lax.fori_loop(0, tm, drain, 0, unroll=8)


def _dispatch(h2, dest, n_rows, *, tm):
    n_tok, d = h2.shape
    nt = n_tok // tm
    return pl.pallas_call(
        _dispatch_kernel,
        grid=(nt,),
        in_specs=[pl.BlockSpec((1, 1, TOP_K * tm), lambda i: (i, 0, 0), memory_space=pltpu.SMEM),
                  pl.BlockSpec((tm, d), lambda i: (i, 0)),
                  pl.BlockSpec(memory_space=pl.ANY)],
        out_specs=pl.BlockSpec(memory_space=pl.ANY),
        out_shape=jax.ShapeDtypeStruct((n_rows, d), h2.dtype),
        scratch_shapes=[pltpu.SemaphoreType.DMA(())],
        input_output_aliases={2: 0},
        compiler_params=_cparams(("arbitrary",)),
        name="moe_dispatch",
    )(dest.reshape(nt, 1, TOP_K * tm), h2, jnp.zeros((n_rows, d), h2.dtype))


def _combine_kernel(*refs, final):
    if final:
        dest_ref, x1_ref, rt_ref, yb_ref, g_ref, o_ref, ybuf, sem = refs
    else:
        dest_ref, x1_ref, rt_ref, yb_ref, o_ref, ybuf, sem = refs
    tm = x1_ref.shape[0]

    def row_copy(tok, slot):
        return pltpu.make_async_copy(yb_ref.at[pl.ds(dest_ref[0, 0, TOP_K * tok + slot], 1)],
                                     ybuf.at[slot, pl.ds(tok, 1)], sem)

    def issue(tok, carry):
        for slot in range(TOP_K):
            row_copy(tok, slot).start()
        return carry

    def drain(tok, carry):
        for slot in range(TOP_K):
            row_copy(tok, slot).wait()
        return carry

    lax.fori_loop(0, tm, issue, 0, unroll=8)
    lax.fori_loop(0, tm, drain, 0, unroll=8)
    rt = rt_ref[...]
    x2 = x1_ref[...] + rt[:, TOP_K:TOP_K + 1] * ybuf[0] + rt[:, TOP_K + 1:TOP_K + 2] * ybuf[1]
    if final:
        ms = jnp.mean(x2 * x2, axis=-1, keepdims=True)
        x2 = x2 * lax.rsqrt(ms + NORM_EPS) * g_ref[...]
    o_ref[...] = x2


def _combine(x1, rt, yb, dest, *, tm, final_g=None):
    n_tok, d = x1.shape
    nt = n_tok // tm
    final = final_g is not None
    in_specs = [pl.BlockSpec((1, 1, TOP_K * tm), lambda i: (i, 0, 0), memory_space=pltpu.SMEM),
                pl.BlockSpec((tm, d), lambda i: (i, 0)),
                pl.BlockSpec((tm, V7X_LANES), lambda i: (i, 0)),
                pl.BlockSpec(memory_space=pl.ANY)]
    args = [dest.reshape(nt, 1, TOP_K * tm), x1, rt, yb]
    if final:
        in_specs.append(pl.BlockSpec((1, d), lambda i: (0, 0)))
        args.append(final_g.reshape(1, d))
    return pl.pallas_call(
        functools.partial(_combine_kernel, final=final),
        grid=(nt,),
        in_specs=in_specs,
        out_specs=pl.BlockSpec((tm, d), lambda i: (i, 0)),
        out_shape=jax.ShapeDtypeStruct((n_tok, d), F32),
        scratch_shapes=[pltpu.VMEM((TOP_K, tm, d), F32), pltpu.SemaphoreType.DMA(())],
        compiler_params=_cparams(("arbitrary",)),
        name="moe_combine_final" if final else "moe_combine",
    )(*args)


def _moe_layer(x1, h2, rt, counts, w1, w3, w2, *, final_g=None):
    n_tok, d = x1.shape
    n_asg = n_tok * TOP_K
    counts = counts[0, :N_EXPERTS].astype(jnp.int32)
    padded = (counts + MOE_BLOCK - 1) // MOE_BLOCK * MOE_BLOCK
    pend = jnp.cumsum(padded)
    pstart = pend - padded
    expert = rt[:, 0:TOP_K].astype(jnp.int32)
    pos = rt[:, 2 * TOP_K:3 * TOP_K].astype(jnp.int32)
    is_e = expert[:, :, None] == jnp.arange(N_EXPERTS, dtype=jnp.int32)[None, None, :]
    dest = jnp.sum(jnp.where(is_e, pstart[None, None, :], 0), axis=-1) + pos
    n_blocks = -(-n_asg // MOE_BLOCK) + N_EXPERTS
    n_rows = n_blocks * MOE_BLOCK
    block_start = jnp.arange(n_blocks, dtype=jnp.int32) * MOE_BLOCK
    block_exp = jnp.minimum(jnp.sum(block_start[:, None] >= pend[None, :], axis=1), N_EXPERTS - 1).astype(jnp.int32)
    block_valid = (block_start < pend[-1]).astype(jnp.int32)
    block_first = jnp.concatenate([jnp.ones((1,), jnp.int32),
                                   (block_exp[1:] != block_exp[:-1]).astype(jnp.int32)])
    xs = _dispatch(h2, dest, n_rows, tm=512)
    yb = _moe_ffn(xs, block_exp, block_valid, block_first, w1, w3, w2)
    return _combine(x1, rt, yb, dest, tm=256, final_g=final_g)


def kernel(x, mem, mem_norm, final_norm, norm_mix, norm_ffn, w_mem_kv, w_out, na_w_in, na_rpb, gla_w_in,
           gla_gate_up, gla_gate_bias, gla_out_norm, moe_w_group, moe_b_group, moe_w_router, moe_b_router,
           moe_w1, moe_w3, moe_w2):
    b, t, d = x.shape
    n = b * t
    depth = norm_mix.shape[0]
    n_mem = mem.shape[1]
    xf = x.reshape(n, d)
    for i in range(depth):
        j = i // 2
        mkv = _norm_matmul(mem.reshape(b * n_mem, d), mem_norm, w_mem_kv[i].astype(BF16),
                           tm=256, out_dtype=F32, name="mem_kv_proj").reshape(b, n_mem, 2 * MEM_DIM)
        kbd, vbd = _memory_kv_blockdiag(mkv)
        if i % 2 == 0:
            u = _norm_matmul(xf, norm_mix[i], na_w_in[j].astype(BF16), tm=512, out_dtype=BF16,
                             name="na_in_proj").reshape(b, t, -1)
            mix = _na_attention(u, na_rpb[j])
            mq_off = NA_MQ_OFF
            w_o = w_out[i].astype(BF16)
            w_o = (w_o[:MIX_DIM], w_o[MIX_DIM:])
        else:
            w_pad, gu, gb, onorm = _gla_weights(gla_w_in[j], gla_gate_up[j], gla_gate_bias[j], gla_out_norm[j])
            u = _norm_matmul(xf, norm_mix[i], w_pad, tm=512, out_dtype=BF16, name="gla_in_proj").reshape(b, t, -1)
            ofwd = _gla_direction(u, gu[0], gb[0], reverse=False, tb=512)
            mix = _gla_direction(u, gu[1], gb[1], reverse=True, tb=512, ofwd=ofwd, onorm=onorm)
            mq_off = GLA_MQ_OFF
            w_mix = w_out[i][:MIX_DIM].reshape(GLA_HEADS, GLA_DV, d)
            w_mix = jnp.pad(w_mix, ((0, 0), (0, GLA_DV_PAD - GLA_DV), (0, 0))).reshape(GLA_MIX_PAD, d)
            w_o = (w_mix.astype(BF16), w_out[i][MIX_DIM:].astype(BF16))
        w_rt = jnp.pad(jnp.concatenate([moe_w_group[i], moe_w_router[i]], axis=1),
                       ((0, 0), (0, V7X_LANES - N_GROUPS - N_EXPERTS)))
        b_rt = jnp.pad(jnp.concatenate([moe_b_group[i], moe_b_router[i]]),
                       (0, V7X_LANES - N_GROUPS - N_EXPERTS)).reshape(1, V7X_LANES)
        x1, h2, rt, counts = _post_mixer(xf.reshape(b, t, d), mix, u, mq_off, kbd, vbd, w_o, norm_ffn[i],
                                         w_rt.astype(F32), b_rt.astype(F32), tm=256)
        xf = _moe_layer(x1.reshape(n, d), h2.reshape(n, d), rt.reshape(n, V7X_LANES), counts,
                        moe_w1[i], moe_w3[i], moe_w2[i],
                        final_g=final_norm if i == depth - 1 else None)
    return xf.reshape(b, t, d)
```

```python
import functools

import numpy as np
import jax
import jax.numpy as jnp
from jax import lax
from jax.experimental import pallas as pl
from jax.experimental.pallas import tpu as pltpu

F32 = jnp.float32
BF16 = jnp.bfloat16

D_MODEL = 1024
GRID_W = 64
HEAD_DIM = 64
NORM_EPS = 1e-6
MEM_HEADS = 4
MEM_DIM = MEM_HEADS * HEAD_DIM
MIX_DIM = D_MODEL - MEM_DIM
NA_HEADS = MIX_DIM // HEAD_DIM
NA_WIN_H = 8
NA_WIN_W = 16
GLA_HEADS = 4
GLA_DK = MIX_DIM // 2 // GLA_HEADS
GLA_DV = MIX_DIM // GLA_HEADS
GLA_GATE_RANK = 16
GLA_TAU = 16.0
GLA_CHUNK = 64
N_GROUPS = 4
EXPERTS_PER_GROUP = 8
N_EXPERTS = N_GROUPS * EXPERTS_PER_GROUP
TOP_K = 2
EXPERT_FF = 512
MOE_BLOCK = 256

V7X_LANES = 128
V7X_MXU_DIM = 256
V7X_VMEM_LIMIT_BYTES = 56 * 1024 * 1024

GLA_DK_PAD = 128
GLA_DV_PAD = 256
GLA_Q_OFF = 0
GLA_K_OFF = GLA_Q_OFF + GLA_HEADS * GLA_DK_PAD
GLA_V_OFF = GLA_K_OFF + GLA_HEADS * GLA_DK_PAD
GLA_G_OFF = GLA_V_OFF + GLA_HEADS * GLA_DV_PAD
GLA_R_OFF = GLA_G_OFF + GLA_HEADS * GLA_DV_PAD
GLA_R_PAD = 256
GLA_MQ_OFF = GLA_R_OFF + GLA_R_PAD
GLA_IN_PAD = GLA_MQ_OFF + MEM_DIM
GLA_MIX_PAD = GLA_HEADS * GLA_DV_PAD
NA_MQ_OFF = 3 * MIX_DIM


def _cparams(semantics):
    return pltpu.CompilerParams(dimension_semantics=semantics, vmem_limit_bytes=V7X_VMEM_LIMIT_BYTES)


def _norm_matmul_kernel(x_ref, g_ref, w_ref, o_ref, *, col_chunk):
    x = x_ref[...]
    ms = jnp.mean(x * x, axis=-1, keepdims=True)
    y = (x * lax.rsqrt(ms + NORM_EPS) * g_ref[...]).astype(BF16)
    n_out = o_ref.shape[1]
    for c in range(0, n_out, col_chunk):
        o_ref[:, c:c + col_chunk] = jnp.dot(
            y, w_ref[:, c:c + col_chunk], preferred_element_type=F32).astype(o_ref.dtype)


def _norm_matmul(x, g, w, *, tm, out_dtype, name):
    n, d = x.shape
    n_out = w.shape[1]
    col_chunk = 512 if n_out % 512 == 0 else n_out
    return pl.pallas_call(
        functools.partial(_norm_matmul_kernel, col_chunk=col_chunk),
        grid=(n // tm,),
        in_specs=[pl.BlockSpec((tm, d), lambda i: (i, 0)),
                  pl.BlockSpec((1, d), lambda i: (0, 0)),
                  pl.BlockSpec((d, n_out), lambda i: (0, 0))],
        out_specs=pl.BlockSpec((tm, n_out), lambda i: (i, 0)),
        out_shape=jax.ShapeDtypeStruct((n, n_out), out_dtype),
        compiler_params=_cparams(("parallel",)),
        name=name,
    )(x, g.reshape(1, d), w)


def _na_kernel(q_ref, k_ref, v_ref, bias_ref, o_ref):
    r = pl.program_id(1)
    rows = k_ref.shape[1] // GRID_W
    rs = jnp.clip(r - NA_WIN_H // 2, 0, rows - NA_WIN_H)
    start = pl.multiple_of(rs * GRID_W, GRID_W)
    n_keys = NA_WIN_H * GRID_W
    heads_per_slab = V7X_MXU_DIM // HEAD_DIM
    scale = HEAD_DIM ** -0.5
    lane_head = lax.broadcasted_iota(jnp.int32, (GRID_W, V7X_MXU_DIM), 1) // HEAD_DIM
    for s in range(MIX_DIM // V7X_MXU_DIM):
        cs = slice(s * V7X_MXU_DIM, (s + 1) * V7X_MXU_DIM)
        qq = q_ref[0, :, cs] * scale
        kw = k_ref[0, pl.ds(start, n_keys), cs]
        vw = v_ref[0, pl.ds(start, n_keys), cs]
        lhs = jnp.concatenate(
            [jnp.where(lane_head == i, qq, jnp.zeros_like(qq)) for i in range(heads_per_slab)], axis=0)
        sc = lax.dot_general(lhs, kw, (((1,), (1,)), ((), ())), preferred_element_type=F32)
        sc = sc + bias_ref[0, s * V7X_MXU_DIM:(s + 1) * V7X_MXU_DIM, :]
        m = jnp.max(sc, axis=-1, keepdims=True)
        p = jnp.exp(sc - m)
        l = jnp.sum(p, axis=-1, keepdims=True)
        o = jnp.dot(p.astype(BF16), vw, preferred_element_type=F32)
        o = o * (1.0 / l)
        acc = jnp.zeros((GRID_W, V7X_MXU_DIM), F32)
        for i in range(heads_per_slab):
            acc = acc + jnp.where(lane_head == i, o[i * GRID_W:(i + 1) * GRID_W], 0.0)
        o_ref[0, :, cs] = acc.astype(o_ref.dtype)


def _na_bias_table(rpb):
    qc = np.arange(GRID_W)[:, None]
    kc = np.arange(GRID_W)[None, :]
    cstart = np.clip(qc - NA_WIN_W // 2, 0, GRID_W - NA_WIN_W)
    col_in = (kc >= cstart) & (kc < cstart + NA_WIN_W)
    dcol = np.clip(kc - qc, 1 - NA_WIN_W, NA_WIN_W - 1) + NA_WIN_W - 1
    pick = jnp.asarray(dcol[None] == np.arange(2 * NA_WIN_W - 1)[:, None, None], F32)
    rows = jnp.stack([rpb.astype(F32)[:, o:o + NA_WIN_H] for o in range(NA_WIN_H)])
    tbl = jnp.einsum('ohjd,dqk->ohqjk', rows, pick, precision=lax.Precision.HIGHEST)
    tbl = jnp.where(col_in[None, None, :, None, :], tbl, -jnp.inf)
    return tbl.reshape(NA_WIN_H, NA_HEADS * GRID_W, NA_WIN_H * GRID_W)


def _na_attention(u, rpb):
    b, t, _ = u.shape
    rows = t // GRID_W
    bias = _na_bias_table(rpb)

    def bias_idx(bi, r):
        return (jnp.clip(r - NA_WIN_H // 2, 0, rows - NA_WIN_H) - r + NA_WIN_H - 1, 0, 0)

    return pl.pallas_call(
        _na_kernel,
        grid=(b, rows),
        in_specs=[pl.BlockSpec((1, GRID_W, MIX_DIM), lambda bi, r: (bi, r, 0)),
                  pl.BlockSpec((1, t, MIX_DIM), lambda bi, r: (bi, 0, 1)),
                  pl.BlockSpec((1, t, MIX_DIM), lambda bi, r: (bi, 0, 2)),
                  pl.BlockSpec((1, NA_HEADS * GRID_W, NA_WIN_H * GRID_W), bias_idx)],
        out_specs=pl.BlockSpec((1, GRID_W, MIX_DIM), lambda bi, r: (bi, r, 0)),
        out_shape=jax.ShapeDtypeStruct((b, t, MIX_DIM), BF16),
        compiler_params=_cparams(("parallel", "arbitrary")),
        name="na_attention",
    )(u, u, u, bias)


def _gla_kernel(*refs, reverse, final):
    if final:
        (q_ref, k_ref, v_ref, r_ref, gu_ref, gb_ref, ofwd_ref, g_ref, onorm_ref, o_ref, st_ref) = refs
    else:
        (q_ref, k_ref, v_ref, r_ref, gu_ref, gb_ref, o_ref, st_ref) = refs
    c = GLA_CHUNK

    @pl.when(pl.program_id(1) == 0)
    def _():
        st_ref[...] = jnp.zeros_like(st_ref)

    z = jnp.dot(r_ref[0], gu_ref[...], preferred_element_type=F32) + gb_ref[...]
    la = (jnp.minimum(z, 0.0) - jnp.log(1.0 + jnp.exp(-jnp.abs(z)))) * (1.0 / GLA_TAU)
    ri = lax.broadcasted_iota(jnp.int32, (c, c), 0)
    ci = lax.broadcasted_iota(jnp.int32, (c, c), 1)
    tri = (ci >= ri) if reverse else (ci <= ri)
    trif = tri.astype(F32)
    mid = c // 2 if reverse else c // 2 - 1
    last = 0 if reverse else c - 1
    scale = GLA_DK ** -0.5
    n_chunks = q_ref.shape[1] // c
    order = range(n_chunks - 1, -1, -1) if reverse else range(n_chunks)
    for ch in order:
        sl = slice(ch * c, (ch + 1) * c)
        bcum = jnp.dot(trif, la[sl], precision=lax.Precision.HIGHEST, preferred_element_type=F32)
        b_mid = bcum[mid:mid + 1]
        b_last = bcum[last:last + 1]
        qc = q_ref[0, sl, :].astype(F32) * scale
        kc = k_ref[0, sl, :].astype(F32)
        vc = v_ref[0, sl, :]
        qe = (qc * jnp.exp(bcum - b_mid)).astype(BF16)
        ke = (kc * jnp.exp(b_mid - bcum)).astype(BF16)
        qs = (qc * jnp.exp(bcum)).astype(BF16)
        ks = (kc * jnp.exp(b_last - bcum)).astype(BF16)
        dec = jnp.exp(b_last)
        for h in range(GLA_HEADS):
            hs = slice(h * GLA_DK_PAD, (h + 1) * GLA_DK_PAD)
            vs = slice(h * GLA_DV_PAD, (h + 1) * GLA_DV_PAD)
            a = lax.dot_general(qe[:, hs], ke[:, hs], (((1,), (1,)), ((), ())), preferred_element_type=F32)
            a = jnp.where(tri, a, 0.0).astype(BF16)
            vh = vc[:, vs]
            st = st_ref[h]
            o = jnp.dot(a, vh, preferred_element_type=F32) + lax.dot_general(
                qs[:, hs], st.astype(BF16), (((1,), (1,)), ((), ())), preferred_element_type=F32)
            kv_t = lax.dot_general(vh, ks[:, hs], (((0,), (0,)), ((), ())), preferred_element_type=F32)
            st_ref[h] = st * dec[:, hs] + kv_t
            if final:
                tot = ofwd_ref[0, sl, vs] + o
                ms = jnp.sum(tot * tot, axis=-1, keepdims=True) * (1.0 / GLA_DV)
                y = tot * lax.rsqrt(ms + NORM_EPS) * onorm_ref[:, vs]
                g = g_ref[0, sl, vs].astype(F32)
                o_ref[0, sl, vs] = (y * (g / (1.0 + jnp.exp(-g)))).astype(o_ref.dtype)
            else:
                o_ref[0, sl, vs] = o


def _gla_direction(u, gu, gb, *, reverse, tb, ofwd=None, onorm=None):
    b, t, _ = u.shape
    nt = t // tb
    final = ofwd is not None
    tix = (lambda ti: nt - 1 - ti) if reverse else (lambda ti: ti)
    qw = GLA_HEADS * GLA_DK_PAD
    vw = GLA_HEADS * GLA_DV_PAD
    in_specs = [pl.BlockSpec((1, tb, qw), lambda bi, ti: (bi, tix(ti), GLA_Q_OFF // qw)),
                pl.BlockSpec((1, tb, qw), lambda bi, ti: (bi, tix(ti), GLA_K_OFF // qw)),
                pl.BlockSpec((1, tb, vw), lambda bi, ti: (bi, tix(ti), GLA_V_OFF // vw)),
                pl.BlockSpec((1, tb, GLA_R_PAD), lambda bi, ti: (bi, tix(ti), GLA_R_OFF // GLA_R_PAD)),
                pl.BlockSpec((GLA_R_PAD, qw), lambda bi, ti: (0, 0)),
                pl.BlockSpec((1, qw), lambda bi, ti: (0, 0))]
    args = [u, u, u, u, gu, gb]
    if final:
        in_specs += [pl.BlockSpec((1, tb, vw), lambda bi, ti: (bi, tix(ti), 0)),
                     pl.BlockSpec((1, tb, vw), lambda bi, ti: (bi, tix(ti), GLA_G_OFF // vw)),
                     pl.BlockSpec((1, vw), lambda bi, ti: (0, 0))]
        args += [ofwd, u, onorm]
    return pl.pallas_call(
        functools.partial(_gla_kernel, reverse=reverse, final=final),
        grid=(b, nt),
        in_specs=in_specs,
        out_specs=pl.BlockSpec((1, tb, vw), lambda bi, ti: (bi, tix(ti), 0)),
        out_shape=jax.ShapeDtypeStruct((b, t, vw), BF16 if final else F32),
        scratch_shapes=[pltpu.VMEM((GLA_HEADS, GLA_DV_PAD, GLA_DK_PAD), F32)],
        compiler_params=_cparams(("parallel", "arbitrary")),
        name="gla_bwd_final" if final else "gla_fwd",
    )(*args)


def _gla_weights(w_in, gate_up, gate_bias, out_norm):
    d = w_in.shape[0]
    kd = GLA_HEADS * GLA_DK
    wq, wk, wv, wg, wr, wmq = jnp.split(
        w_in, np.cumsum([kd, kd, MIX_DIM, MIX_DIM, 2 * GLA_GATE_RANK]), axis=1)

    def pad_heads(w, dh, dh_pad):
        w = w.reshape(w.shape[0], GLA_HEADS, dh)
        w = jnp.pad(w, ((0, 0), (0, 0), (0, dh_pad - dh)))
        return w.reshape(w.shape[0], GLA_HEADS * dh_pad)

    w_pad = jnp.concatenate([
        pad_heads(wq, GLA_DK, GLA_DK_PAD), pad_heads(wk, GLA_DK, GLA_DK_PAD),
        pad_heads(wv, GLA_DV, GLA_DV_PAD), pad_heads(wg, GLA_DV, GLA_DV_PAD),
        jnp.pad(wr, ((0, 0), (0, GLA_R_PAD - 2 * GLA_GATE_RANK))), wmq], axis=1)
    gu = []
    for di in range(2):
        up = pad_heads(gate_up[di], GLA_DK, GLA_DK_PAD)
        gu.append(jnp.pad(up, ((di * GLA_GATE_RANK, GLA_R_PAD - (di + 1) * GLA_GATE_RANK), (0, 0))))
    gb = [pad_heads(gate_bias[di][None, :], GLA_DK, GLA_DK_PAD) for di in range(2)]
    onorm = pad_heads(out_norm[None, :].repeat(GLA_HEADS, 0).reshape(1, MIX_DIM), GLA_DV, GLA_DV_PAD)
    return w_pad.astype(BF16), [g.astype(BF16) for g in gu], [x.astype(F32) for x in gb], onorm.astype(F32)


def _post_kernel(x_ref, mix_ref, mq_ref, kbd_ref, vbd_ref, wom_ref, woa_ref, g_ref, wrt_ref, brt_ref,
                 x1_ref, h2_ref, rt_ref, cnt_ref, base_ref):
    @pl.when((pl.program_id(0) == 0) & (pl.program_id(1) == 0))
    def _():
        base_ref[...] = jnp.zeros_like(base_ref)

    mq = mq_ref[0]
    s = jnp.dot(mq, kbd_ref[0], preferred_element_type=F32) * (HEAD_DIM ** -0.5)
    n_mem = kbd_ref.shape[2] // MEM_HEADS
    ps = []
    for h in range(MEM_HEADS):
        seg = s[:, h * n_mem:(h + 1) * n_mem]
        e = jnp.exp(seg - jnp.max(seg, axis=-1, keepdims=True))
        ps.append((e * (1.0 / jnp.sum(e, axis=-1, keepdims=True))).astype(BF16))
    att = jnp.dot(jnp.concatenate(ps, axis=1), vbd_ref[0], preferred_element_type=F32).astype(BF16)
    x1 = (x_ref[0] + jnp.dot(mix_ref[0], wom_ref[...], preferred_element_type=F32)
          + jnp.dot(att, woa_ref[...], preferred_element_type=F32))
    x1_ref[0] = x1
    ms = jnp.mean(x1 * x1, axis=-1, keepdims=True)
    h2 = x1 * lax.rsqrt(ms + NORM_EPS) * g_ref[...]
    h2_ref[0] = h2.astype(h2_ref.dtype)
    logits = jnp.dot(h2, wrt_ref[...], precision=lax.Precision.HIGHEST,
                     preferred_element_type=F32) + brt_ref[...]
    lane = lax.broadcasted_iota(jnp.int32, logits.shape, 1).astype(F32)
    neg = -jnp.inf
    big = 1e9
    gl = jnp.where(lane < N_GROUPS, logits, neg)
    gm = jnp.max(gl, axis=-1, keepdims=True)
    g_gate = 1.0 / jnp.sum(jnp.exp(gl - gm), axis=-1, keepdims=True)
    g_idx = jnp.min(jnp.where(gl == gm, lane, big), axis=-1, keepdims=True)
    lo = N_GROUPS + EXPERTS_PER_GROUP * g_idx
    v1 = jnp.where((lane >= lo) & (lane < lo + EXPERTS_PER_GROUP), logits, neg)
    m1 = jnp.max(v1, axis=-1, keepdims=True)
    i1 = jnp.min(jnp.where(v1 == m1, lane, big), axis=-1, keepdims=True)
    v2 = jnp.where(lane == i1, neg, v1)
    m2 = jnp.max(v2, axis=-1, keepdims=True)
    i2 = jnp.min(jnp.where(v2 == m2, lane, big), axis=-1, keepdims=True)
    t = jnp.exp(m2 - m1)
    w1 = g_gate / (1.0 + t)
    w2 = g_gate * t / (1.0 + t)
    e1 = i1 - N_GROUPS
    e2 = i2 - N_GROUPS
    tm = logits.shape[0]
    onehot = jnp.where(lane == e1, 1.0, 0.0) + jnp.where(lane == e2, 1.0, 0.0)
    ri = lax.broadcasted_iota(jnp.int32, (tm, tm), 0)
    ci = lax.broadcasted_iota(jnp.int32, (tm, tm), 1)
    before = jnp.where(ci < ri, 1.0, 0.0).astype(BF16)
    ahead = jnp.dot(before, onehot.astype(BF16), preferred_element_type=F32) + base_ref[...]
    pos1 = jnp.sum(jnp.where(lane == e1, ahead, 0.0), axis=-1, keepdims=True)
    pos2 = jnp.sum(jnp.where(lane == e2, ahead, 0.0), axis=-1, keepdims=True)
    base_ref[...] = base_ref[...] + jnp.sum(onehot, axis=0, keepdims=True)
    cnt_ref[...] = base_ref[...]
    rt = jnp.where(lane == 0, e1,
                   jnp.where(lane == 1, e2,
                             jnp.where(lane == 2, w1,
                                       jnp.where(lane == 3, w2,
                                                 jnp.where(lane == 4, pos1, jnp.where(lane == 5, pos2, 0.0))))))
    rt_ref[0] = rt


def _post_mixer(x, mix, u, mq_off, kbd, vbd, w_out, g, w_rt, b_rt, *, tm):
    b, t, d = x.shape
    mixw = mix.shape[2]
    wom = w_out[0]
    woa = w_out[1]
    return pl.pallas_call(
        _post_kernel,
        grid=(b, t // tm),
        in_specs=[pl.BlockSpec((1, tm, d), lambda bi, ti: (bi, ti, 0)),
                  pl.BlockSpec((1, tm, mixw), lambda bi, ti: (bi, ti, 0)),
                  pl.BlockSpec((1, tm, MEM_DIM), lambda bi, ti: (bi, ti, mq_off // MEM_DIM)),
                  pl.BlockSpec((1,) + kbd.shape[1:], lambda bi, ti: (bi, 0, 0)),
                  pl.BlockSpec((1,) + vbd.shape[1:], lambda bi, ti: (bi, 0, 0)),
                  pl.BlockSpec(wom.shape, lambda bi, ti: (0, 0)),
                  pl.BlockSpec(woa.shape, lambda bi, ti: (0, 0)),
                  pl.BlockSpec((1, d), lambda bi, ti: (0, 0)),
                  pl.BlockSpec(w_rt.shape, lambda bi, ti: (0, 0)),
                  pl.BlockSpec((1, V7X_LANES), lambda bi, ti: (0, 0))],
        out_specs=[pl.BlockSpec((1, tm, d), lambda bi, ti: (bi, ti, 0)),
                   pl.BlockSpec((1, tm, d), lambda bi, ti: (bi, ti, 0)),
                   pl.BlockSpec((1, tm, V7X_LANES), lambda bi, ti: (bi, ti, 0)),
                   pl.BlockSpec((1, V7X_LANES), lambda bi, ti: (0, 0))],
        out_shape=[jax.ShapeDtypeStruct((b, t, d), F32),
                   jax.ShapeDtypeStruct((b, t, d), F32),
                   jax.ShapeDtypeStruct((b, t, V7X_LANES), F32),
                   jax.ShapeDtypeStruct((1, V7X_LANES), F32)],
        scratch_shapes=[pltpu.VMEM((1, V7X_LANES), F32)],
        compiler_params=_cparams(("arbitrary", "arbitrary")),
        name="post_mixer",
    )(x, mix, u, kbd, vbd, wom, woa, g.reshape(1, d), w_rt, b_rt)


def _memory_kv_blockdiag(mkv):
    k, v = jnp.split(mkv, 2, axis=-1)
    head_of = np.arange(MEM_DIM) // HEAD_DIM
    sel = jnp.asarray(head_of[None, :] == np.arange(MEM_HEADS)[:, None], F32)
    kbd = jnp.einsum('bmd,hd->bdhm', k, sel).reshape(k.shape[0], MEM_DIM, -1)
    vbd = jnp.einsum('bmd,hd->bhmd', v, sel).reshape(v.shape[0], -1, MEM_DIM)
    return kbd.astype(BF16), vbd.astype(BF16)


def _moe_kernel(bexp_ref, bvalid_ref, bfirst_ref, xs_ref, w1_ref, w3_ref, w2_ref, y_ref, wb1, wb3, wb2):
    blk = pl.program_id(0)

    @pl.when(bvalid_ref[blk] == 0)
    def _():
        y_ref[...] = jnp.zeros_like(y_ref)

    @pl.when(bvalid_ref[blk] != 0)
    def _():
        @pl.when(bfirst_ref[blk] != 0)
        def _():
            wb1[...] = w1_ref[0, 0].astype(BF16)
            wb3[...] = w3_ref[0, 0].astype(BF16)
            wb2[...] = w2_ref[0, 0].astype(BF16)

        xb = xs_ref[...].astype(BF16)
        a = jnp.dot(xb, wb1[...], preferred_element_type=F32)
        c = jnp.dot(xb, wb3[...], preferred_element_type=F32)
        hid = (a / (1.0 + jnp.exp(-a)) * c).astype(BF16)
        y_ref[...] = jnp.dot(hid, wb2[...], preferred_element_type=F32)


def _moe_ffn(xs, block_exp, block_valid, block_first, w1, w3, w2, layer):
    n_rows, d = xs.shape
    n_blocks = n_rows // MOE_BLOCK
    ff = w1.shape[3]
    grid_spec = pltpu.PrefetchScalarGridSpec(
        num_scalar_prefetch=3,
        grid=(n_blocks,),
        in_specs=[pl.BlockSpec((MOE_BLOCK, d), lambda i, be, bv, bf: (i, 0)),
                  pl.BlockSpec((1, 1, d, ff), lambda i, be, bv, bf: (layer, be[i], 0, 0)),
                  pl.BlockSpec((1, 1, d, ff), lambda i, be, bv, bf: (layer, be[i], 0, 0)),
                  pl.BlockSpec((1, 1, ff, d), lambda i, be, bv, bf: (layer, be[i], 0, 0))],
        out_specs=pl.BlockSpec((MOE_BLOCK, d), lambda i, be, bv, bf: (i, 0)),
        scratch_shapes=[pltpu.VMEM((d, ff), BF16), pltpu.VMEM((d, ff), BF16), pltpu.VMEM((ff, d), BF16)],
    )
    return pl.pallas_call(
        _moe_kernel,
        grid_spec=grid_spec,
        out_shape=jax.ShapeDtypeStruct((n_rows, d), F32),
        compiler_params=_cparams(("arbitrary",)),
        name="moe_ffn",
    )(block_exp, block_valid, block_first, xs, w1, w3, w2)


def _dispatch_kernel(dest_ref, h_ref, xs_in_ref, xs_ref, sem):
    del xs_in_ref
    tm = h_ref.shape[0]

    def row_copy(tok, slot):
        return pltpu.make_async_copy(h_ref.at[pl.ds(tok, 1)],
                                     xs_ref.at[pl.ds(dest_ref[0, 0, TOP_K * tok + slot], 1)], sem)

    def issue(tok, carry):
        for slot in range(TOP_K):
            row_copy(tok, slot).start()
        return carry

    def drain(tok, carry):
        for slot in range(TOP_K):
            row_copy(tok, slot).wait()
        return carry

    lax.fori_loop(0, tm, issue, 0, unroll=8)
    lax.fori_loop(0, tm, drain, 0, unroll=8)


def _dispatch(h2, dest, n_rows, *, tm):
    n_tok, d = h2.shape
    nt = n_tok // tm
    return pl.pallas_call(
        _dispatch_kernel,
        grid=(nt,),
        in_specs=[pl.BlockSpec((1, 1, TOP_K * tm), lambda i: (i, 0, 0), memory_space=pltpu.SMEM),
                  pl.BlockSpec((tm, d), lambda i: (i, 0)),
                  pl.BlockSpec(memory_space=pl.ANY)],
        out_specs=pl.BlockSpec(memory_space=pl.ANY),
        out_shape=jax.ShapeDtypeStruct((n_rows, d), h2.dtype),
        scratch_shapes=[pltpu.SemaphoreType.DMA(())],
        input_output_aliases={2: 0},
        compiler_params=_cparams(("arbitrary",)),
        name="moe_dispatch",
    )(dest.reshape(nt, 1, TOP_K * tm), h2, jnp.zeros((n_rows, d), h2.dtype))


def _combine_kernel(*refs, final):
    if final:
        dest_ref, x1_ref, rt_ref, yb_ref, g_ref, o_ref, ybuf, sem = refs
    else:
        dest_ref, x1_ref, rt_ref, yb_ref, o_ref, ybuf, sem = refs
    tm = x1_ref.shape[0]

    def row_copy(tok, slot):
        return pltpu.make_async_copy(yb_ref.at[pl.ds(dest_ref[0, 0, TOP_K * tok + slot], 1)],
                                     ybuf.at[slot, pl.ds(tok, 1)], sem)

    def issue(tok, carry):
        for slot in range(TOP_K):
            row_copy(tok, slot).start()
        return carry

    def drain(tok, carry):
        for slot in range(TOP_K):
            row_copy(tok, slot).wait()
        return carry

    lax.fori_loop(0, tm, issue, 0, unroll=8)
    lax.fori_loop(0, tm, drain, 0, unroll=8)
    rt = rt_ref[...]
    x2 = x1_ref[...] + rt[:, TOP_K:TOP_K + 1] * ybuf[0] + rt[:, TOP_K + 1:TOP_K + 2] * ybuf[1]
    if final:
        ms = jnp.mean(x2 * x2, axis=-1, keepdims=True)
        x2 = x2 * lax.rsqrt(ms + NORM_EPS) * g_ref[...]
    o_ref[...] = x2


def _combine(x1, rt, yb, dest, *, tm, final_g=None):
    n_tok, d = x1.shape
    nt = n_tok // tm
    final = final_g is not None
    in_specs = [pl.BlockSpec((1, 1, TOP_K * tm), lambda i: (i, 0, 0), memory_space=pltpu.SMEM),
                pl.BlockSpec((tm, d), lambda i: (i, 0)),
                pl.BlockSpec((tm, V7X_LANES), lambda i: (i, 0)),
                pl.BlockSpec(memory_space=pl.ANY)]
    args = [dest.reshape(nt, 1, TOP_K * tm), x1, rt, yb]
    if final:
        in_specs.append(pl.BlockSpec((1, d), lambda i: (0, 0)))
        args.append(final_g.reshape(1, d))
    return pl.pallas_call(
        functools.partial(_combine_kernel, final=final),
        grid=(nt,),
        in_specs=in_specs,
        out_specs=pl.BlockSpec((tm, d), lambda i: (i, 0)),
        out_shape=jax.ShapeDtypeStruct((n_tok, d), F32),
        scratch_shapes=[pltpu.VMEM((TOP_K, tm, d), F32), pltpu.SemaphoreType.DMA(())],
        compiler_params=_cparams(("arbitrary",)),
        name="moe_combine_final" if final else "moe_combine",
    )(*args)


def _moe_layer(x1, h2, rt, counts, w1, w3, w2, layer, *, final_g=None):
    n_tok, d = x1.shape
    n_asg = n_tok * TOP_K
    counts = counts[0, :N_EXPERTS].astype(jnp.int32)
    padded = (counts + MOE_BLOCK - 1) // MOE_BLOCK * MOE_BLOCK
    pend = jnp.cumsum(padded)
    pstart = pend - padded
    expert = rt[:, 0:TOP_K].astype(jnp.int32)
    pos = rt[:, 2 * TOP_K:3 * TOP_K].astype(jnp.int32)
    is_e = expert[:, :, None] == jnp.arange(N_EXPERTS, dtype=jnp.int32)[None, None, :]
    dest = jnp.sum(jnp.where(is_e, pstart[None, None, :], 0), axis=-1) + pos
    n_blocks = -(-n_asg // MOE_BLOCK) + N_EXPERTS
    n_rows = n_blocks * MOE_BLOCK
    block_start = jnp.arange(n_blocks, dtype=jnp.int32) * MOE_BLOCK
    block_exp = jnp.minimum(jnp.sum(block_start[:, None] >= pend[None, :], axis=1), N_EXPERTS - 1).astype(jnp.int32)
    block_valid = (block_start < pend[-1]).astype(jnp.int32)
    block_first = jnp.concatenate([jnp.ones((1,), jnp.int32),
                                   (block_exp[1:] != block_exp[:-1]).astype(jnp.int32)])
    xs = _dispatch(h2, dest, n_rows, tm=512)
    yb = _moe_ffn(xs, block_exp, block_valid, block_first, w1, w3, w2, layer)
    return _combine(x1, rt, yb, dest, tm=256, final_g=final_g)


def kernel(x, mem, mem_norm, final_norm, norm_mix, norm_ffn, w_mem_kv, w_out, na_w_in, na_rpb, gla_w_in,
           gla_gate_up, gla_gate_bias, gla_out_norm, moe_w_group, moe_b_group, moe_w_router, moe_b_router,
           moe_w1, moe_w3, moe_w2):
    b, t, d = x.shape
    n = b * t
    depth = norm_mix.shape[0]
    n_mem = mem.shape[1]
    xf = x.reshape(n, d)
    for i in range(depth):
        j = i // 2
        mkv = _norm_matmul(mem.reshape(b * n_mem, d), mem_norm, w_mem_kv[i].astype(BF16),
                           tm=256, out_dtype=F32, name="mem_kv_proj").reshape(b, n_mem, 2 * MEM_DIM)
        kbd, vbd = _memory_kv_blockdiag(mkv)
        if i % 2 == 0:
            u = _norm_matmul(xf, norm_mix[i], na_w_in[j].astype(BF16), tm=512, out_dtype=BF16,
                             name="na_in_proj").reshape(b, t, -1)
            mix = _na_attention(u, na_rpb[j])
            mq_off = NA_MQ_OFF
            w_o = w_out[i].astype(BF16)
            w_o = (w_o[:MIX_DIM], w_o[MIX_DIM:])
        else:
            w_pad, gu, gb, onorm = _gla_weights(gla_w_in[j], gla_gate_up[j], gla_gate_bias[j], gla_out_norm[j])
            u = _norm_matmul(xf, norm_mix[i], w_pad, tm=512, out_dtype=BF16, name="gla_in_proj").reshape(b, t, -1)
            ofwd = _gla_direction(u, gu[0], gb[0], reverse=False, tb=512)
            mix = _gla_direction(u, gu[1], gb[1], reverse=True, tb=512, ofwd=ofwd, onorm=onorm)
            mq_off = GLA_MQ_OFF
            w_mix = w_out[i][:MIX_DIM].reshape(GLA_HEADS, GLA_DV, d)
            w_mix = jnp.pad(w_mix, ((0, 0), (0, GLA_DV_PAD - GLA_DV), (0, 0))).reshape(GLA_MIX_PAD, d)
            w_o = (w_mix.astype(BF16), w_out[i][MIX_DIM:].astype(BF16))
        w_rt = jnp.pad(jnp.concatenate([moe_w_group[i], moe_w_router[i]], axis=1),
                       ((0, 0), (0, V7X_LANES - N_GROUPS - N_EXPERTS)))
        b_rt = jnp.pad(jnp.concatenate([moe_b_group[i], moe_b_router[i]]),
                       (0, V7X_LANES - N_GROUPS - N_EXPERTS)).reshape(1, V7X_LANES)
        x1, h2, rt, counts = _post_mixer(xf.reshape(b, t, d), mix, u, mq_off, kbd, vbd, w_o, norm_ffn[i],
                                         w_rt.astype(F32), b_rt.astype(F32), tm=256)
        xf = _moe_layer(x1.reshape(n, d), h2.reshape(n, d), rt.reshape(n, V7X_LANES), counts,
                        moe_w1, moe_w3, moe_w2, i,
                        final_g=final_norm if i == depth - 1 else None)
    return xf.reshape(b, t, d)
```

```python
import functools

import numpy as np
import jax
import jax.numpy as jnp
from jax import lax
from jax.experimental import pallas as pl
from jax.experimental.pallas import tpu as pltpu

F32 = jnp.float32
BF16 = jnp.bfloat16

D_MODEL = 1024
GRID_W = 64
HEAD_DIM = 64
NORM_EPS = 1e-6
MEM_HEADS = 4
MEM_DIM = MEM_HEADS * HEAD_DIM
MIX_DIM = D_MODEL - MEM_DIM
NA_HEADS = MIX_DIM // HEAD_DIM
NA_WIN_H = 8
NA_WIN_W = 16
GLA_HEADS = 4
GLA_DK = MIX_DIM // 2 // GLA_HEADS
GLA_DV = MIX_DIM // GLA_HEADS
GLA_GATE_RANK = 16
GLA_TAU = 16.0
GLA_CHUNK = 64
N_GROUPS = 4
EXPERTS_PER_GROUP = 8
N_EXPERTS = N_GROUPS * EXPERTS_PER_GROUP
TOP_K = 2
EXPERT_FF = 512
MOE_BLOCK = 256

V7X_LANES = 128
V7X_MXU_DIM = 256
V7X_VMEM_LIMIT_BYTES = 56 * 1024 * 1024

GLA_DK_PAD = 128
GLA_DV_PAD = 256
GLA_Q_OFF = 0
GLA_K_OFF = GLA_Q_OFF + GLA_HEADS * GLA_DK_PAD
GLA_V_OFF = GLA_K_OFF + GLA_HEADS * GLA_DK_PAD
GLA_G_OFF = GLA_V_OFF + GLA_HEADS * GLA_DV_PAD
GLA_R_OFF = GLA_G_OFF + GLA_HEADS * GLA_DV_PAD
GLA_R_PAD = 256
GLA_MQ_OFF = GLA_R_OFF + GLA_R_PAD
GLA_IN_PAD = GLA_MQ_OFF + MEM_DIM
GLA_MIX_PAD = GLA_HEADS * GLA_DV_PAD
NA_MQ_OFF = 3 * MIX_DIM


def _cparams(semantics):
    return pltpu.CompilerParams(dimension_semantics=semantics, vmem_limit_bytes=V7X_VMEM_LIMIT_BYTES)


ROW_TILE = D_MODEL // V7X_LANES


def _store_row_tiles(ref, val):
    rows = val.shape[0]
    for s in range(ROW_TILE):
        ref[pl.ds(s, rows, stride=ROW_TILE), :] = val[:, s * V7X_LANES:(s + 1) * V7X_LANES]


def _load_row_tiles(ref, rows):
    return jnp.concatenate([ref[pl.ds(s, rows, stride=ROW_TILE), :] for s in range(ROW_TILE)], axis=1)


def _norm_matmul_kernel(x_ref, g_ref, w_ref, o_ref, *, col_chunk):
    x = x_ref[...]
    ms = jnp.mean(x * x, axis=-1, keepdims=True)
    y = (x * lax.rsqrt(ms + NORM_EPS) * g_ref[...]).astype(BF16)
    n_out = o_ref.shape[1]
    for c in range(0, n_out, col_chunk):
        o_ref[:, c:c + col_chunk] = jnp.dot(
            y, w_ref[:, c:c + col_chunk], preferred_element_type=F32).astype(o_ref.dtype)


def _norm_matmul(x, g, w, *, tm, out_dtype, name):
    n, d = x.shape
    n_out = w.shape[1]
    col_chunk = 512 if n_out % 512 == 0 else n_out
    return pl.pallas_call(
        functools.partial(_norm_matmul_kernel, col_chunk=col_chunk),
        grid=(n // tm,),
        in_specs=[pl.BlockSpec((tm, d), lambda i: (i, 0)),
                  pl.BlockSpec((1, d), lambda i: (0, 0)),
                  pl.BlockSpec((d, n_out), lambda i: (0, 0))],
        out_specs=pl.BlockSpec((tm, n_out), lambda i: (i, 0)),
        out_shape=jax.ShapeDtypeStruct((n, n_out), out_dtype),
        compiler_params=_cparams(("parallel",)),
        name=name,
    )(x, g.reshape(1, d), w)


def _na_kernel(q_ref, k_ref, v_ref, bias_ref, o_ref):
    r = pl.program_id(1)
    rows = k_ref.shape[1] // GRID_W
    rs = jnp.clip(r - NA_WIN_H // 2, 0, rows - NA_WIN_H)
    start = pl.multiple_of(rs * GRID_W, GRID_W)
    n_keys = NA_WIN_H * GRID_W
    heads_per_slab = V7X_MXU_DIM // HEAD_DIM
    scale = HEAD_DIM ** -0.5
    lane_head = lax.broadcasted_iota(jnp.int32, (GRID_W, V7X_MXU_DIM), 1) // HEAD_DIM
    for s in range(MIX_DIM // V7X_MXU_DIM):
        cs = slice(s * V7X_MXU_DIM, (s + 1) * V7X_MXU_DIM)
        qq = q_ref[0, :, cs] * scale
        kw = k_ref[0, pl.ds(start, n_keys), cs]
        vw = v_ref[0, pl.ds(start, n_keys), cs]
        lhs = jnp.concatenate(
            [jnp.where(lane_head == i, qq, jnp.zeros_like(qq)) for i in range(heads_per_slab)], axis=0)
        sc = lax.dot_general(lhs, kw, (((1,), (1,)), ((), ())), preferred_element_type=F32)
        sc = sc + bias_ref[0, s * V7X_MXU_DIM:(s + 1) * V7X_MXU_DIM, :]
        m = jnp.max(sc, axis=-1, keepdims=True)
        p = jnp.exp(sc - m)
        l = jnp.sum(p, axis=-1, keepdims=True)
        o = jnp.dot(p.astype(BF16), vw, preferred_element_type=F32)
        o = o * (1.0 / l)
        acc = jnp.zeros((GRID_W, V7X_MXU_DIM), F32)
        for i in range(heads_per_slab):
            acc = acc + jnp.where(lane_head == i, o[i * GRID_W:(i + 1) * GRID_W], 0.0)
        o_ref[0, :, cs] = acc.astype(o_ref.dtype)


def _na_bias_table(rpb):
    qc = np.arange(GRID_W)[:, None]
    kc = np.arange(GRID_W)[None, :]
    cstart = np.clip(qc - NA_WIN_W // 2, 0, GRID_W - NA_WIN_W)
    col_in = (kc >= cstart) & (kc < cstart + NA_WIN_W)
    dcol = np.clip(kc - qc, 1 - NA_WIN_W, NA_WIN_W - 1) + NA_WIN_W - 1
    pick = jnp.asarray(dcol[None] == np.arange(2 * NA_WIN_W - 1)[:, None, None], F32)
    rows = jnp.stack([rpb.astype(F32)[:, o:o + NA_WIN_H] for o in range(NA_WIN_H)])
    tbl = jnp.einsum('ohjd,dqk->ohqjk', rows, pick, precision=lax.Precision.HIGHEST)
    tbl = jnp.where(col_in[None, None, :, None, :], tbl, -jnp.inf)
    return tbl.reshape(NA_WIN_H, NA_HEADS * GRID_W, NA_WIN_H * GRID_W)


def _na_attention(u, rpb):
    b, t, _ = u.shape
    rows = t // GRID_W
    bias = _na_bias_table(rpb)

    def bias_idx(bi, r):
        return (jnp.clip(r - NA_WIN_H // 2, 0, rows - NA_WIN_H) - r + NA_WIN_H - 1, 0, 0)

    return pl.pallas_call(
        _na_kernel,
        grid=(b, rows),
        in_specs=[pl.BlockSpec((1, GRID_W, MIX_DIM), lambda bi, r: (bi, r, 0)),
                  pl.BlockSpec((1, t, MIX_DIM), lambda bi, r: (bi, 0, 1)),
                  pl.BlockSpec((1, t, MIX_DIM), lambda bi, r: (bi, 0, 2)),
                  pl.BlockSpec((1, NA_HEADS * GRID_W, NA_WIN_H * GRID_W), bias_idx)],
        out_specs=pl.BlockSpec((1, GRID_W, MIX_DIM), lambda bi, r: (bi, r, 0)),
        out_shape=jax.ShapeDtypeStruct((b, t, MIX_DIM), BF16),
        compiler_params=_cparams(("parallel", "arbitrary")),
        name="na_attention",
    )(u, u, u, bias)


def _gla_kernel(*refs, reverse, final):
    if final:
        (q_ref, k_ref, v_ref, r_ref, gu_ref, gb_ref, ofwd_ref, g_ref, onorm_ref, o_ref, st_ref) = refs
    else:
        (q_ref, k_ref, v_ref, r_ref, gu_ref, gb_ref, o_ref, st_ref) = refs
    c = GLA_CHUNK

    @pl.when(pl.program_id(1) == 0)
    def _():
        st_ref[...] = jnp.zeros_like(st_ref)

    z = jnp.dot(r_ref[0], gu_ref[...], preferred_element_type=F32) + gb_ref[...]
    la = (jnp.minimum(z, 0.0) - jnp.log(1.0 + jnp.exp(-jnp.abs(z)))) * (1.0 / GLA_TAU)
    ri = lax.broadcasted_iota(jnp.int32, (c, c), 0)
    ci = lax.broadcasted_iota(jnp.int32, (c, c), 1)
    tri = (ci >= ri) if reverse else (ci <= ri)
    trif = tri.astype(F32)
    mid = c // 2 if reverse else c // 2 - 1
    last = 0 if reverse else c - 1
    scale = GLA_DK ** -0.5
    n_chunks = q_ref.shape[1] // c
    order = range(n_chunks - 1, -1, -1) if reverse else range(n_chunks)
    for ch in order:
        sl = slice(ch * c, (ch + 1) * c)
        bcum = jnp.dot(trif, la[sl], precision=lax.Precision.HIGHEST, preferred_element_type=F32)
        b_mid = bcum[mid:mid + 1]
        b_last = bcum[last:last + 1]
        qc = q_ref[0, sl, :].astype(F32) * scale
        kc = k_ref[0, sl, :].astype(F32)
        vc = v_ref[0, sl, :]
        qe = (qc * jnp.exp(bcum - b_mid)).astype(BF16)
        ke = (kc * jnp.exp(b_mid - bcum)).astype(BF16)
        qs = (qc * jnp.exp(bcum)).astype(BF16)
        ks = (kc * jnp.exp(b_last - bcum)).astype(BF16)
        dec = jnp.exp(b_last)
        for h in range(GLA_HEADS):
            hs = slice(h * GLA_DK_PAD, (h + 1) * GLA_DK_PAD)
            vs = slice(h * GLA_DV_PAD, (h + 1) * GLA_DV_PAD)
            a = lax.dot_general(qe[:, hs], ke[:, hs], (((1,), (1,)), ((), ())), preferred_element_type=F32)
            a = jnp.where(tri, a, 0.0).astype(BF16)
            vh = vc[:, vs]
            st = st_ref[h]
            o = jnp.dot(a, vh, preferred_element_type=F32) + lax.dot_general(
                qs[:, hs], st.astype(BF16), (((1,), (1,)), ((), ())), preferred_element_type=F32)
            kv_t = lax.dot_general(vh, ks[:, hs], (((0,), (0,)), ((), ())), preferred_element_type=F32)
            st_ref[h] = st * dec[:, hs] + kv_t
            if final:
                tot = ofwd_ref[0, sl, vs] + o
                ms = jnp.sum(tot * tot, axis=-1, keepdims=True) * (1.0 / GLA_DV)
                y = tot * lax.rsqrt(ms + NORM_EPS) * onorm_ref[:, vs]
                g = g_ref[0, sl, vs].astype(F32)
                o_ref[0, sl, vs] = (y * (g / (1.0 + jnp.exp(-g)))).astype(o_ref.dtype)
            else:
                o_ref[0, sl, vs] = o


def _gla_direction(u, gu, gb, *, reverse, tb, ofwd=None, onorm=None):
    b, t, _ = u.shape
    nt = t // tb
    final = ofwd is not None
    tix = (lambda ti: nt - 1 - ti) if reverse else (lambda ti: ti)
    qw = GLA_HEADS * GLA_DK_PAD
    vw = GLA_HEADS * GLA_DV_PAD
    in_specs = [pl.BlockSpec((1, tb, qw), lambda bi, ti: (bi, tix(ti), GLA_Q_OFF // qw)),
                pl.BlockSpec((1, tb, qw), lambda bi, ti: (bi, tix(ti), GLA_K_OFF // qw)),
                pl.BlockSpec((1, tb, vw), lambda bi, ti: (bi, tix(ti), GLA_V_OFF // vw)),
                pl.BlockSpec((1, tb, GLA_R_PAD), lambda bi, ti: (bi, tix(ti), GLA_R_OFF // GLA_R_PAD)),
                pl.BlockSpec((GLA_R_PAD, qw), lambda bi, ti: (0, 0)),
                pl.BlockSpec((1, qw), lambda bi, ti: (0, 0))]
    args = [u, u, u, u, gu, gb]
    if final:
        in_specs += [pl.BlockSpec((1, tb, vw), lambda bi, ti: (bi, tix(ti), 0)),
                     pl.BlockSpec((1, tb, vw), lambda bi, ti: (bi, tix(ti), GLA_G_OFF // vw)),
                     pl.BlockSpec((1, vw), lambda bi, ti: (0, 0))]
        args += [ofwd, u, onorm]
    return pl.pallas_call(
        functools.partial(_gla_kernel, reverse=reverse, final=final),
        grid=(b, nt),
        in_specs=in_specs,
        out_specs=pl.BlockSpec((1, tb, vw), lambda bi, ti: (bi, tix(ti), 0)),
        out_shape=jax.ShapeDtypeStruct((b, t, vw), BF16 if final else F32),
        scratch_shapes=[pltpu.VMEM((GLA_HEADS, GLA_DV_PAD, GLA_DK_PAD), F32)],
        compiler_params=_cparams(("parallel", "arbitrary")),
        name="gla_bwd_final" if final else "gla_fwd",
    )(*args)


def _gla_weights(w_in, gate_up, gate_bias, out_norm):
    d = w_in.shape[0]
    kd = GLA_HEADS * GLA_DK
    wq, wk, wv, wg, wr, wmq = jnp.split(
        w_in, np.cumsum([kd, kd, MIX_DIM, MIX_DIM, 2 * GLA_GATE_RANK]), axis=1)

    def pad_heads(w, dh, dh_pad):
        w = w.reshape(w.shape[0], GLA_HEADS, dh)
        w = jnp.pad(w, ((0, 0), (0, 0), (0, dh_pad - dh)))
        return w.reshape(w.shape[0], GLA_HEADS * dh_pad)

    w_pad = jnp.concatenate([
        pad_heads(wq, GLA_DK, GLA_DK_PAD), pad_heads(wk, GLA_DK, GLA_DK_PAD),
        pad_heads(wv, GLA_DV, GLA_DV_PAD), pad_heads(wg, GLA_DV, GLA_DV_PAD),
        jnp.pad(wr, ((0, 0), (0, GLA_R_PAD - 2 * GLA_GATE_RANK))), wmq], axis=1)
    gu = []
    for di in range(2):
        up = pad_heads(gate_up[di], GLA_DK, GLA_DK_PAD)
        gu.append(jnp.pad(up, ((di * GLA_GATE_RANK, GLA_R_PAD - (di + 1) * GLA_GATE_RANK), (0, 0))))
    gb = [pad_heads(gate_bias[di][None, :], GLA_DK, GLA_DK_PAD) for di in range(2)]
    onorm = pad_heads(out_norm[None, :].repeat(GLA_HEADS, 0).reshape(1, MIX_DIM), GLA_DV, GLA_DV_PAD)
    return w_pad.astype(BF16), [g.astype(BF16) for g in gu], [x.astype(F32) for x in gb], onorm.astype(F32)


def _post_kernel(x_ref, mix_ref, mq_ref, kbd_ref, vbd_ref, wom_ref, woa_ref, g_ref, wrt_ref, brt_ref,
                 x1_ref, h2_ref, rt_ref, base_ref):
    @pl.when((pl.program_id(0) == 0) & (pl.program_id(1) == 0))
    def _():
        base_ref[...] = jnp.zeros_like(base_ref)

    mq = mq_ref[0]
    s = jnp.dot(mq, kbd_ref[0], preferred_element_type=F32) * (HEAD_DIM ** -0.5)
    n_mem = kbd_ref.shape[2] // MEM_HEADS
    ps = []
    for h in range(MEM_HEADS):
        seg = s[:, h * n_mem:(h + 1) * n_mem]
        e = jnp.exp(seg - jnp.max(seg, axis=-1, keepdims=True))
        ps.append((e * (1.0 / jnp.sum(e, axis=-1, keepdims=True))).astype(BF16))
    att = jnp.dot(jnp.concatenate(ps, axis=1), vbd_ref[0], preferred_element_type=F32).astype(BF16)
    x1 = (x_ref[0] + jnp.dot(mix_ref[0], wom_ref[...], preferred_element_type=F32)
          + jnp.dot(att, woa_ref[...], preferred_element_type=F32))
    x1_ref[0] = x1
    ms = jnp.mean(x1 * x1, axis=-1, keepdims=True)
    h2 = x1 * lax.rsqrt(ms + NORM_EPS) * g_ref[...]
    _store_row_tiles(h2_ref, h2)
    logits =jnp.dot(h2, wrt_ref[...], precision=lax.Precision.HIGHEST,
                     preferred_element_type=F32) + brt_ref[...]
    lane = lax.broadcasted_iota(jnp.int32, logits.shape, 1).astype(F32)
    neg = -jnp.inf
    big = 1e9
    gl = jnp.where(lane < N_GROUPS, logits, neg)
    gm = jnp.max(gl, axis=-1, keepdims=True)
    g_gate = 1.0 / jnp.sum(jnp.exp(gl - gm), axis=-1, keepdims=True)
    g_idx = jnp.min(jnp.where(gl == gm, lane, big), axis=-1, keepdims=True)
    lo = N_GROUPS + EXPERTS_PER_GROUP * g_idx
    v1 = jnp.where((lane >= lo) & (lane < lo + EXPERTS_PER_GROUP), logits, neg)
    m1 = jnp.max(v1, axis=-1, keepdims=True)
    i1 = jnp.min(jnp.where(v1 == m1, lane, big), axis=-1, keepdims=True)
    v2 = jnp.where(lane == i1, neg, v1)
    m2 = jnp.max(v2, axis=-1, keepdims=True)
    i2 = jnp.min(jnp.where(v2 == m2, lane, big), axis=-1, keepdims=True)
    t = jnp.exp(m2 - m1)
    w1 = g_gate / (1.0 + t)
    w2 = g_gate * t / (1.0 + t)
    e1 = i1 - N_GROUPS
    e2 = i2 - N_GROUPS
    tm = logits.shape[0]
    onehot = jnp.where(lane == e1, 1.0, 0.0) + jnp.where(lane == e2, 1.0, 0.0)
    ri = lax.broadcasted_iota(jnp.int32, (tm, tm), 0)
    ci = lax.broadcasted_iota(jnp.int32, (tm, tm), 1)
    before = jnp.where(ci < ri, 1.0, 0.0).astype(BF16)
    ahead = jnp.dot(before, onehot.astype(BF16), preferred_element_type=F32) + base_ref[...]
    pos1 = jnp.sum(jnp.where(lane == e1, ahead, 0.0), axis=-1, keepdims=True)
    pos2 = jnp.sum(jnp.where(lane == e2, ahead, 0.0), axis=-1, keepdims=True)
    base_ref[...] = base_ref[...] + jnp.sum(onehot, axis=0, keepdims=True)
    rt = jnp.where(lane == 0, e1,
                   jnp.where(lane == 1, e2,
                             jnp.where(lane == 2, w1,
                                       jnp.where(lane == 3, w2,
                                                 jnp.where(lane == 4, pos1, jnp.where(lane == 5, pos2, 0.0))))))
    rt_ref[0] = rt


def _post_mixer(x, mix, u, mq_off, kbd, vbd, w_out, g, w_rt, b_rt, *, tm):
    b, t, d = x.shape
    mixw = mix.shape[2]
    wom = w_out[0]
    woa = w_out[1]
    return pl.pallas_call(
        _post_kernel,
        grid=(b, t // tm),
        in_specs=[pl.BlockSpec((1, tm, d), lambda bi, ti: (bi, ti, 0)),
                  pl.BlockSpec((1, tm, mixw), lambda bi, ti: (bi, ti, 0)),
                  pl.BlockSpec((1, tm, MEM_DIM), lambda bi, ti: (bi, ti, mq_off // MEM_DIM)),
                  pl.BlockSpec((1,) + kbd.shape[1:], lambda bi, ti: (bi, 0, 0)),
                  pl.BlockSpec((1,) + vbd.shape[1:], lambda bi, ti: (bi, 0, 0)),
                  pl.BlockSpec(wom.shape, lambda bi, ti: (0, 0)),
                  pl.BlockSpec(woa.shape, lambda bi, ti: (0, 0)),
                  pl.BlockSpec((1, d), lambda bi, ti: (0, 0)),
                  pl.BlockSpec(w_rt.shape, lambda bi, ti: (0, 0)),
                  pl.BlockSpec((1, V7X_LANES), lambda bi, ti: (0, 0))],
        out_specs=[pl.BlockSpec((1, tm, d), lambda bi, ti: (bi, ti, 0)),
                   pl.BlockSpec((tm * ROW_TILE, V7X_LANES), lambda bi, ti: (bi * (t // tm) + ti, 0)),
                   pl.BlockSpec((1, tm, V7X_LANES), lambda bi, ti: (bi, ti, 0))],
        out_shape=[jax.ShapeDtypeStruct((b, t, d), F32),
                   jax.ShapeDtypeStruct((b * t * ROW_TILE, V7X_LANES), F32),
                   jax.ShapeDtypeStruct((b, t, V7X_LANES), F32)],
        scratch_shapes=[pltpu.VMEM((1, V7X_LANES), F32)],
        compiler_params=_cparams(("arbitrary", "arbitrary")),
        name="post_mixer",
    )(x, mix, u, kbd, vbd, wom, woa, g.reshape(1, d), w_rt, b_rt)


def _memory_kv_blockdiag(mkv):
    k, v = jnp.split(mkv, 2, axis=-1)
    head_of = np.arange(MEM_DIM) // HEAD_DIM
    sel = jnp.asarray(head_of[None, :] == np.arange(MEM_HEADS)[:, None], F32)
    kbd = jnp.einsum('bmd,hd->bdhm', k, sel).reshape(k.shape[0], MEM_DIM, -1)
    vbd = jnp.einsum('bmd,hd->bhmd', v, sel).reshape(v.shape[0], -1, MEM_DIM)
    return kbd.astype(BF16), vbd.astype(BF16)


def _moe_kernel(bexp_ref, brow_ref, bvalid_ref, bfirst_ref, xs_ref, w1_ref, w3_ref, w2_ref, y_ref, wb1, wb3, wb2):
    del brow_ref
    blk = pl.program_id(0)

    @pl.when(bvalid_ref[blk] == 0)
    def _():
        y_ref[...] = jnp.zeros_like(y_ref)

    @pl.when(bvalid_ref[blk] != 0)
    def _():
        @pl.when(bfirst_ref[blk] != 0)
        def _():
            wb1[...] = w1_ref[0, 0].astype(BF16)
            wb3[...] = w3_ref[0, 0].astype(BF16)
            wb2[...] = w2_ref[0, 0].astype(BF16)

        xb = _load_row_tiles(xs_ref, MOE_BLOCK).astype(BF16)
        a = jnp.dot(xb, wb1[...], preferred_element_type=F32)
        c = jnp.dot(xb, wb3[...], preferred_element_type=F32)
        hid = (a / (1.0 + jnp.exp(-a)) * c).astype(BF16)
        _store_row_tiles(y_ref, jnp.dot(hid, wb2[...], preferred_element_type=F32))


def _moe_ffn(xs, block_exp, block_row, block_valid, block_first, w1, w3, w2, layer):
    n_steps = block_exp.shape[0]
    d = w1.shape[2]
    ff = w1.shape[3]
    blk_rows = MOE_BLOCK * ROW_TILE
    grid_spec = pltpu.PrefetchScalarGridSpec(
        num_scalar_prefetch=4,
        grid=(n_steps,),
        in_specs=[pl.BlockSpec((blk_rows, V7X_LANES), lambda i, be, br, bv, bf: (br[i], 0)),
                  pl.BlockSpec((1, 1, d, ff), lambda i, be, br, bv, bf: (layer, be[i], 0, 0)),
                  pl.BlockSpec((1, 1, d, ff), lambda i, be, br, bv, bf: (layer, be[i], 0, 0)),
                  pl.BlockSpec((1, 1, ff, d), lambda i, be, br, bv, bf: (layer, be[i], 0, 0))],
        out_specs=pl.BlockSpec((blk_rows, V7X_LANES), lambda i, be, br, bv, bf: (i, 0)),
        scratch_shapes=[pltpu.VMEM((d, ff), BF16), pltpu.VMEM((d, ff), BF16), pltpu.VMEM((ff, d), BF16)],
    )
    return pl.pallas_call(
        _moe_kernel,
        grid_spec=grid_spec,
        out_shape=jax.ShapeDtypeStruct(xs.shape, F32),
        compiler_params=_cparams(("arbitrary",)),
        name="moe_ffn",
    )(block_exp, block_row, block_valid, block_first, xs, w1, w3, w2)


def _row_slice(row, n_rows):
    return pl.ds(pl.multiple_of(row * ROW_TILE, ROW_TILE), n_rows * ROW_TILE)


def _pow2_pieces(n, largest, act):
    k = largest
    while k >= 1:
        shift = k.bit_length()

        @pl.when((n & k) != 0)
        def _(k=k, shift=shift):
            act((n >> shift) << shift, k)
        k //= 2


def _run_copies(runs_ref, make_copy, act):
    def per_expert(e, carry):
        n = runs_ref[0, 0, e]
        local_row = runs_ref[0, 1, e]
        sorted_row = runs_ref[0, 2, e]

        def chunk(i, c):
            act(make_copy(local_row + 8 * i, sorted_row + 8 * i, 8))
            return c

        lax.fori_loop(0, n >> 3, chunk, 0)
        _pow2_pieces(n, 4, lambda off, k: act(make_copy(local_row + off, sorted_row + off, k)))
        return carry

    lax.fori_loop(0, N_EXPERTS, per_expert, 0)


def _dispatch_kernel(seg_ref, lpos_ref, runs_ref, h_ref, xs_ref, local, zbuf, sem):
    tm = h_ref.shape[0] // ROW_TILE
    n_rows = xs_ref.shape[0] // ROW_TILE
    zrows = zbuf.shape[0] // ROW_TILE

    def place(tok, carry):
        row = h_ref[_row_slice(tok, 1), :]
        for slot in range(TOP_K):
            local[_row_slice(lpos_ref[0, 0, TOP_K * tok + slot], 1), :] = row
        return carry

    lax.fori_loop(0, tm, place, 0, unroll=8)

    def to_sorted(local_row, sorted_row, n_rows):
        return pltpu.make_async_copy(local.at[_row_slice(local_row, n_rows)],
                                     xs_ref.at[_row_slice(sorted_row, n_rows)], sem)

    _run_copies(runs_ref, to_sorted, lambda cp: cp.start())
    _run_copies(runs_ref, to_sorted, lambda cp: cp.wait())

    @pl.when(pl.program_id(0) == pl.num_programs(0) - 1)
    def _():
        zbuf[...] = jnp.zeros_like(zbuf)

        def zero_copy(first_row, k):
            return pltpu.make_async_copy(zbuf.at[_row_slice(0, k)], xs_ref.at[_row_slice(first_row, k)], sem)

        def pad_copies(act):
            def per_expert(e, carry):
                cnt = seg_ref[0, e]
                first = seg_ref[1, e] + cnt
                _pow2_pieces((-cnt) & (MOE_BLOCK - 1), zrows, lambda off, k: act(zero_copy(first + off, k)))
                return carry

            lax.fori_loop(0, N_EXPERTS, per_expert, 0)
            used = seg_ref[2, 0]

            def tail(i, carry):
                act(zero_copy(used + i * zrows, zrows))
                return carry

            lax.fori_loop(0, (n_rows - used) // zrows, tail, 0)

        pad_copies(lambda cp: cp.start())
        pad_copies(lambda cp: cp.wait())


def _dispatch(h2_tiles, seg, lpos, runs, n_rows, *, tm):
    n_tok = h2_tiles.shape[0] // ROW_TILE
    nt = n_tok // tm
    grid_spec = pltpu.PrefetchScalarGridSpec(
        num_scalar_prefetch=1,
        grid=(nt,),
        in_specs=[pl.BlockSpec((1, 1, TOP_K * tm), lambda i, c: (i, 0, 0), memory_space=pltpu.SMEM),
                  pl.BlockSpec((1, 3, N_EXPERTS), lambda i, c: (i, 0, 0), memory_space=pltpu.SMEM),
                  pl.BlockSpec((tm * ROW_TILE, V7X_LANES), lambda i, c: (i, 0))],
        out_specs=pl.BlockSpec(memory_space=pl.ANY),
        scratch_shapes=[pltpu.VMEM((TOP_K * tm * ROW_TILE, V7X_LANES), F32),
                        pltpu.VMEM((MOE_BLOCK // 2 * ROW_TILE, V7X_LANES), F32),
                        pltpu.SemaphoreType.DMA(())],
    )
    return pl.pallas_call(
        _dispatch_kernel,
        grid_spec=grid_spec,
        out_shape=jax.ShapeDtypeStruct((n_rows * ROW_TILE, V7X_LANES), F32),
        compiler_params=_cparams(("arbitrary",)),
        name="moe_dispatch",
    )(seg, lpos, runs, h2_tiles)


def _combine_kernel(*refs, final):
    if final:
        lpos_ref, gate_ref, runs_ref, x1_ref, yb_ref, g_ref, o_ref, local, acc, sem = refs
    else:
        lpos_ref, gate_ref, runs_ref, x1_ref, yb_ref, o_ref, local, acc, sem = refs
    tm = x1_ref.shape[0]

    def from_sorted(local_row, sorted_row, n_rows):
        return pltpu.make_async_copy(yb_ref.at[_row_slice(sorted_row, n_rows)],
                                     local.at[_row_slice(local_row, n_rows)], sem)

    _run_copies(runs_ref, from_sorted, lambda cp: cp.start())
    _run_copies(runs_ref, from_sorted, lambda cp: cp.wait())

    def mix(tok, carry):
        y = jnp.zeros((ROW_TILE, V7X_LANES), F32)
        for slot in range(TOP_K):
            idx = TOP_K * tok + slot
            y = y + gate_ref[0, 0, idx] * local[_row_slice(lpos_ref[0, 0, idx], 1), :]
        acc[_row_slice(tok, 1), :] = y
        return carry

    lax.fori_loop(0, tm, mix, 0, unroll=8)
    x2 = x1_ref[...] + _load_row_tiles(acc, tm)
    if final:
        ms = jnp.mean(x2 * x2, axis=-1, keepdims=True)
        x2 = x2 * lax.rsqrt(ms + NORM_EPS) * g_ref[...]
    o_ref[...] = x2


def _combine(x1, yb, lpos, gates, runs, *, tm, final_g=None):
    n_tok, d = x1.shape
    nt = n_tok // tm
    final = final_g is not None
    in_specs = [pl.BlockSpec((1, 1, TOP_K * tm), lambda i: (i, 0, 0), memory_space=pltpu.SMEM),
                pl.BlockSpec((1, 1, TOP_K * tm), lambda i: (i, 0, 0), memory_space=pltpu.SMEM),
                pl.BlockSpec((1, 3, N_EXPERTS), lambda i: (i, 0, 0), memory_space=pltpu.SMEM),
                pl.BlockSpec((tm, d), lambda i: (i, 0)),
                pl.BlockSpec(memory_space=pl.ANY)]
    args = [lpos, gates, runs, x1, yb]
    if final:
        in_specs.append(pl.BlockSpec((1, d), lambda i: (0, 0)))
        args.append(final_g.reshape(1, d))
    return pl.pallas_call(
        functools.partial(_combine_kernel, final=final),
        grid=(nt,),
        in_specs=in_specs,
        out_specs=pl.BlockSpec((tm, d), lambda i: (i, 0)),
        out_shape=jax.ShapeDtypeStruct((n_tok, d), F32),
        scratch_shapes=[pltpu.VMEM((TOP_K * tm * ROW_TILE, V7X_LANES), F32),
                        pltpu.VMEM((tm * ROW_TILE, V7X_LANES), F32),
                        pltpu.SemaphoreType.DMA(())],
        compiler_params=_cparams(("arbitrary",)),
        name="moe_combine_final" if final else "moe_combine",
    )(*args)


def _moe_plan(rt, n_tok, tm):
    nt = n_tok // tm
    expert = rt[:, 0:TOP_K].astype(jnp.int32)
    gate = rt[:, TOP_K:2 * TOP_K]
    pos = rt[:, 2 * TOP_K:3 * TOP_K].astype(jnp.int32)
    is_e = (expert[:, :, None] == jnp.arange(N_EXPERTS, dtype=jnp.int32)).reshape(nt, tm, TOP_K, N_EXPERTS)
    cnt_tile = jnp.sum(is_e, axis=(1, 2), dtype=jnp.int32)
    seen_before = jnp.cumsum(cnt_tile, axis=0) - cnt_tile
    local_start = jnp.cumsum(cnt_tile, axis=1) - cnt_tile
    counts = jnp.sum(cnt_tile, axis=0)
    shift = local_start - seen_before
    lpos = pos.reshape(nt, tm, TOP_K) + jnp.sum(jnp.where(is_e, shift[:, None, None, :], 0), axis=-1)
    nblk = (counts + MOE_BLOCK - 1) // MOE_BLOCK
    bend = jnp.cumsum(nblk)
    seg_start = (bend - nblk) * MOE_BLOCK
    sorted_start = seg_start[None, :] + seen_before
    runs = jnp.stack([cnt_tile, local_start, sorted_start], axis=1)
    seg = jnp.stack([counts, seg_start, jnp.full((N_EXPERTS,), bend[-1] * MOE_BLOCK, jnp.int32)])
    n_steps = n_tok * TOP_K // MOE_BLOCK + N_EXPERTS
    step = jnp.arange(n_steps, dtype=jnp.int32)
    valid = step < bend[-1]
    brow = jnp.minimum(step, bend[-1] - 1)
    bexp = jnp.sum(brow[:, None] >= bend[None, :], axis=1).astype(jnp.int32)
    bfirst = jnp.concatenate([jnp.ones((1,), jnp.int32), (bexp[1:] != bexp[:-1]).astype(jnp.int32)])
    return dict(seg=seg.astype(jnp.int32), n_rows=n_steps * MOE_BLOCK,
                lpos=lpos.reshape(nt, 1, TOP_K * tm), gates=gate.reshape(nt, 1, TOP_K * tm),
                runs=runs, bexp=bexp, brow=brow.astype(jnp.int32), bvalid=valid.astype(jnp.int32), bfirst=bfirst)


def _moe_layer(x1, h2_tiles, rt, w1, w3, w2, layer, *, final_g=None):
    n_tok = x1.shape[0]
    tm = 512
    plan = _moe_plan(rt, n_tok, tm)
    xs = _dispatch(h2_tiles, plan["seg"], plan["lpos"], plan["runs"], plan["n_rows"], tm=tm)
    yb = _moe_ffn(xs, plan["bexp"], plan["brow"], plan["bvalid"], plan["bfirst"], w1, w3, w2, layer)
    return _combine(x1, yb, plan["lpos"], plan["gates"], plan["runs"], tm=tm, final_g=final_g)


def kernel(x, mem, mem_norm, final_norm, norm_mix, norm_ffn, w_mem_kv, w_out, na_w_in, na_rpb, gla_w_in,
           gla_gate_up, gla_gate_bias, gla_out_norm, moe_w_group, moe_b_group, moe_w_router, moe_b_router,
           moe_w1, moe_w3, moe_w2):
    b, t, d = x.shape
    n = b * t
    depth = norm_mix.shape[0]
    n_mem = mem.shape[1]
    xf = x.reshape(n, d)
    for i in range(depth):
        j = i // 2
        mkv = _norm_matmul(mem.reshape(b * n_mem, d), mem_norm, w_mem_kv[i].astype(BF16),
                           tm=256, out_dtype=F32, name="mem_kv_proj").reshape(b, n_mem, 2 * MEM_DIM)
        kbd, vbd = _memory_kv_blockdiag(mkv)
        if i % 2 == 0:
            u = _norm_matmul(xf, norm_mix[i], na_w_in[j].astype(BF16), tm=512, out_dtype=BF16,
                             name="na_in_proj").reshape(b, t, -1)
            mix = _na_attention(u, na_rpb[j])
            mq_off = NA_MQ_OFF
            w_o = w_out[i].astype(BF16)
            w_o = (w_o[:MIX_DIM], w_o[MIX_DIM:])
        else:
            w_pad, gu, gb, onorm = _gla_weights(gla_w_in[j], gla_gate_up[j], gla_gate_bias[j], gla_out_norm[j])
            u = _norm_matmul(xf, norm_mix[i], w_pad, tm=512, out_dtype=BF16, name="gla_in_proj").reshape(b, t, -1)
            ofwd = _gla_direction(u, gu[0], gb[0], reverse=False, tb=512)
            mix = _gla_direction(u, gu[1], gb[1], reverse=True, tb=512, ofwd=ofwd, onorm=onorm)
            mq_off = GLA_MQ_OFF
            w_mix = w_out[i][:MIX_DIM].reshape(GLA_HEADS, GLA_DV, d)
            w_mix = jnp.pad(w_mix, ((0, 0), (0, GLA_DV_PAD - GLA_DV), (0, 0))).reshape(GLA_MIX_PAD, d)
            w_o = (w_mix.astype(BF16), w_out[i][MIX_DIM:].astype(BF16))
        w_rt = jnp.pad(jnp.concatenate([moe_w_group[i], moe_w_router[i]], axis=1),
                       ((0, 0), (0, V7X_LANES - N_GROUPS - N_EXPERTS)))
        b_rt = jnp.pad(jnp.concatenate([moe_b_group[i], moe_b_router[i]]),
                       (0, V7X_LANES - N_GROUPS - N_EXPERTS)).reshape(1, V7X_LANES)
        x1, h2_tiles, rt = _post_mixer(xf.reshape(b, t, d), mix, u, mq_off, kbd, vbd, w_o, norm_ffn[i],
                                       w_rt.astype(F32), b_rt.astype(F32), tm=256)
        xf = _moe_layer(x1.reshape(n, d), h2_tiles, rt.reshape(n, V7X_LANES),
                        moe_w1, moe_w3, moe_w2, i,
                        final_g=final_norm if i == depth - 1 else None)
    return xf.reshape(b, t, d)
```

```python
import functools

import numpy as np
import jax
import jax.numpy as jnp
from jax import lax
from jax.experimental import pallas as pl
from jax.experimental.pallas import tpu as pltpu

F32 = jnp.float32
BF16 = jnp.bfloat16

D_MODEL = 1024
GRID_W = 64
HEAD_DIM = 64
NORM_EPS = 1e-6
MEM_HEADS = 4
MEM_DIM = MEM_HEADS * HEAD_DIM
MIX_DIM = D_MODEL - MEM_DIM
NA_HEADS = MIX_DIM // HEAD_DIM
NA_WIN_H = 8
NA_WIN_W = 16
GLA_HEADS = 4
GLA_DK = MIX_DIM // 2 // GLA_HEADS
GLA_DV = MIX_DIM // GLA_HEADS
GLA_GATE_RANK = 16
GLA_TAU = 16.0
GLA_CHUNK = 64
N_GROUPS = 4
EXPERTS_PER_GROUP = 8
N_EXPERTS = N_GROUPS * EXPERTS_PER_GROUP
TOP_K = 2
EXPERT_FF = 512
MOE_BLOCK = 256

V7X_LANES = 128
V7X_MXU_DIM = 256
V7X_VMEM_LIMIT_BYTES = 56 * 1024 * 1024

GLA_DK_PAD = 128
GLA_DV_PAD = 256
GLA_Q_OFF = 0
GLA_K_OFF = GLA_Q_OFF + GLA_HEADS * GLA_DK_PAD
GLA_V_OFF = GLA_K_OFF + GLA_HEADS * GLA_DK_PAD
GLA_G_OFF = GLA_V_OFF + GLA_HEADS * GLA_DV_PAD
GLA_R_OFF = GLA_G_OFF + GLA_HEADS * GLA_DV_PAD
GLA_R_PAD = 256
GLA_MQ_OFF = GLA_R_OFF + GLA_R_PAD
GLA_IN_PAD = GLA_MQ_OFF + MEM_DIM
GLA_MIX_PAD = GLA_HEADS * GLA_DV_PAD
NA_MQ_OFF = 3 * MIX_DIM
NA_ROWS_PER_STEP = 4


def _cparams(semantics):
    return pltpu.CompilerParams(dimension_semantics=semantics, vmem_limit_bytes=V7X_VMEM_LIMIT_BYTES)


def _split_bf16(x, n_pieces):
    pieces = []
    rest = x
    for _ in range(n_pieces - 1):
        c = rest * 65537.0
        hi = c - (c - rest)
        pieces.append(hi.astype(BF16))
        rest = rest - hi
    pieces.append(rest.astype(BF16))
    return pieces


ROW_TILE = D_MODEL // V7X_LANES


def _store_row_tiles(ref, val):
    rows = val.shape[0]
    for s in range(ROW_TILE):
        ref[pl.ds(s, rows, stride=ROW_TILE), :] = val[:, s * V7X_LANES:(s + 1) * V7X_LANES]


def _load_row_tiles(ref, rows):
    return jnp.concatenate([ref[pl.ds(s, rows, stride=ROW_TILE), :] for s in range(ROW_TILE)], axis=1)


def _norm_matmul_kernel(x_ref, g_ref, w_ref, o_ref, *, col_chunk):
    x = x_ref[...]
    ms = jnp.mean(x * x, axis=-1, keepdims=True)
    y = (x * lax.rsqrt(ms + NORM_EPS) * g_ref[...]).astype(BF16)
    n_out = o_ref.shape[1]
    for c in range(0, n_out, col_chunk):
        o_ref[:, c:c + col_chunk] = jnp.dot(
            y, w_ref[:, c:c + col_chunk], preferred_element_type=F32).astype(o_ref.dtype)


def _norm_matmul(x, g, w, *, tm, out_dtype, name):
    n, d = x.shape
    n_out = w.shape[1]
    col_chunk = 512 if n_out % 512 == 0 else n_out
    return pl.pallas_call(
        functools.partial(_norm_matmul_kernel, col_chunk=col_chunk),
        grid=(n // tm,),
        in_specs=[pl.BlockSpec((tm, d), lambda i: (i, 0)),
                  pl.BlockSpec((1, d), lambda i: (0, 0)),
                  pl.BlockSpec((d, n_out), lambda i: (0, 0))],
        out_specs=pl.BlockSpec((tm, n_out), lambda i: (i, 0)),
        out_shape=jax.ShapeDtypeStruct((n, n_out), out_dtype),
        compiler_params=_cparams(("parallel",)),
        name=name,
    )(x, g.reshape(1, d), w)


def _na_kernel(q_ref, k_ref, v_ref, *rest):
    *bias_refs, o_ref = rest
    rows = k_ref.shape[1] // GRID_W
    n_keys = NA_WIN_H * GRID_W
    heads_per_slab = V7X_MXU_DIM // HEAD_DIM
    scale = HEAD_DIM ** -0.5
    lane_head = lax.broadcasted_iota(jnp.int32, (GRID_W, V7X_MXU_DIM), 1) // HEAD_DIM
    for rr, bias_ref in enumerate(bias_refs):
        r = pl.program_id(1) * len(bias_refs) + rr
        rs = jnp.clip(r - NA_WIN_H // 2, 0, rows - NA_WIN_H)
        start = pl.multiple_of(rs * GRID_W, GRID_W)
        qrow = slice(rr * GRID_W, (rr + 1) * GRID_W)
        for s in range(MIX_DIM // V7X_MXU_DIM):
            cs = slice(s * V7X_MXU_DIM, (s + 1) * V7X_MXU_DIM)
            qq = q_ref[0, qrow, cs] * scale
            kw = k_ref[0, pl.ds(start, n_keys), cs]
            vw = v_ref[0, pl.ds(start, n_keys), cs]
            lhs = jnp.concatenate(
                [jnp.where(lane_head == i, qq, jnp.zeros_like(qq)) for i in range(heads_per_slab)], axis=0)
            sc = lax.dot_general(lhs, kw, (((1,), (1,)), ((), ())), preferred_element_type=F32)
            sc = sc + bias_ref[0, s * V7X_MXU_DIM:(s + 1) * V7X_MXU_DIM, :]
            m = jnp.max(sc, axis=-1, keepdims=True)
            p = jnp.exp(sc - m)
            l = jnp.sum(p, axis=-1, keepdims=True)
            o = jnp.dot(p.astype(BF16), vw, preferred_element_type=F32)
            o = o * (1.0 / l)
            acc = jnp.zeros((GRID_W, V7X_MXU_DIM), F32)
            for i in range(heads_per_slab):
                acc = acc + jnp.where(lane_head == i, o[i * GRID_W:(i + 1) * GRID_W], 0.0)
            o_ref[0, qrow, cs] = acc.astype(o_ref.dtype)


def _na_bias_table(rpb):
    qc = np.arange(GRID_W)[:, None]
    kc = np.arange(GRID_W)[None, :]
    cstart = np.clip(qc - NA_WIN_W // 2, 0, GRID_W - NA_WIN_W)
    col_in = (kc >= cstart) & (kc < cstart + NA_WIN_W)
    dcol = np.clip(kc - qc, 1 - NA_WIN_W, NA_WIN_W - 1) + NA_WIN_W - 1
    pick = jnp.asarray(dcol[None] == np.arange(2 * NA_WIN_W - 1)[:, None, None], F32)
    rows = jnp.stack([rpb.astype(F32)[:, o:o + NA_WIN_H] for o in range(NA_WIN_H)])
    tbl = jnp.einsum('ohjd,dqk->ohqjk', rows, pick, precision=lax.Precision.HIGHEST)
    tbl = jnp.where(col_in[None, None, :, None, :], tbl, -jnp.inf)
    return tbl.reshape(NA_WIN_H, NA_HEADS * GRID_W, NA_WIN_H * GRID_W)


def _na_attention(u, rpb):
    b, t, _ = u.shape
    rows = t // GRID_W
    bias = _na_bias_table(rpb)

    def bias_spec(rr):
        def bias_idx(bi, g):
            r = g * NA_ROWS_PER_STEP + rr
            return (jnp.clip(r - NA_WIN_H // 2, 0, rows - NA_WIN_H) - r + NA_WIN_H - 1, 0, 0)
        return pl.BlockSpec((1, NA_HEADS * GRID_W, NA_WIN_H * GRID_W), bias_idx)

    qt = NA_ROWS_PER_STEP * GRID_W
    return pl.pallas_call(
        _na_kernel,
        grid=(b, rows // NA_ROWS_PER_STEP),
        in_specs=[pl.BlockSpec((1, qt, MIX_DIM), lambda bi, g: (bi, g, 0)),
                  pl.BlockSpec((1, t, MIX_DIM), lambda bi, g: (bi, 0, 1)),
                  pl.BlockSpec((1, t, MIX_DIM), lambda bi, g: (bi, 0, 2))]
        + [bias_spec(rr) for rr in range(NA_ROWS_PER_STEP)],
        out_specs=pl.BlockSpec((1, qt, MIX_DIM), lambda bi, g: (bi, g, 0)),
        out_shape=jax.ShapeDtypeStruct((b, t, MIX_DIM), BF16),
        compiler_params=_cparams(("parallel", "arbitrary")),
        name="na_attention",
    )(u, u, u, *([bias] * NA_ROWS_PER_STEP))


def _gla_kernel(*refs, reverse, final):
    if final:
        (q_ref, k_ref, v_ref, r_ref, gu_ref, gb_ref, ofwd_ref, g_ref, onorm_ref, o_ref, st_ref) = refs
    else:
        (q_ref, k_ref, v_ref, r_ref, gu_ref, gb_ref, o_ref, st_ref) = refs
    c = GLA_CHUNK

    @pl.when(pl.program_id(1) == 0)
    def _():
        st_ref[...] = jnp.zeros_like(st_ref)

    z = jnp.dot(r_ref[0], gu_ref[...], preferred_element_type=F32) + gb_ref[...]
    la = (jnp.minimum(z, 0.0) - jnp.log(1.0 + jnp.exp(-jnp.abs(z)))) * (1.0 / GLA_TAU)
    ri = lax.broadcasted_iota(jnp.int32, (c, c), 0)
    ci = lax.broadcasted_iota(jnp.int32, (c, c), 1)
    tri = (ci >= ri) if reverse else (ci <= ri)
    trib = jnp.where(tri, 1.0, 0.0).astype(BF16)
    la3 = jnp.concatenate(_split_bf16(la, 3), axis=1)
    gw = la.shape[1]
    mid = c // 2 if reverse else c // 2 - 1
    last = 0 if reverse else c - 1
    scale = GLA_DK ** -0.5
    n_chunks = q_ref.shape[1] // c
    order = range(n_chunks - 1, -1, -1) if reverse else range(n_chunks)
    for ch in order:
        sl = slice(ch * c, (ch + 1) * c)
        b3 = jnp.dot(trib, la3[sl], preferred_element_type=F32)
        bcum = b3[:, :gw] + b3[:, gw:2 * gw] + b3[:, 2 * gw:]
        b_mid = bcum[mid:mid + 1]
        b_last = bcum[last:last + 1]
        qc = q_ref[0, sl, :].astype(F32) * scale
        kc = k_ref[0, sl, :].astype(F32)
        vc = v_ref[0, sl, :]
        qe = (qc * jnp.exp(bcum - b_mid)).astype(BF16)
        ke = (kc * jnp.exp(b_mid - bcum)).astype(BF16)
        qs = (qc * jnp.exp(bcum)).astype(BF16)
        ks = (kc * jnp.exp(b_last - bcum)).astype(BF16)
        dec = jnp.exp(b_last)
        for h in range(GLA_HEADS):
            hs = slice(h * GLA_DK_PAD, (h + 1) * GLA_DK_PAD)
            vs = slice(h * GLA_DV_PAD, (h + 1) * GLA_DV_PAD)
            a = lax.dot_general(qe[:, hs], ke[:, hs], (((1,), (1,)), ((), ())), preferred_element_type=F32)
            a = jnp.where(tri, a, 0.0).astype(BF16)
            vh = vc[:, vs]
            st = st_ref[h]
            o = jnp.dot(a, vh, preferred_element_type=F32) + lax.dot_general(
                qs[:, hs], st.astype(BF16), (((1,), (1,)), ((), ())), preferred_element_type=F32)
            kv_t = lax.dot_general(vh, ks[:, hs], (((0,), (0,)), ((), ())), preferred_element_type=F32)
            st_ref[h] = st * dec[:, hs] + kv_t
            if final:
                tot = ofwd_ref[0, sl, vs] + o
                ms = jnp.sum(tot * tot, axis=-1, keepdims=True) * (1.0 / GLA_DV)
                y = tot * lax.rsqrt(ms + NORM_EPS) * onorm_ref[:, vs]
                g = g_ref[0, sl, vs].astype(F32)
                o_ref[0, sl, vs] = (y * (g / (1.0 + jnp.exp(-g)))).astype(o_ref.dtype)
            else:
                o_ref[0, sl, vs] = o


def _gla_direction(u, gu, gb, *, reverse, tb, ofwd=None, onorm=None):
    b, t, _ = u.shape
    nt = t // tb
    final = ofwd is not None
    tix = (lambda ti: nt - 1 - ti) if reverse else (lambda ti: ti)
    qw = GLA_HEADS * GLA_DK_PAD
    vw = GLA_HEADS * GLA_DV_PAD
    in_specs = [pl.BlockSpec((1, tb, qw), lambda bi, ti: (bi, tix(ti), GLA_Q_OFF // qw)),
                pl.BlockSpec((1, tb, qw), lambda bi, ti: (bi, tix(ti), GLA_K_OFF // qw)),
                pl.BlockSpec((1, tb, vw), lambda bi, ti: (bi, tix(ti), GLA_V_OFF // vw)),
                pl.BlockSpec((1, tb, GLA_R_PAD), lambda bi, ti: (bi, tix(ti), GLA_R_OFF // GLA_R_PAD)),
                pl.BlockSpec((GLA_R_PAD, qw), lambda bi, ti: (0, 0)),
                pl.BlockSpec((1, qw), lambda bi, ti: (0, 0))]
    args = [u, u, u, u, gu, gb]
    if final:
        in_specs += [pl.BlockSpec((1, tb, vw), lambda bi, ti: (bi, tix(ti), 0)),
                     pl.BlockSpec((1, tb, vw), lambda bi, ti: (bi, tix(ti), GLA_G_OFF // vw)),
                     pl.BlockSpec((1, vw), lambda bi, ti: (0, 0))]
        args += [ofwd, u, onorm]
    return pl.pallas_call(
        functools.partial(_gla_kernel, reverse=reverse, final=final),
        grid=(b, nt),
        in_specs=in_specs,
        out_specs=pl.BlockSpec((1, tb, vw), lambda bi, ti: (bi, tix(ti), 0)),
        out_shape=jax.ShapeDtypeStruct((b, t, vw), BF16 if final else F32),
        scratch_shapes=[pltpu.VMEM((GLA_HEADS, GLA_DV_PAD, GLA_DK_PAD), F32)],
        compiler_params=_cparams(("parallel", "arbitrary")),
        name="gla_bwd_final" if final else "gla_fwd",
    )(*args)


def _gla_weights(w_in, gate_up, gate_bias, out_norm):
    d = w_in.shape[0]
    kd = GLA_HEADS * GLA_DK
    wq, wk, wv, wg, wr, wmq = jnp.split(
        w_in, np.cumsum([kd, kd, MIX_DIM, MIX_DIM, 2 * GLA_GATE_RANK]), axis=1)

    def pad_heads(w, dh, dh_pad):
        w = w.reshape(w.shape[0], GLA_HEADS, dh)
        w = jnp.pad(w, ((0, 0), (0, 0), (0, dh_pad - dh)))
        return w.reshape(w.shape[0], GLA_HEADS * dh_pad)

    w_pad = jnp.concatenate([
        pad_heads(wq, GLA_DK, GLA_DK_PAD), pad_heads(wk, GLA_DK, GLA_DK_PAD),
        pad_heads(wv, GLA_DV, GLA_DV_PAD), pad_heads(wg, GLA_DV, GLA_DV_PAD),
        jnp.pad(wr, ((0, 0), (0, GLA_R_PAD - 2 * GLA_GATE_RANK))), wmq], axis=1)
    gu = []
    for di in range(2):
        up = pad_heads(gate_up[di], GLA_DK, GLA_DK_PAD)
        gu.append(jnp.pad(up, ((di * GLA_GATE_RANK, GLA_R_PAD - (di + 1) * GLA_GATE_RANK), (0, 0))))
    gb = [pad_heads(gate_bias[di][None, :], GLA_DK, GLA_DK_PAD) for di in range(2)]
    onorm = pad_heads(out_norm[None, :].repeat(GLA_HEADS, 0).reshape(1, MIX_DIM), GLA_DV, GLA_DV_PAD)
    return w_pad.astype(BF16), [g.astype(BF16) for g in gu], [x.astype(F32) for x in gb], onorm.astype(F32)


def _post_kernel(x_ref, mix_ref, mq_ref, kbd_ref, vbd_ref, wom_ref, woa_ref, g_ref, wrt_ref, brt_ref,
                 x1_ref, h2_ref, rt_ref, base_ref):
    @pl.when((pl.program_id(0) == 0) & (pl.program_id(1) == 0))
    def _():
        base_ref[...] = jnp.zeros_like(base_ref)

    mq = mq_ref[0]
    s = jnp.dot(mq, kbd_ref[0], preferred_element_type=F32) * (HEAD_DIM ** -0.5)
    n_mem = kbd_ref.shape[2] // MEM_HEADS
    ps = []
    for h in range(MEM_HEADS):
        seg = s[:, h * n_mem:(h + 1) * n_mem]
        e = jnp.exp(seg - jnp.max(seg, axis=-1, keepdims=True))
        ps.append((e * (1.0 / jnp.sum(e, axis=-1, keepdims=True))).astype(BF16))
    att = jnp.dot(jnp.concatenate(ps, axis=1), vbd_ref[0], preferred_element_type=F32).astype(BF16)
    x1 = (x_ref[0] + jnp.dot(mix_ref[0], wom_ref[...], preferred_element_type=F32)
          + jnp.dot(att, woa_ref[...], preferred_element_type=F32))
    x1_ref[0] = x1
    ms = jnp.mean(x1 * x1, axis=-1, keepdims=True)
    h2 = x1 * lax.rsqrt(ms + NORM_EPS) * g_ref[...]
    _store_row_tiles(h2_ref, h2)
    h_hi, h_lo = _split_bf16(h2, 2)
    hw = jnp.dot(h_hi, wrt_ref[...], preferred_element_type=F32)
    logits = (hw[:, :V7X_LANES] + hw[:, V7X_LANES:]
              + jnp.dot(h_lo, wrt_ref[:, :V7X_LANES], preferred_element_type=F32) + brt_ref[...])
    lane = lax.broadcasted_iota(jnp.int32, logits.shape, 1).astype(F32)
    neg = -jnp.inf
    big = 1e9
    gl = jnp.where(lane < N_GROUPS, logits, neg)
    gm = jnp.max(gl, axis=-1, keepdims=True)
    g_gate = 1.0 / jnp.sum(jnp.exp(gl - gm), axis=-1, keepdims=True)
    g_idx = jnp.min(jnp.where(gl == gm, lane, big), axis=-1, keepdims=True)
    lo = N_GROUPS + EXPERTS_PER_GROUP * g_idx
    v1 = jnp.where((lane >= lo) & (lane < lo + EXPERTS_PER_GROUP), logits, neg)
    m1 = jnp.max(v1, axis=-1, keepdims=True)
    i1 = jnp.min(jnp.where(v1 == m1, lane, big), axis=-1, keepdims=True)
    v2 = jnp.where(lane == i1, neg, v1)
    m2 = jnp.max(v2, axis=-1, keepdims=True)
    i2 = jnp.min(jnp.where(v2 == m2, lane, big), axis=-1, keepdims=True)
    t = jnp.exp(m2 - m1)
    w1 = g_gate / (1.0 + t)
    w2 = g_gate * t / (1.0 + t)
    e1 = i1 - N_GROUPS
    e2 = i2 - N_GROUPS
    tm = logits.shape[0]
    onehot = jnp.where(lane == e1, 1.0, 0.0) + jnp.where(lane == e2, 1.0, 0.0)
    ri = lax.broadcasted_iota(jnp.int32, (tm, tm), 0)
    ci = lax.broadcasted_iota(jnp.int32, (tm, tm), 1)
    before = jnp.where(ci < ri, 1.0, 0.0).astype(BF16)
    ahead = jnp.dot(before, onehot.astype(BF16), preferred_element_type=F32) + base_ref[...]
    pos1 = jnp.sum(jnp.where(lane == e1, ahead, 0.0), axis=-1, keepdims=True)
    pos2 = jnp.sum(jnp.where(lane == e2, ahead, 0.0), axis=-1, keepdims=True)
    base_ref[...] = base_ref[...] + jnp.sum(onehot, axis=0, keepdims=True)
    rt = jnp.where(lane == 0, e1,
                   jnp.where(lane == 1, e2,
                             jnp.where(lane == 2, w1,
                                       jnp.where(lane == 3, w2,
                                                 jnp.where(lane == 4, pos1, jnp.where(lane == 5, pos2, 0.0))))))
    rt_ref[0] = rt


def _post_mixer(x, mix, u, mq_off, kbd, vbd, w_out, g, w_rt, b_rt, *, tm):
    b, t, d = x.shape
    mixw = mix.shape[2]
    wom = w_out[0]
    woa = w_out[1]
    return pl.pallas_call(
        _post_kernel,
        grid=(b, t // tm),
        in_specs=[pl.BlockSpec((1, tm, d), lambda bi, ti: (bi, ti, 0)),
                  pl.BlockSpec((1, tm, mixw), lambda bi, ti: (bi, ti, 0)),
                  pl.BlockSpec((1, tm, MEM_DIM), lambda bi, ti: (bi, ti, mq_off // MEM_DIM)),
                  pl.BlockSpec((1,) + kbd.shape[1:], lambda bi, ti: (bi, 0, 0)),
                  pl.BlockSpec((1,) + vbd.shape[1:], lambda bi, ti: (bi, 0, 0)),
                  pl.BlockSpec(wom.shape, lambda bi, ti: (0, 0)),
                  pl.BlockSpec(woa.shape, lambda bi, ti: (0, 0)),
                  pl.BlockSpec((1, d), lambda bi, ti: (0, 0)),
                  pl.BlockSpec(w_rt.shape, lambda bi, ti: (0, 0)),
                  pl.BlockSpec((1, V7X_LANES), lambda bi, ti: (0, 0))],
        out_specs=[pl.BlockSpec((1, tm, d), lambda bi, ti: (bi, ti, 0)),
                   pl.BlockSpec((tm * ROW_TILE, V7X_LANES), lambda bi, ti: (bi * (t // tm) + ti, 0)),
                   pl.BlockSpec((1, tm, V7X_LANES), lambda bi, ti: (bi, ti, 0))],
        out_shape=[jax.ShapeDtypeStruct((b, t, d), F32),
                   jax.ShapeDtypeStruct((b * t * ROW_TILE, V7X_LANES), F32),
                   jax.ShapeDtypeStruct((b, t, V7X_LANES), F32)],
        scratch_shapes=[pltpu.VMEM((1, V7X_LANES), F32)],
        compiler_params=_cparams(("arbitrary", "arbitrary")),
        name="post_mixer",
    )(x, mix, u, kbd, vbd, wom, woa, g.reshape(1, d), w_rt, b_rt)


def _memory_kv_blockdiag(mkv):
    k, v = jnp.split(mkv, 2, axis=-1)
    head_of = np.arange(MEM_DIM) // HEAD_DIM
    sel = jnp.asarray(head_of[None, :] == np.arange(MEM_HEADS)[:, None], F32)
    kbd = jnp.einsum('bmd,hd->bdhm', k, sel).reshape(k.shape[0], MEM_DIM, -1)
    vbd = jnp.einsum('bmd,hd->bhmd', v, sel).reshape(v.shape[0], -1, MEM_DIM)
    return kbd.astype(BF16), vbd.astype(BF16)


def _moe_kernel(bexp_ref, brow_ref, bvalid_ref, bfirst_ref, xs_ref, w1_ref, w3_ref, w2_ref, y_ref, wb1, wb3, wb2):
    del brow_ref
    blk = pl.program_id(0)

    @pl.when(bvalid_ref[blk] == 0)
    def _():
        y_ref[...] = jnp.zeros_like(y_ref)

    @pl.when(bvalid_ref[blk] != 0)
    def _():
        @pl.when(bfirst_ref[blk] != 0)
        def _():
            wb1[...] = w1_ref[0, 0].astype(BF16)
            wb3[...] = w3_ref[0, 0].astype(BF16)
            wb2[...] = w2_ref[0, 0].astype(BF16)

        xb = _load_row_tiles(xs_ref, MOE_BLOCK).astype(BF16)
        a = jnp.dot(xb, wb1[...], preferred_element_type=F32)
        c = jnp.dot(xb, wb3[...], preferred_element_type=F32)
        hid = (a / (1.0 + jnp.exp(-a)) * c).astype(BF16)
        _store_row_tiles(y_ref, jnp.dot(hid, wb2[...], preferred_element_type=F32))


def _moe_ffn(xs, block_exp, block_row, block_valid, block_first, w1, w3, w2, layer):
    n_steps = block_exp.shape[0]
    d = w1.shape[2]
    ff = w1.shape[3]
    blk_rows = MOE_BLOCK * ROW_TILE
    grid_spec = pltpu.PrefetchScalarGridSpec(
        num_scalar_prefetch=4,
        grid=(n_steps,),
        in_specs=[pl.BlockSpec((blk_rows, V7X_LANES), lambda i, be, br, bv, bf: (br[i], 0)),
                  pl.BlockSpec((1, 1, d, ff), lambda i, be, br, bv, bf: (layer, be[i], 0, 0)),
                  pl.BlockSpec((1, 1, d, ff), lambda i, be, br, bv, bf: (layer, be[i], 0, 0)),
                  pl.BlockSpec((1, 1, ff, d), lambda i, be, br, bv, bf: (layer, be[i], 0, 0))],
        out_specs=pl.BlockSpec((blk_rows, V7X_LANES), lambda i, be, br, bv, bf: (i, 0)),
        scratch_shapes=[pltpu.VMEM((d, ff), BF16), pltpu.VMEM((d, ff), BF16), pltpu.VMEM((ff, d), BF16)],
    )
    return pl.pallas_call(
        _moe_kernel,
        grid_spec=grid_spec,
        out_shape=jax.ShapeDtypeStruct(xs.shape, F32),
        compiler_params=_cparams(("arbitrary",)),
        name="moe_ffn",
    )(block_exp, block_row, block_valid, block_first, xs, w1, w3, w2)


def _row_slice(row, n_rows):
    return pl.ds(pl.multiple_of(row * ROW_TILE, ROW_TILE), n_rows * ROW_TILE)


def _pow2_pieces(n, largest, act):
    k = largest
    while k >= 1:
        shift = k.bit_length()

        @pl.when((n & k) != 0)
        def _(k=k, shift=shift):
            act((n >> shift) << shift, k)
        k //= 2


def _run_copies(runs_ref, make_copy, act):
    def per_expert(e, carry):
        n = runs_ref[0, 0, e]
        local_row = runs_ref[0, 1, e]
        sorted_row = runs_ref[0, 2, e]

        def chunk(i, c):
            act(make_copy(local_row + 8 * i, sorted_row + 8 * i, 8))
            return c

        lax.fori_loop(0, n >> 3, chunk, 0)
        _pow2_pieces(n, 4, lambda off, k: act(make_copy(local_row + off, sorted_row + off, k)))
        return carry

    lax.fori_loop(0, N_EXPERTS, per_expert, 0)


def _dispatch_kernel(seg_ref, lpos_ref, runs_ref, h_ref, xs_ref, local, zbuf, sem):
    tm = h_ref.shape[0] // ROW_TILE
    n_rows = xs_ref.shape[0] // ROW_TILE
    zrows = zbuf.shape[0] // ROW_TILE

    def place(tok, carry):
        row = h_ref[_row_slice(tok, 1), :]
        for slot in range(TOP_K):
            local[_row_slice(lpos_ref[0, 0, TOP_K * tok + slot], 1), :] = row
        return carry

    lax.fori_loop(0, tm, place, 0, unroll=8)

    def to_sorted(local_row, sorted_row, n_rows):
        return pltpu.make_async_copy(local.at[_row_slice(local_row, n_rows)],
                                     xs_ref.at[_row_slice(sorted_row, n_rows)], sem)

    _run_copies(runs_ref, to_sorted, lambda cp: cp.start())
    _run_copies(runs_ref, to_sorted, lambda cp: cp.wait())

    @pl.when(pl.program_id(0) == pl.num_programs(0) - 1)
    def _():
        zbuf[...] = jnp.zeros_like(zbuf)

        def zero_copy(first_row, k):
            return pltpu.make_async_copy(zbuf.at[_row_slice(0, k)], xs_ref.at[_row_slice(first_row, k)], sem)

        def pad_copies(act):
            def per_expert(e, carry):
                cnt = seg_ref[0, e]
                first = seg_ref[1, e] + cnt
                _pow2_pieces((-cnt) & (MOE_BLOCK - 1), zrows, lambda off, k: act(zero_copy(first + off, k)))
                return carry

            lax.fori_loop(0, N_EXPERTS, per_expert, 0)
            used = seg_ref[2, 0]

            def tail(i, carry):
                act(zero_copy(used + i * zrows, zrows))
                return carry

            lax.fori_loop(0, (n_rows - used) // zrows, tail, 0)

        pad_copies(lambda cp: cp.start())
        pad_copies(lambda cp: cp.wait())


def _dispatch(h2_tiles, seg, lpos, runs, n_rows, *, tm):
    n_tok = h2_tiles.shape[0] // ROW_TILE
    nt = n_tok // tm
    grid_spec = pltpu.PrefetchScalarGridSpec(
        num_scalar_prefetch=1,
        grid=(nt,),
        in_specs=[pl.BlockSpec((1, 1, TOP_K * tm), lambda i, c: (i, 0, 0), memory_space=pltpu.SMEM),
                  pl.BlockSpec((1, 3, N_EXPERTS), lambda i, c: (i, 0, 0), memory_space=pltpu.SMEM),
                  pl.BlockSpec((tm * ROW_TILE, V7X_LANES), lambda i, c: (i, 0))],
        out_specs=pl.BlockSpec(memory_space=pl.ANY),
        scratch_shapes=[pltpu.VMEM((TOP_K * tm * ROW_TILE, V7X_LANES), F32),
                        pltpu.VMEM((MOE_BLOCK // 2 * ROW_TILE, V7X_LANES), F32),
                        pltpu.SemaphoreType.DMA(())],
    )
    return pl.pallas_call(
        _dispatch_kernel,
        grid_spec=grid_spec,
        out_shape=jax.ShapeDtypeStruct((n_rows * ROW_TILE, V7X_LANES), F32),
        compiler_params=_cparams(("arbitrary",)),
        name="moe_dispatch",
    )(seg, lpos, runs, h2_tiles)


def _combine_kernel(*refs, final):
    if final:
        lpos_ref, gate_ref, runs_ref, x1_ref, yb_ref, g_ref, o_ref, local, acc, sem = refs
    else:
        lpos_ref, gate_ref, runs_ref, x1_ref, yb_ref, o_ref, local, acc, sem = refs
    tm = x1_ref.shape[0]

    def from_sorted(local_row, sorted_row, n_rows):
        return pltpu.make_async_copy(yb_ref.at[_row_slice(sorted_row, n_rows)],
                                     local.at[_row_slice(local_row, n_rows)], sem)

    _run_copies(runs_ref, from_sorted, lambda cp: cp.start())
    _run_copies(runs_ref, from_sorted, lambda cp: cp.wait())

    def mix(tok, carry):
        y = jnp.zeros((ROW_TILE, V7X_LANES), F32)
        for slot in range(TOP_K):
            idx = TOP_K * tok + slot
            y = y + gate_ref[0, 0, idx] * local[_row_slice(lpos_ref[0, 0, idx], 1), :]
        acc[_row_slice(tok, 1), :] = y
        return carry

    lax.fori_loop(0, tm, mix, 0, unroll=8)
    x2 = x1_ref[...] + _load_row_tiles(acc, tm)
    if final:
        ms = jnp.mean(x2 * x2, axis=-1, keepdims=True)
        x2 = x2 * lax.rsqrt(ms + NORM_EPS) * g_ref[...]
    o_ref[...] = x2


def _combine(x1, yb, lpos, gates, runs, *, tm, final_g=None):
    n_tok, d = x1.shape
    nt = n_tok // tm
    final = final_g is not None
    in_specs = [pl.BlockSpec((1, 1, TOP_K * tm), lambda i: (i, 0, 0), memory_space=pltpu.SMEM),
                pl.BlockSpec((1, 1, TOP_K * tm), lambda i: (i, 0, 0), memory_space=pltpu.SMEM),
                pl.BlockSpec((1, 3, N_EXPERTS), lambda i: (i, 0, 0), memory_space=pltpu.SMEM),
                pl.BlockSpec((tm, d), lambda i: (i, 0)),
                pl.BlockSpec(memory_space=pl.ANY)]
    args = [lpos, gates, runs, x1, yb]
    if final:
        in_specs.append(pl.BlockSpec((1, d), lambda i: (0, 0)))
        args.append(final_g.reshape(1, d))
    return pl.pallas_call(
        functools.partial(_combine_kernel, final=final),
        grid=(nt,),
        in_specs=in_specs,
        out_specs=pl.BlockSpec((tm, d), lambda i: (i, 0)),
        out_shape=jax.ShapeDtypeStruct((n_tok, d), F32),
        scratch_shapes=[pltpu.VMEM((TOP_K * tm * ROW_TILE, V7X_LANES), F32),
                        pltpu.VMEM((tm * ROW_TILE, V7X_LANES), F32),
                        pltpu.SemaphoreType.DMA(())],
        compiler_params=_cparams(("arbitrary",)),
        name="moe_combine_final" if final else "moe_combine",
    )(*args)


def _moe_plan(rt, n_tok, tm):
    nt = n_tok // tm
    expert = rt[:, 0:TOP_K].astype(jnp.int32)
    gate = rt[:, TOP_K:2 * TOP_K]
    pos = rt[:, 2 * TOP_K:3 * TOP_K].astype(jnp.int32)
    is_e = (expert[:, :, None] == jnp.arange(N_EXPERTS, dtype=jnp.int32)).reshape(nt, tm, TOP_K, N_EXPERTS)
    cnt_tile = jnp.sum(is_e, axis=(1, 2), dtype=jnp.int32)
    seen_before = jnp.cumsum(cnt_tile, axis=0) - cnt_tile
    local_start = jnp.cumsum(cnt_tile, axis=1) - cnt_tile
    counts = jnp.sum(cnt_tile, axis=0)
    shift = local_start - seen_before
    lpos = pos.reshape(nt, tm, TOP_K) + jnp.sum(jnp.where(is_e, shift[:, None, None, :], 0), axis=-1)
    nblk = (counts + MOE_BLOCK - 1) // MOE_BLOCK
    bend = jnp.cumsum(nblk)
    seg_start = (bend - nblk) * MOE_BLOCK
    sorted_start = seg_start[None, :] + seen_before
    runs = jnp.stack([cnt_tile, local_start, sorted_start], axis=1)
    seg = jnp.stack([counts, seg_start, jnp.full((N_EXPERTS,), bend[-1] * MOE_BLOCK, jnp.int32)])
    n_steps = n_tok * TOP_K // MOE_BLOCK + N_EXPERTS
    step = jnp.arange(n_steps, dtype=jnp.int32)
    valid = step < bend[-1]
    brow = jnp.minimum(step, bend[-1] - 1)
    bexp = jnp.sum(brow[:, None] >= bend[None, :], axis=1).astype(jnp.int32)
    bfirst = jnp.concatenate([jnp.ones((1,), jnp.int32), (bexp[1:] != bexp[:-1]).astype(jnp.int32)])
    return dict(seg=seg.astype(jnp.int32), n_rows=n_steps * MOE_BLOCK,
                lpos=lpos.reshape(nt, 1, TOP_K * tm), gates=gate.reshape(nt, 1, TOP_K * tm),
                runs=runs, bexp=bexp, brow=brow.astype(jnp.int32), bvalid=valid.astype(jnp.int32), bfirst=bfirst)


def _moe_layer(x1, h2_tiles, rt, w1, w3, w2, layer, *, final_g=None):
    n_tok = x1.shape[0]
    tm = 1024
    plan = _moe_plan(rt, n_tok, tm)
    xs = _dispatch(h2_tiles, plan["seg"], plan["lpos"], plan["runs"], plan["n_rows"], tm=tm)
    yb = _moe_ffn(xs, plan["bexp"], plan["brow"], plan["bvalid"], plan["bfirst"], w1, w3, w2, layer)
    return _combine(x1, yb, plan["lpos"], plan["gates"], plan["runs"], tm=tm, final_g=final_g)


def kernel(x, mem, mem_norm, final_norm, norm_mix, norm_ffn, w_mem_kv, w_out, na_w_in, na_rpb, gla_w_in,
           gla_gate_up, gla_gate_bias, gla_out_norm, moe_w_group, moe_b_group, moe_w_router, moe_b_router,
           moe_w1, moe_w3, moe_w2):
    b, t, d = x.shape
    n = b * t
    depth = norm_mix.shape[0]
    n_mem = mem.shape[1]
    xf = x.reshape(n, d)
    for i in range(depth):
        j = i // 2
        mkv = _norm_matmul(mem.reshape(b * n_mem, d), mem_norm, w_mem_kv[i].astype(BF16),
                           tm=256, out_dtype=F32, name="mem_kv_proj").reshape(b, n_mem, 2 * MEM_DIM)
        kbd, vbd = _memory_kv_blockdiag(mkv)
        if i % 2 == 0:
            u = _norm_matmul(xf, norm_mix[i], na_w_in[j].astype(BF16), tm=512, out_dtype=BF16,
                             name="na_in_proj").reshape(b, t, -1)
            mix = _na_attention(u, na_rpb[j])
            mq_off = NA_MQ_OFF
            w_o = w_out[i].astype(BF16)
            w_o = (w_o[:MIX_DIM], w_o[MIX_DIM:])
        else:
            w_pad, gu, gb, onorm = _gla_weights(gla_w_in[j], gla_gate_up[j], gla_gate_bias[j], gla_out_norm[j])
            u = _norm_matmul(xf, norm_mix[i], w_pad, tm=512, out_dtype=BF16, name="gla_in_proj").reshape(b, t, -1)
            ofwd = _gla_direction(u, gu[0], gb[0], reverse=False, tb=512)
            mix = _gla_direction(u, gu[1], gb[1], reverse=True, tb=512, ofwd=ofwd, onorm=onorm)
            mq_off = GLA_MQ_OFF
            w_mix = w_out[i][:MIX_DIM].reshape(GLA_HEADS, GLA_DV, d)
            w_mix = jnp.pad(w_mix, ((0, 0), (0, GLA_DV_PAD - GLA_DV), (0, 0))).reshape(GLA_MIX_PAD, d)
            w_o = (w_mix.astype(BF16), w_out[i][MIX_DIM:].astype(BF16))
        w_rt = jnp.pad(jnp.concatenate([moe_w_group[i], moe_w_router[i]], axis=1),
                       ((0, 0), (0, V7X_LANES - N_GROUPS - N_EXPERTS))).astype(F32)
        w_rt = jnp.concatenate(_split_bf16(w_rt, 2), axis=1)
        b_rt = jnp.pad(jnp.concatenate([moe_b_group[i], moe_b_router[i]]),
                       (0, V7X_LANES - N_GROUPS - N_EXPERTS)).reshape(1, V7X_LANES)
        x1, h2_tiles, rt = _post_mixer(xf.reshape(b, t, d), mix, u, mq_off, kbd, vbd, w_o, norm_ffn[i],
                                       w_rt, b_rt.astype(F32), tm=256)
        xf = _moe_layer(x1.reshape(n, d), h2_tiles, rt.reshape(n, V7X_LANES),
                        moe_w1, moe_w3, moe_w2, i,
                        final_g=final_norm if i == depth - 1 else None)
    return xf.reshape(b, t, d)
```

```python
import functools

import numpy as np
import jax
import jax.numpy as jnp
from jax import lax
from jax.experimental import pallas as pl
from jax.experimental.pallas import tpu as pltpu

F32 = jnp.float32
BF16 = jnp.bfloat16

D_MODEL = 1024
GRID_W = 64
HEAD_DIM = 64
NORM_EPS = 1e-6
MEM_HEADS = 4
MEM_DIM = MEM_HEADS * HEAD_DIM
MIX_DIM = D_MODEL - MEM_DIM
NA_HEADS = MIX_DIM // HEAD_DIM
NA_WIN_H = 8
NA_WIN_W = 16
GLA_HEADS = 4
GLA_DK = MIX_DIM // 2 // GLA_HEADS
GLA_DV = MIX_DIM // GLA_HEADS
GLA_GATE_RANK = 16
GLA_TAU = 16.0
GLA_CHUNK = 64
N_GROUPS = 4
EXPERTS_PER_GROUP = 8
N_EXPERTS = N_GROUPS * EXPERTS_PER_GROUP
TOP_K = 2
EXPERT_FF = 512
MOE_BLOCK = 256

V7X_LANES = 128
V7X_MXU_DIM = 256
V7X_VMEM_LIMIT_BYTES = 56 * 1024 * 1024

GLA_DK_PAD = 128
GLA_DV_PAD = 256
GLA_Q_OFF = 0
GLA_K_OFF = GLA_Q_OFF + GLA_HEADS * GLA_DK_PAD
GLA_V_OFF = GLA_K_OFF + GLA_HEADS * GLA_DK_PAD
GLA_G_OFF = GLA_V_OFF + GLA_HEADS * GLA_DV_PAD
GLA_R_OFF = GLA_G_OFF + GLA_HEADS * GLA_DV_PAD
GLA_R_PAD = 256
GLA_MQ_OFF = GLA_R_OFF + GLA_R_PAD
GLA_IN_PAD = GLA_MQ_OFF + MEM_DIM
GLA_MIX_PAD = GLA_HEADS * GLA_DV_PAD
NA_MQ_OFF = 3 * MIX_DIM
NA_ROWS_PER_STEP = 4


def _cparams(semantics):
    return pltpu.CompilerParams(dimension_semantics=semantics, vmem_limit_bytes=V7X_VMEM_LIMIT_BYTES)


def _split_bf16(x, n_pieces):
    pieces = []
    rest = x
    for _ in range(n_pieces - 1):
        c = rest * 65537.0
        hi = c - (c - rest)
        pieces.append(hi.astype(BF16))
        rest = rest - hi
    pieces.append(rest.astype(BF16))
    return pieces


ROW_TILE = D_MODEL // V7X_LANES


def _store_row_tiles(ref, val):
    rows = val.shape[0]
    for s in range(ROW_TILE):
        ref[pl.ds(s, rows, stride=ROW_TILE), :] = val[:, s * V7X_LANES:(s + 1) * V7X_LANES]


def _load_row_tiles(ref, rows):
    return jnp.concatenate([ref[pl.ds(s, rows, stride=ROW_TILE), :] for s in range(ROW_TILE)], axis=1)


def _norm_matmul_kernel(x_ref, g_ref, w_ref, o_ref, *, col_chunk):
    x = x_ref[...]
    ms = jnp.mean(x * x, axis=-1, keepdims=True)
    y = (x * lax.rsqrt(ms + NORM_EPS) * g_ref[...]).astype(BF16)
    n_out = o_ref.shape[1]
    for c in range(0, n_out, col_chunk):
        o_ref[:, c:c + col_chunk] = jnp.dot(
            y, w_ref[:, c:c + col_chunk], preferred_element_type=F32).astype(o_ref.dtype)


def _norm_matmul(x, g, w, *, tm, out_dtype, name):
    n, d = x.shape
    n_out = w.shape[1]
    col_chunk = 512 if n_out % 512 == 0 else n_out
    return pl.pallas_call(
        functools.partial(_norm_matmul_kernel, col_chunk=col_chunk),
        grid=(n // tm,),
        in_specs=[pl.BlockSpec((tm, d), lambda i: (i, 0)),
                  pl.BlockSpec((1, d), lambda i: (0, 0)),
                  pl.BlockSpec((d, n_out), lambda i: (0, 0))],
        out_specs=pl.BlockSpec((tm, n_out), lambda i: (i, 0)),
        out_shape=jax.ShapeDtypeStruct((n, n_out), out_dtype),
        compiler_params=_cparams(("parallel",)),
        name=name,
    )(x, g.reshape(1, d), w)


def _na_kernel(q_ref, k_ref, v_ref, *rest):
    *bias_refs, o_ref = rest
    rows = k_ref.shape[1] // GRID_W
    n_keys = NA_WIN_H * GRID_W
    heads_per_slab = V7X_MXU_DIM // HEAD_DIM
    scale = HEAD_DIM ** -0.5
    lane_head = lax.broadcasted_iota(jnp.int32, (GRID_W, V7X_MXU_DIM), 1) // HEAD_DIM
    for rr, bias_ref in enumerate(bias_refs):
        r = pl.program_id(1) * len(bias_refs) + rr
        rs = jnp.clip(r - NA_WIN_H // 2, 0, rows - NA_WIN_H)
        start = pl.multiple_of(rs * GRID_W, GRID_W)
        qrow = slice(rr * GRID_W, (rr + 1) * GRID_W)
        for s in range(MIX_DIM // V7X_MXU_DIM):
            cs = slice(s * V7X_MXU_DIM, (s + 1) * V7X_MXU_DIM)
            qq = q_ref[0, qrow, cs] * scale
            kw = k_ref[0, pl.ds(start, n_keys), cs]
            vw = v_ref[0, pl.ds(start, n_keys), cs]
            lhs = jnp.concatenate(
                [jnp.where(lane_head == i, qq, jnp.zeros_like(qq)) for i in range(heads_per_slab)], axis=0)
            sc = lax.dot_general(lhs, kw, (((1,), (1,)), ((), ())), preferred_element_type=F32)
            sc = sc + bias_ref[0, s * V7X_MXU_DIM:(s + 1) * V7X_MXU_DIM, :]
            m = jnp.max(sc, axis=-1, keepdims=True)
            p = jnp.exp(sc - m)
            l = jnp.sum(p, axis=-1, keepdims=True)
            o = jnp.dot(p.astype(BF16), vw, preferred_element_type=F32)
            o = o * (1.0 / l)
            acc = jnp.zeros((GRID_W, V7X_MXU_DIM), F32)
            for i in range(heads_per_slab):
                acc = acc + jnp.where(lane_head == i, o[i * GRID_W:(i + 1) * GRID_W], 0.0)
            o_ref[0, qrow, cs] = acc.astype(o_ref.dtype)


def _na_bias_table(rpb):
    qc = np.arange(GRID_W)[:, None]
    kc = np.arange(GRID_W)[None, :]
    cstart = np.clip(qc - NA_WIN_W // 2, 0, GRID_W - NA_WIN_W)
    col_in = (kc >= cstart) & (kc < cstart + NA_WIN_W)
    dcol = np.clip(kc - qc, 1 - NA_WIN_W, NA_WIN_W - 1) + NA_WIN_W - 1
    pick = jnp.asarray(dcol[None] == np.arange(2 * NA_WIN_W - 1)[:, None, None], F32)
    rows = jnp.stack([rpb.astype(F32)[:, o:o + NA_WIN_H] for o in range(NA_WIN_H)])
    tbl = jnp.einsum('ohjd,dqk->ohqjk', rows, pick, precision=lax.Precision.HIGHEST)
    tbl = jnp.where(col_in[None, None, :, None, :], tbl, -jnp.inf)
    return tbl.reshape(NA_WIN_H, NA_HEADS * GRID_W, NA_WIN_H * GRID_W)


def _na_attention(u, rpb):
    b, t, _ = u.shape
    rows = t // GRID_W
    bias = _na_bias_table(rpb)

    def bias_spec(rr):
        def bias_idx(bi, g):
            r = g * NA_ROWS_PER_STEP + rr
            return (jnp.clip(r - NA_WIN_H // 2, 0, rows - NA_WIN_H) - r + NA_WIN_H - 1, 0, 0)
        return pl.BlockSpec((1, NA_HEADS * GRID_W, NA_WIN_H * GRID_W), bias_idx)

    qt = NA_ROWS_PER_STEP * GRID_W
    return pl.pallas_call(
        _na_kernel,
        grid=(b, rows // NA_ROWS_PER_STEP),
        in_specs=[pl.BlockSpec((1, qt, MIX_DIM), lambda bi, g: (bi, g, 0)),
                  pl.BlockSpec((1, t, MIX_DIM), lambda bi, g: (bi, 0, 1)),
                  pl.BlockSpec((1, t, MIX_DIM), lambda bi, g: (bi, 0, 2))]
        + [bias_spec(rr) for rr in range(NA_ROWS_PER_STEP)],
        out_specs=pl.BlockSpec((1, qt, MIX_DIM), lambda bi, g: (bi, g, 0)),
        out_shape=jax.ShapeDtypeStruct((b, t, MIX_DIM), BF16),
        compiler_params=_cparams(("parallel", "arbitrary")),
        name="na_attention",
    )(u, u, u, *([bias] * NA_ROWS_PER_STEP))


def _gla_kernel(*refs, reverse, final):
    if final:
        (q_ref, k_ref, v_ref, r_ref, gu_ref, gb_ref, ofwd_ref, g_ref, onorm_ref, o_ref, st_ref) = refs
    else:
        (q_ref, k_ref, v_ref, r_ref, gu_ref, gb_ref, o_ref, st_ref) = refs
    c = GLA_CHUNK

    @pl.when(pl.program_id(1) == 0)
    def _():
        st_ref[...] = jnp.zeros_like(st_ref)

    z = jnp.dot(r_ref[0], gu_ref[...], preferred_element_type=F32) + gb_ref[...]
    la = (jnp.minimum(z, 0.0) - jnp.log(1.0 + jnp.exp(-jnp.abs(z)))) * (1.0 / GLA_TAU)
    ri = lax.broadcasted_iota(jnp.int32, (c, c), 0)
    ci = lax.broadcasted_iota(jnp.int32, (c, c), 1)
    tri = (ci >= ri) if reverse else (ci <= ri)
    trib = jnp.where(tri, 1.0, 0.0).astype(BF16)
    la3 = jnp.concatenate(_split_bf16(la, 3), axis=1)
    gw = la.shape[1]
    mid = c // 2 if reverse else c // 2 - 1
    last = 0 if reverse else c - 1
    scale = GLA_DK ** -0.5
    n_chunks = q_ref.shape[1] // c
    order = range(n_chunks - 1, -1, -1) if reverse else range(n_chunks)
    for ch in order:
        sl = slice(ch * c, (ch + 1) * c)
        b3 = jnp.dot(trib, la3[sl], preferred_element_type=F32)
        bcum = b3[:, :gw] + b3[:, gw:2 * gw] + b3[:, 2 * gw:]
        b_mid = bcum[mid:mid + 1]
        b_last = bcum[last:last + 1]
        qc = q_ref[0, sl, :].astype(F32) * scale
        kc = k_ref[0, sl, :].astype(F32)
        vc = v_ref[0, sl, :]
        qe = (qc * jnp.exp(bcum - b_mid)).astype(BF16)
        ke = (kc * jnp.exp(b_mid - bcum)).astype(BF16)
        qs = (qc * jnp.exp(bcum)).astype(BF16)
        ks = (kc * jnp.exp(b_last - bcum)).astype(BF16)
        dec = jnp.exp(b_last)
        for h in range(GLA_HEADS):
            hs = slice(h * GLA_DK_PAD, (h + 1) * GLA_DK_PAD)
            vs = slice(h * GLA_DV_PAD, (h + 1) * GLA_DV_PAD)
            a = lax.dot_general(qe[:, hs], ke[:, hs], (((1,), (1,)), ((), ())), preferred_element_type=F32)
            a = jnp.where(tri, a, 0.0).astype(BF16)
            vh = vc[:, vs]
            st = st_ref[h]
            o = jnp.dot(a, vh, preferred_element_type=F32) + lax.dot_general(
                qs[:, hs], st.astype(BF16), (((1,), (1,)), ((), ())), preferred_element_type=F32)
            kv_t = lax.dot_general(vh, ks[:, hs], (((0,), (0,)), ((), ())), preferred_element_type=F32)
            st_ref[h] = st * dec[:, hs] + kv_t
            if final:
                tot = ofwd_ref[0, sl, vs] + o
                ms = jnp.sum(tot * tot, axis=-1, keepdims=True) * (1.0 / GLA_DV)
                y = tot * lax.rsqrt(ms + NORM_EPS) * onorm_ref[:, vs]
                g = g_ref[0, sl, vs].astype(F32)
                o_ref[0, sl, vs] = (y * (g / (1.0 + jnp.exp(-g)))).astype(o_ref.dtype)
            else:
                o_ref[0, sl, vs] = o


def _gla_direction(u, gu, gb, *, reverse, tb, ofwd=None, onorm=None):
    b, t, _ = u.shape
    nt = t // tb
    final = ofwd is not None
    tix = (lambda ti: nt - 1 - ti) if reverse else (lambda ti: ti)
    qw = GLA_HEADS * GLA_DK_PAD
    vw = GLA_HEADS * GLA_DV_PAD
    in_specs = [pl.BlockSpec((1, tb, qw), lambda bi, ti: (bi, tix(ti), GLA_Q_OFF // qw)),
                pl.BlockSpec((1, tb, qw), lambda bi, ti: (bi, tix(ti), GLA_K_OFF // qw)),
                pl.BlockSpec((1, tb, vw), lambda bi, ti: (bi, tix(ti), GLA_V_OFF // vw)),
                pl.BlockSpec((1, tb, GLA_R_PAD), lambda bi, ti: (bi, tix(ti), GLA_R_OFF // GLA_R_PAD)),
                pl.BlockSpec((GLA_R_PAD, qw), lambda bi, ti: (0, 0)),
                pl.BlockSpec((1, qw), lambda bi, ti: (0, 0))]
    args = [u, u, u, u, gu, gb]
    if final:
        in_specs += [pl.BlockSpec((1, tb, vw), lambda bi, ti: (bi, tix(ti), 0)),
                     pl.BlockSpec((1, tb, vw), lambda bi, ti: (bi, tix(ti), GLA_G_OFF // vw)),
                     pl.BlockSpec((1, vw), lambda bi, ti: (0, 0))]
        args += [ofwd, u, onorm]
    return pl.pallas_call(
        functools.partial(_gla_kernel, reverse=reverse, final=final),
        grid=(b, nt),
        in_specs=in_specs,
        out_specs=pl.BlockSpec((1, tb, vw), lambda bi, ti: (bi, tix(ti), 0)),
        out_shape=jax.ShapeDtypeStruct((b, t, vw), BF16 if final else F32),
        scratch_shapes=[pltpu.VMEM((GLA_HEADS, GLA_DV_PAD, GLA_DK_PAD), F32)],
        compiler_params=_cparams(("parallel", "arbitrary")),
        name="gla_bwd_final" if final else "gla_fwd",
    )(*args)


def _gla_weights(w_in, gate_up, gate_bias, out_norm):
    d = w_in.shape[0]
    kd = GLA_HEADS * GLA_DK
    wq, wk, wv, wg, wr, wmq = jnp.split(
        w_in, np.cumsum([kd, kd, MIX_DIM, MIX_DIM, 2 * GLA_GATE_RANK]), axis=1)

    def pad_heads(w, dh, dh_pad):
        w = w.reshape(w.shape[0], GLA_HEADS, dh)
        w = jnp.pad(w, ((0, 0), (0, 0), (0, dh_pad - dh)))
        return w.reshape(w.shape[0], GLA_HEADS * dh_pad)

    w_pad = jnp.concatenate([
        pad_heads(wq, GLA_DK, GLA_DK_PAD), pad_heads(wk, GLA_DK, GLA_DK_PAD),
        pad_heads(wv, GLA_DV, GLA_DV_PAD), pad_heads(wg, GLA_DV, GLA_DV_PAD),
        jnp.pad(wr, ((0, 0), (0, GLA_R_PAD - 2 * GLA_GATE_RANK))), wmq], axis=1)
    gu = []
    for di in range(2):
        up = pad_heads(gate_up[di], GLA_DK, GLA_DK_PAD)
        gu.append(jnp.pad(up, ((di * GLA_GATE_RANK, GLA_R_PAD - (di + 1) * GLA_GATE_RANK), (0, 0))))
    gb = [pad_heads(gate_bias[di][None, :], GLA_DK, GLA_DK_PAD) for di in range(2)]
    onorm = pad_heads(out_norm[None, :].repeat(GLA_HEADS, 0).reshape(1, MIX_DIM), GLA_DV, GLA_DV_PAD)
    return w_pad.astype(BF16), [g.astype(BF16) for g in gu], [x.astype(F32) for x in gb], onorm.astype(F32)


def _post_kernel(x_ref, mix_ref, mq_ref, kbd_ref, vbd_ref, wom_ref, woa_ref, g_ref, wrt_ref, brt_ref,
                 x1_ref, h2_ref, rt_ref, base_ref):
    @pl.when((pl.program_id(0) == 0) & (pl.program_id(1) == 0))
    def _():
        base_ref[...] = jnp.zeros_like(base_ref)

    mq = mq_ref[0]
    s = jnp.dot(mq, kbd_ref[0], preferred_element_type=F32) * (HEAD_DIM ** -0.5)
    n_mem = kbd_ref.shape[2] // MEM_HEADS
    ps = []
    for h in range(MEM_HEADS):
        seg = s[:, h * n_mem:(h + 1) * n_mem]
        e = jnp.exp(seg - jnp.max(seg, axis=-1, keepdims=True))
        ps.append((e * (1.0 / jnp.sum(e, axis=-1, keepdims=True))).astype(BF16))
    att = jnp.dot(jnp.concatenate(ps, axis=1), vbd_ref[0], preferred_element_type=F32).astype(BF16)
    x1 = (x_ref[0] + jnp.dot(mix_ref[0], wom_ref[...], preferred_element_type=F32)
          + jnp.dot(att, woa_ref[...], preferred_element_type=F32))
    x1_ref[0] = x1
    ms = jnp.mean(x1 * x1, axis=-1, keepdims=True)
    h2 = x1 * lax.rsqrt(ms + NORM_EPS) * g_ref[...]
    _store_row_tiles(h2_ref, h2)
    h_hi, h_lo = _split_bf16(h2, 2)
    hw = jnp.dot(h_hi, wrt_ref[...], preferred_element_type=F32)
    logits = (hw[:, :V7X_LANES] + hw[:, V7X_LANES:]
              + jnp.dot(h_lo, wrt_ref[:, :V7X_LANES], preferred_element_type=F32) + brt_ref[...])
    lane = lax.broadcasted_iota(jnp.int32, logits.shape, 1).astype(F32)
    neg = -jnp.inf
    big = 1e9
    gl = jnp.where(lane < N_GROUPS, logits, neg)
    gm = jnp.max(gl, axis=-1, keepdims=True)
    g_gate = 1.0 / jnp.sum(jnp.exp(gl - gm), axis=-1, keepdims=True)
    g_idx = jnp.min(jnp.where(gl == gm, lane, big), axis=-1, keepdims=True)
    lo = N_GROUPS + EXPERTS_PER_GROUP * g_idx
    v1 = jnp.where((lane >= lo) & (lane < lo + EXPERTS_PER_GROUP), logits, neg)
    m1 = jnp.max(v1, axis=-1, keepdims=True)
    i1 = jnp.min(jnp.where(v1 == m1, lane, big), axis=-1, keepdims=True)
    v2 = jnp.where(lane == i1, neg, v1)
    m2 = jnp.max(v2, axis=-1, keepdims=True)
    i2 = jnp.min(jnp.where(v2 == m2, lane, big), axis=-1, keepdims=True)
    t = jnp.exp(m2 - m1)
    w1 = g_gate / (1.0 + t)
    w2 = g_gate * t / (1.0 + t)
    e1 = i1 - N_GROUPS
    e2 = i2 - N_GROUPS
    tm = logits.shape[0]
    onehot = jnp.where(lane == e1, 1.0, 0.0) + jnp.where(lane == e2, 1.0, 0.0)
    ri = lax.broadcasted_iota(jnp.int32, (tm, tm), 0)
    ci = lax.broadcasted_iota(jnp.int32, (tm, tm), 1)
    before = jnp.where(ci < ri, 1.0, 0.0).astype(BF16)
    ahead = jnp.dot(before, onehot.astype(BF16), preferred_element_type=F32) + base_ref[...]
    pos1 = jnp.sum(jnp.where(lane == e1, ahead, 0.0), axis=-1, keepdims=True)
    pos2 = jnp.sum(jnp.where(lane == e2, ahead, 0.0), axis=-1, keepdims=True)
    base_ref[...] = base_ref[...] + jnp.sum(onehot, axis=0, keepdims=True)
    rt = jnp.where(lane == 0, e1,
                   jnp.where(lane == 1, e2,
                             jnp.where(lane == 2, w1,
                                       jnp.where(lane == 3, w2,
                                                 jnp.where(lane == 4, pos1, jnp.where(lane == 5, pos2, 0.0))))))
    rt_ref[0] = rt


def _post_mixer(x, mix, u, mq_off, kbd, vbd, w_out, g, w_rt, b_rt, *, tm):
    b, t, d = x.shape
    mixw = mix.shape[2]
    wom = w_out[0]
    woa = w_out[1]
    return pl.pallas_call(
        _post_kernel,
        grid=(b, t // tm),
        in_specs=[pl.BlockSpec((1, tm, d), lambda bi, ti: (bi, ti, 0)),
                  pl.BlockSpec((1, tm, mixw), lambda bi, ti: (bi, ti, 0)),
                  pl.BlockSpec((1, tm, MEM_DIM), lambda bi, ti: (bi, ti, mq_off // MEM_DIM)),
                  pl.BlockSpec((1,) + kbd.shape[1:], lambda bi, ti: (bi, 0, 0)),
                  pl.BlockSpec((1,) + vbd.shape[1:], lambda bi, ti: (bi, 0, 0)),
                  pl.BlockSpec(wom.shape, lambda bi, ti: (0, 0)),
                  pl.BlockSpec(woa.shape, lambda bi, ti: (0, 0)),
                  pl.BlockSpec((1, d), lambda bi, ti: (0, 0)),
                  pl.BlockSpec(w_rt.shape, lambda bi, ti: (0, 0)),
                  pl.BlockSpec((1, V7X_LANES), lambda bi, ti: (0, 0))],
        out_specs=[pl.BlockSpec((1, tm, d), lambda bi, ti: (bi, ti, 0)),
                   pl.BlockSpec((tm * ROW_TILE, V7X_LANES), lambda bi, ti: (bi * (t // tm) + ti, 0)),
                   pl.BlockSpec((1, tm, V7X_LANES), lambda bi, ti: (bi, ti, 0))],
        out_shape=[jax.ShapeDtypeStruct((b, t, d), F32),
                   jax.ShapeDtypeStruct((b * t * ROW_TILE, V7X_LANES), F32),
                   jax.ShapeDtypeStruct((b, t, V7X_LANES), F32)],
        scratch_shapes=[pltpu.VMEM((1, V7X_LANES), F32)],
        compiler_params=_cparams(("arbitrary", "arbitrary")),
        name="post_mixer",
    )(x, mix, u, kbd, vbd, wom, woa, g.reshape(1, d), w_rt, b_rt)


def _memory_kv_blockdiag(mkv):
    k, v = jnp.split(mkv, 2, axis=-1)
    head_of = np.arange(MEM_DIM) // HEAD_DIM
    sel = jnp.asarray(head_of[None, :] == np.arange(MEM_HEADS)[:, None], F32)
    kbd = jnp.einsum('bmd,hd->bdhm', k, sel).reshape(k.shape[0], MEM_DIM, -1)
    vbd = jnp.einsum('bmd,hd->bhmd', v, sel).reshape(v.shape[0], -1, MEM_DIM)
    return kbd.astype(BF16), vbd.astype(BF16)


def _moe_kernel(seg_ref, w1_ref, w3_ref, w2_ref, xs_ref, y_ref, xbuf, ybuf, wb1, wb3, wb2, in_sem, out_sem):
    del xs_ref
    e = pl.program_id(0)
    n_blk = (seg_ref[0, e] + MOE_BLOCK - 1) // MOE_BLOCK
    first_row = seg_ref[1, e]

    def block_rows(blk):
        return _row_slice(first_row + blk * MOE_BLOCK, MOE_BLOCK)

    def fetch(blk, slot):
        return pltpu.make_async_copy(y_ref.at[block_rows(blk)], xbuf.at[slot], in_sem.at[slot])

    def writeback(blk, slot):
        return pltpu.make_async_copy(ybuf.at[slot], y_ref.at[block_rows(blk)], out_sem.at[slot])

    @pl.when(n_blk > 0)
    def _():
        fetch(0, 0).start()
        wb1[...] = w1_ref[0, 0].astype(BF16)
        wb3[...] = w3_ref[0, 0].astype(BF16)
        wb2[...] = w2_ref[0, 0].astype(BF16)

        def per_block(blk, carry):
            slot = blk % 2

            @pl.when(blk + 1 < n_blk)
            def _():
                fetch(blk + 1, 1 - slot).start()

            fetch(blk, slot).wait()

            @pl.when(blk >= 2)
            def _():
                writeback(blk - 2, slot).wait()

            xb = _load_row_tiles(xbuf.at[slot], MOE_BLOCK).astype(BF16)
            a = jnp.dot(xb, wb1[...], preferred_element_type=F32)
            c = jnp.dot(xb, wb3[...], preferred_element_type=F32)
            hid = (a / (1.0 + jnp.exp(-a)) * c).astype(BF16)
            _store_row_tiles(ybuf.at[slot], jnp.dot(hid, wb2[...], preferred_element_type=F32))
            writeback(blk, slot).start()
            return carry

        lax.fori_loop(0, n_blk, per_block, 0)

        @pl.when(n_blk >= 2)
        def _():
            writeback(n_blk - 2, n_blk % 2).wait()

        writeback(n_blk - 1, (n_blk - 1) % 2).wait()


def _moe_ffn(xs, seg, w1, w3, w2, layer):
    d = w1.shape[2]
    ff = w1.shape[3]
    blk_rows = MOE_BLOCK * ROW_TILE
    grid_spec = pltpu.PrefetchScalarGridSpec(
        num_scalar_prefetch=1,
        grid=(N_EXPERTS,),
        in_specs=[pl.BlockSpec((1, 1, d, ff), lambda e, sg: (layer, e, 0, 0)),
                  pl.BlockSpec((1, 1, d, ff), lambda e, sg: (layer, e, 0, 0)),
                  pl.BlockSpec((1, 1, ff, d), lambda e, sg: (layer, e, 0, 0)),
                  pl.BlockSpec(memory_space=pl.ANY)],
        out_specs=pl.BlockSpec(memory_space=pl.ANY),
        scratch_shapes=[pltpu.VMEM((2, blk_rows, V7X_LANES), F32), pltpu.VMEM((2, blk_rows, V7X_LANES), F32),
                        pltpu.VMEM((d, ff), BF16), pltpu.VMEM((d, ff), BF16), pltpu.VMEM((ff, d), BF16),
                        pltpu.SemaphoreType.DMA((2,)), pltpu.SemaphoreType.DMA((2,))],
    )
    return pl.pallas_call(
        _moe_kernel,
        grid_spec=grid_spec,
        out_shape=jax.ShapeDtypeStruct(xs.shape, F32),
        input_output_aliases={4: 0},
        compiler_params=_cparams(("arbitrary",)),
        name="moe_ffn",
    )(seg, w1, w3, w2, xs)


def _row_slice(row, n_rows):
    return pl.ds(pl.multiple_of(row * ROW_TILE, ROW_TILE), n_rows * ROW_TILE)


def _pow2_pieces(n, largest, act):
    k = largest
    while k >= 1:
        shift = k.bit_length()

        @pl.when((n & k) != 0)
        def _(k=k, shift=shift):
            act((n >> shift) << shift, k)
        k //= 2


def _run_copies(runs_ref, make_copy, act):
    def per_expert(e, carry):
        n = runs_ref[0, 0, e]
        local_row = runs_ref[0, 1, e]
        sorted_row = runs_ref[0, 2, e]

        def chunk(i, c):
            act(make_copy(local_row + 8 * i, sorted_row + 8 * i, 8))
            return c

        lax.fori_loop(0, n >> 3, chunk, 0)
        _pow2_pieces(n, 4, lambda off, k: act(make_copy(local_row + off, sorted_row + off, k)))
        return carry

    lax.fori_loop(0, N_EXPERTS, per_expert, 0)


def _dispatch_kernel(seg_ref, lpos_ref, runs_ref, prev_runs_ref, h_ref, xs_ref, local, zbuf, run_sem, sem):
    tm = h_ref.shape[0] // ROW_TILE
    n_rows = xs_ref.shape[0] // ROW_TILE
    zrows = zbuf.shape[0] // ROW_TILE
    step = pl.program_id(0)
    last_step = pl.num_programs(0) - 1
    cur = step % 2

    def place(tok, carry):
        row = h_ref[_row_slice(tok, 1), :]
        for k in range(TOP_K):
            local[cur, _row_slice(lpos_ref[0, 0, TOP_K * tok + k], 1), :] = row
        return carry

    lax.fori_loop(0, tm, place, 0, unroll=8)

    def to_sorted(slot):
        def make(local_row, sorted_row, n):
            return pltpu.make_async_copy(local.at[slot, _row_slice(local_row, n)],
                                         xs_ref.at[_row_slice(sorted_row, n)], run_sem.at[slot])
        return make

    _run_copies(runs_ref, to_sorted(cur), lambda cp: cp.start())

    @pl.when(step > 0)
    def _():
        _run_copies(prev_runs_ref, to_sorted(1 - cur), lambda cp: cp.wait())

    @pl.when(step == last_step)
    def _():
        _run_copies(runs_ref, to_sorted(cur), lambda cp: cp.wait())

    @pl.when(pl.program_id(0) == pl.num_programs(0) - 1)
    def _():
        zbuf[...] = jnp.zeros_like(zbuf)

        def zero_copy(first_row, k):
            return pltpu.make_async_copy(zbuf.at[_row_slice(0, k)], xs_ref.at[_row_slice(first_row, k)], sem)

        def pad_copies(act):
            def per_expert(e, carry):
                cnt = seg_ref[0, e]
                first = seg_ref[1, e] + cnt
                _pow2_pieces((-cnt) & (MOE_BLOCK - 1), zrows, lambda off, k: act(zero_copy(first + off, k)))
                return carry

            lax.fori_loop(0, N_EXPERTS, per_expert, 0)
            used = seg_ref[2, 0]

            def tail(i, carry):
                act(zero_copy(used + i * zrows, zrows))
                return carry

            lax.fori_loop(0, (n_rows - used) // zrows, tail, 0)

        pad_copies(lambda cp: cp.start())
        pad_copies(lambda cp: cp.wait())


def _dispatch(h2_tiles, seg, lpos, runs, n_rows, *, tm):
    n_tok = h2_tiles.shape[0] // ROW_TILE
    nt = n_tok // tm
    grid_spec = pltpu.PrefetchScalarGridSpec(
        num_scalar_prefetch=1,
        grid=(nt,),
        in_specs=[pl.BlockSpec((1, 1, TOP_K * tm), lambda i, c: (i, 0, 0), memory_space=pltpu.SMEM),
                  pl.BlockSpec((1, 3, N_EXPERTS), lambda i, c: (i, 0, 0), memory_space=pltpu.SMEM),
                  pl.BlockSpec((1, 3, N_EXPERTS), lambda i, c: (jnp.maximum(i - 1, 0), 0, 0),
                               memory_space=pltpu.SMEM),
                  pl.BlockSpec((tm * ROW_TILE, V7X_LANES), lambda i, c: (i, 0))],
        out_specs=pl.BlockSpec(memory_space=pl.ANY),
        scratch_shapes=[pltpu.VMEM((2, TOP_K * tm * ROW_TILE, V7X_LANES), F32),
                        pltpu.VMEM((MOE_BLOCK // 2 * ROW_TILE, V7X_LANES), F32),
                        pltpu.SemaphoreType.DMA((2,)),
                        pltpu.SemaphoreType.DMA(())],
    )
    return pl.pallas_call(
        _dispatch_kernel,
        grid_spec=grid_spec,
        out_shape=jax.ShapeDtypeStruct((n_rows * ROW_TILE, V7X_LANES), F32),
        compiler_params=_cparams(("arbitrary",)),
        name="moe_dispatch",
    )(seg, lpos, runs, runs, h2_tiles)


def _combine_kernel(*refs, final):
    if final:
        lpos_ref, gate_ref, runs_ref, next_runs_ref, x1_ref, yb_ref, g_ref, o_ref, local, acc, sem = refs
    else:
        lpos_ref, gate_ref, runs_ref, next_runs_ref, x1_ref, yb_ref, o_ref, local, acc, sem = refs
    tm = x1_ref.shape[0]
    step = pl.program_id(0)
    cur = step % 2

    def from_sorted(slot):
        def make(local_row, sorted_row, n):
            return pltpu.make_async_copy(yb_ref.at[_row_slice(sorted_row, n)],
                                         local.at[slot, _row_slice(local_row, n)], sem.at[slot])
        return make

    @pl.when(step == 0)
    def _():
        _run_copies(runs_ref, from_sorted(cur), lambda cp: cp.start())

    @pl.when(step + 1 < pl.num_programs(0))
    def _():
        _run_copies(next_runs_ref, from_sorted(1 - cur), lambda cp: cp.start())

    _run_copies(runs_ref, from_sorted(cur), lambda cp: cp.wait())

    def mix(tok, carry):
        y = jnp.zeros((ROW_TILE, V7X_LANES), F32)
        for k in range(TOP_K):
            idx = TOP_K * tok + k
            y = y + gate_ref[0, 0, idx] * local[cur, _row_slice(lpos_ref[0, 0, idx], 1), :]
        acc[_row_slice(tok, 1), :] = y
        return carry

    lax.fori_loop(0, tm, mix, 0, unroll=8)
    x2 = x1_ref[...] + _load_row_tiles(acc, tm)
    if final:
        ms = jnp.mean(x2 * x2, axis=-1, keepdims=True)
        x2 = x2 * lax.rsqrt(ms + NORM_EPS) * g_ref[...]
    o_ref[...] = x2


def _combine(x1, yb, lpos, gates, runs, *, tm, final_g=None):
    n_tok, d = x1.shape
    nt = n_tok // tm
    final = final_g is not None
    in_specs = [pl.BlockSpec((1, 1, TOP_K * tm), lambda i: (i, 0, 0), memory_space=pltpu.SMEM),
                pl.BlockSpec((1, 1, TOP_K * tm), lambda i: (i, 0, 0), memory_space=pltpu.SMEM),
                pl.BlockSpec((1, 3, N_EXPERTS), lambda i: (i, 0, 0), memory_space=pltpu.SMEM),
                pl.BlockSpec((1, 3, N_EXPERTS), lambda i: (jnp.minimum(i + 1, nt - 1), 0, 0),
                             memory_space=pltpu.SMEM),
                pl.BlockSpec((tm, d), lambda i: (i, 0)),
                pl.BlockSpec(memory_space=pl.ANY)]
    args = [lpos, gates, runs, runs, x1, yb]
    if final:
        in_specs.append(pl.BlockSpec((1, d), lambda i: (0, 0)))
        args.append(final_g.reshape(1, d))
    return pl.pallas_call(
        functools.partial(_combine_kernel, final=final),
        grid=(nt,),
        in_specs=in_specs,
        out_specs=pl.BlockSpec((tm, d), lambda i: (i, 0)),
        out_shape=jax.ShapeDtypeStruct((n_tok, d), F32),
        scratch_shapes=[pltpu.VMEM((2, TOP_K * tm * ROW_TILE, V7X_LANES), F32),
                        pltpu.VMEM((tm * ROW_TILE, V7X_LANES), F32),
                        pltpu.SemaphoreType.DMA((2,))],
        compiler_params=_cparams(("arbitrary",)),
        name="moe_combine_final" if final else "moe_combine",
    )(*args)


def _moe_plan(rt, n_tok, tm):
    nt = n_tok // tm
    expert = rt[:, 0:TOP_K].astype(jnp.int32)
    gate = rt[:, TOP_K:2 * TOP_K]
    pos = rt[:, 2 * TOP_K:3 * TOP_K].astype(jnp.int32)
    is_e = (expert[:, :, None] == jnp.arange(N_EXPERTS, dtype=jnp.int32)).reshape(nt, tm, TOP_K, N_EXPERTS)
    cnt_tile = jnp.sum(is_e, axis=(1, 2), dtype=jnp.int32)
    seen_before = jnp.cumsum(cnt_tile, axis=0) - cnt_tile
    local_start = jnp.cumsum(cnt_tile, axis=1) - cnt_tile
    counts = jnp.sum(cnt_tile, axis=0)
    shift = local_start - seen_before
    lpos = pos.reshape(nt, tm, TOP_K) + jnp.sum(jnp.where(is_e, shift[:, None, None, :], 0), axis=-1)
    nblk = (counts + MOE_BLOCK - 1) // MOE_BLOCK
    bend = jnp.cumsum(nblk)
    seg_start = (bend - nblk) * MOE_BLOCK
    sorted_start = seg_start[None, :] + seen_before
    runs = jnp.stack([cnt_tile, local_start, sorted_start], axis=1)
    seg = jnp.stack([counts, seg_start, jnp.full((N_EXPERTS,), bend[-1] * MOE_BLOCK, jnp.int32)])
    n_rows = n_tok * TOP_K + N_EXPERTS * MOE_BLOCK
    return dict(seg=seg.astype(jnp.int32), n_rows=n_rows,
                lpos=lpos.reshape(nt, 1, TOP_K * tm), gates=gate.reshape(nt, 1, TOP_K * tm), runs=runs)


def _moe_layer(x1, h2_tiles, rt, w1, w3, w2, layer, *, final_g=None):
    n_tok = x1.shape[0]
    tm = 1024
    plan = _moe_plan(rt, n_tok, tm)
    xs = _dispatch(h2_tiles, plan["seg"], plan["lpos"], plan["runs"], plan["n_rows"], tm=tm)
    yb = _moe_ffn(xs, plan["seg"], w1, w3, w2, layer)
    return _combine(x1, yb, plan["lpos"], plan["gates"], plan["runs"], tm=tm, final_g=final_g)


def kernel(x, mem, mem_norm, final_norm, norm_mix, norm_ffn, w_mem_kv, w_out, na_w_in, na_rpb, gla_w_in,
           gla_gate_up, gla_gate_bias, gla_out_norm, moe_w_group, moe_b_group, moe_w_router, moe_b_router,
           moe_w1, moe_w3, moe_w2):
    b, t, d = x.shape
    n = b * t
    depth = norm_mix.shape[0]
    n_mem = mem.shape[1]
    xf = x.reshape(n, d)
    for i in range(depth):
        j = i // 2
        mkv = _norm_matmul(mem.reshape(b * n_mem, d), mem_norm, w_mem_kv[i].astype(BF16),
                           tm=256, out_dtype=F32, name="mem_kv_proj").reshape(b, n_mem, 2 * MEM_DIM)
        kbd, vbd = _memory_kv_blockdiag(mkv)
        if i % 2 == 0:
            u = _norm_matmul(xf, norm_mix[i], na_w_in[j].astype(BF16), tm=512, out_dtype=BF16,
                             name="na_in_proj").reshape(b, t, -1)
            mix = _na_attention(u, na_rpb[j])
            mq_off = NA_MQ_OFF
            w_o = w_out[i].astype(BF16)
            w_o = (w_o[:MIX_DIM], w_o[MIX_DIM:])
        else:
            w_pad, gu, gb, onorm = _gla_weights(gla_w_in[j], gla_gate_up[j], gla_gate_bias[j], gla_out_norm[j])
            u = _norm_matmul(xf, norm_mix[i], w_pad, tm=512, out_dtype=BF16, name="gla_in_proj").reshape(b, t, -1)
            ofwd = _gla_direction(u, gu[0], gb[0], reverse=False, tb=512)
            mix = _gla_direction(u, gu[1], gb[1], reverse=True, tb=512, ofwd=ofwd, onorm=onorm)
            mq_off = GLA_MQ_OFF
            w_mix = w_out[i][:MIX_DIM].reshape(GLA_HEADS, GLA_DV, d)
            w_mix = jnp.pad(w_mix, ((0, 0), (0, GLA_DV_PAD - GLA_DV), (0, 0))).reshape(GLA_MIX_PAD, d)
            w_o = (w_mix.astype(BF16), w_out[i][MIX_DIM:].astype(BF16))
        w_rt = jnp.pad(jnp.concatenate([moe_w_group[i], moe_w_router[i]], axis=1),
                       ((0, 0), (0, V7X_LANES - N_GROUPS - N_EXPERTS))).astype(F32)
        w_rt = jnp.concatenate(_split_bf16(w_rt, 2), axis=1)
        b_rt = jnp.pad(jnp.concatenate([moe_b_group[i], moe_b_router[i]]),
                       (0, V7X_LANES - N_GROUPS - N_EXPERTS)).reshape(1, V7X_LANES)
        x1, h2_tiles, rt = _post_mixer(xf.reshape(b, t, d), mix, u, mq_off, kbd, vbd, w_o, norm_ffn[i],
                                       w_rt, b_rt.astype(F32), tm=256)
        xf = _moe_layer(x1.reshape(n, d), h2_tiles, rt.reshape(n, V7X_LANES),
                        moe_w1, moe_w3, moe_w2, i,
                        final_g=final_norm if i == depth - 1 else None)
    return xf.reshape(b, t, d)
```

```python
import functools

import numpy as np
import jax
import jax.numpy as jnp
from jax import lax
from jax.experimental import pallas as pl
from jax.experimental.pallas import tpu as pltpu

F32 = jnp.float32
BF16 = jnp.bfloat16

D_MODEL = 1024
GRID_W = 64
HEAD_DIM = 64
NORM_EPS = 1e-6
MEM_HEADS = 4
MEM_DIM = MEM_HEADS * HEAD_DIM
MIX_DIM = D_MODEL - MEM_DIM
NA_HEADS = MIX_DIM // HEAD_DIM
NA_WIN_H = 8
NA_WIN_W = 16
GLA_HEADS = 4
GLA_DK = MIX_DIM // 2 // GLA_HEADS
GLA_DV = MIX_DIM // GLA_HEADS
GLA_GATE_RANK = 16
GLA_TAU = 16.0
GLA_CHUNK = 64
N_GROUPS = 4
EXPERTS_PER_GROUP = 8
N_EXPERTS = N_GROUPS * EXPERTS_PER_GROUP
TOP_K = 2
EXPERT_FF = 512
MOE_BLOCK = 256

V7X_LANES = 128
V7X_MXU_DIM = 256
V7X_VMEM_LIMIT_BYTES = 56 * 1024 * 1024

GLA_DK_PAD = 128
GLA_DV_PAD = 256
GLA_Q_OFF = 0
GLA_K_OFF = GLA_Q_OFF + GLA_HEADS * GLA_DK_PAD
GLA_V_OFF = GLA_K_OFF + GLA_HEADS * GLA_DK_PAD
GLA_G_OFF = GLA_V_OFF + GLA_HEADS * GLA_DV_PAD
GLA_R_OFF = GLA_G_OFF + GLA_HEADS * GLA_DV_PAD
GLA_R_PAD = 256
GLA_MQ_OFF = GLA_R_OFF + GLA_R_PAD
GLA_IN_PAD = GLA_MQ_OFF + MEM_DIM
GLA_MIX_PAD = GLA_HEADS * GLA_DV_PAD
NA_MQ_OFF = 3 * MIX_DIM
NA_ROWS_PER_STEP = 4
POST_TILE = 512
POST_SUBTILES = 2


def _cparams(semantics):
    return pltpu.CompilerParams(dimension_semantics=semantics, vmem_limit_bytes=V7X_VMEM_LIMIT_BYTES)


def _split_bf16(x, n_pieces):
    pieces = []
    rest = x
    for _ in range(n_pieces - 1):
        c = rest * 65537.0
        hi = c - (c - rest)
        pieces.append(hi.astype(BF16))
        rest = rest - hi
    pieces.append(rest.astype(BF16))
    return pieces


ROW_TILE = D_MODEL // V7X_LANES


def _store_row_tiles(ref, val):
    rows = val.shape[0]
    for s in range(ROW_TILE):
        ref[pl.ds(s, rows, stride=ROW_TILE), :] = val[:, s * V7X_LANES:(s + 1) * V7X_LANES]


def _load_row_tiles(ref, rows):
    return jnp.concatenate([ref[pl.ds(s, rows, stride=ROW_TILE), :] for s in range(ROW_TILE)], axis=1)


def _norm_matmul_kernel(x_ref, g_ref, w_ref, o_ref, *, col_chunk):
    x = x_ref[...]
    ms = jnp.mean(x * x, axis=-1, keepdims=True)
    y = (x * lax.rsqrt(ms + NORM_EPS) * g_ref[...]).astype(BF16)
    n_out = o_ref.shape[1]
    for c in range(0, n_out, col_chunk):
        o_ref[:, c:c + col_chunk] = jnp.dot(
            y, w_ref[:, c:c + col_chunk], preferred_element_type=F32).astype(o_ref.dtype)


def _norm_matmul(x, g, w, *, tm, out_dtype, name):
    n, d = x.shape
    n_out = w.shape[1]
    col_chunk = 512 if n_out % 512 == 0 else n_out
    return pl.pallas_call(
        functools.partial(_norm_matmul_kernel, col_chunk=col_chunk),
        grid=(n // tm,),
        in_specs=[pl.BlockSpec((tm, d), lambda i: (i, 0)),
                  pl.BlockSpec((1, d), lambda i: (0, 0)),
                  pl.BlockSpec((d, n_out), lambda i: (0, 0))],
        out_specs=pl.BlockSpec((tm, n_out), lambda i: (i, 0)),
        out_shape=jax.ShapeDtypeStruct((n, n_out), out_dtype),
        compiler_params=_cparams(("parallel",)),
        name=name,
    )(x, g.reshape(1, d), w)


def _na_kernel(q_ref, k_ref, v_ref, *rest):
    *bias_refs, o_ref = rest
    rows = k_ref.shape[1] // GRID_W
    n_keys = NA_WIN_H * GRID_W
    heads_per_slab = V7X_MXU_DIM // HEAD_DIM
    scale = HEAD_DIM ** -0.5
    lane_head = lax.broadcasted_iota(jnp.int32, (GRID_W, V7X_MXU_DIM), 1) // HEAD_DIM
    slabs = [slice(s * V7X_MXU_DIM, (s + 1) * V7X_MXU_DIM) for s in range(MIX_DIM // V7X_MXU_DIM)]

    def window_start(rr):
        r = pl.program_id(1) * len(bias_refs) + rr
        rs = jnp.clip(r - NA_WIN_H // 2, 0, rows - NA_WIN_H)
        return pl.multiple_of(rs * GRID_W, GRID_W)

    def scores(rr, cs):
        qq = q_ref[0, rr * GRID_W:(rr + 1) * GRID_W, cs] * scale
        kw = k_ref[0, pl.ds(window_start(rr), n_keys), cs]
        lhs = jnp.concatenate(
            [jnp.where(lane_head == i, qq, jnp.zeros_like(qq)) for i in range(heads_per_slab)], axis=0)
        sc = lax.dot_general(lhs, kw, (((1,), (1,)), ((), ())), preferred_element_type=F32)
        return sc + bias_refs[rr][0, cs, :]

    units = [(rr, cs) for rr in range(len(bias_refs)) for cs in slabs]
    nxt = scores(*units[0])
    for ui, (rr, cs) in enumerate(units):
        sc = nxt
        if ui + 1 < len(units):
            nxt = scores(*units[ui + 1])
        vw = v_ref[0, pl.ds(window_start(rr), n_keys), cs]
        m = jnp.max(sc, axis=-1, keepdims=True)
        p = jnp.exp(sc - m)
        l = jnp.sum(p, axis=-1, keepdims=True)
        o = jnp.dot(p.astype(BF16), vw, preferred_element_type=F32)
        o = o * (1.0 / l)
        acc = jnp.zeros((GRID_W, V7X_MXU_DIM), F32)
        for i in range(heads_per_slab):
            acc = acc + jnp.where(lane_head == i, o[i * GRID_W:(i + 1) * GRID_W], 0.0)
        o_ref[0, rr * GRID_W:(rr + 1) * GRID_W, cs] = acc.astype(o_ref.dtype)


def _na_bias_table(rpb):
    qc = np.arange(GRID_W)[:, None]
    kc = np.arange(GRID_W)[None, :]
    cstart = np.clip(qc - NA_WIN_W // 2, 0, GRID_W - NA_WIN_W)
    col_in = (kc >= cstart) & (kc < cstart + NA_WIN_W)
    dcol = np.clip(kc - qc, 1 - NA_WIN_W, NA_WIN_W - 1) + NA_WIN_W - 1
    pick = jnp.asarray(dcol[None] == np.arange(2 * NA_WIN_W - 1)[:, None, None], F32)
    rows = jnp.stack([rpb.astype(F32)[:, o:o + NA_WIN_H] for o in range(NA_WIN_H)])
    tbl = jnp.einsum('ohjd,dqk->ohqjk', rows, pick, precision=lax.Precision.HIGHEST)
    tbl = jnp.where(col_in[None, None, :, None, :], tbl, -jnp.inf)
    return tbl.reshape(NA_WIN_H, NA_HEADS * GRID_W, NA_WIN_H * GRID_W)


def _na_attention(u, rpb):
    b, t, _ = u.shape
    rows = t // GRID_W
    bias = _na_bias_table(rpb)

    def bias_spec(rr):
        def bias_idx(bi, g):
            r = g * NA_ROWS_PER_STEP + rr
            return (jnp.clip(r - NA_WIN_H // 2, 0, rows - NA_WIN_H) - r + NA_WIN_H - 1, 0, 0)
        return pl.BlockSpec((1, NA_HEADS * GRID_W, NA_WIN_H * GRID_W), bias_idx)

    qt = NA_ROWS_PER_STEP * GRID_W
    return pl.pallas_call(
        _na_kernel,
        grid=(b, rows // NA_ROWS_PER_STEP),
        in_specs=[pl.BlockSpec((1, qt, MIX_DIM), lambda bi, g: (bi, g, 0)),
                  pl.BlockSpec((1, t, MIX_DIM), lambda bi, g: (bi, 0, 1)),
                  pl.BlockSpec((1, t, MIX_DIM), lambda bi, g: (bi, 0, 2))]
        + [bias_spec(rr) for rr in range(NA_ROWS_PER_STEP)],
        out_specs=pl.BlockSpec((1, qt, MIX_DIM), lambda bi, g: (bi, g, 0)),
        out_shape=jax.ShapeDtypeStruct((b, t, MIX_DIM), BF16),
        compiler_params=_cparams(("parallel", "arbitrary")),
        name="na_attention",
    )(u, u, u, *([bias] * NA_ROWS_PER_STEP))


def _gla_kernel(*refs, reverse, final):
    if final:
        (q_ref, k_ref, v_ref, r_ref, gu_ref, gb_ref, ofwd_ref, g_ref, onorm_ref, o_ref, st_ref) = refs
    else:
        (q_ref, k_ref, v_ref, r_ref, gu_ref, gb_ref, o_ref, st_ref) = refs
    c = GLA_CHUNK

    @pl.when(pl.program_id(1) == 0)
    def _():
        st_ref[...] = jnp.zeros_like(st_ref)

    z = jnp.dot(r_ref[0], gu_ref[...], preferred_element_type=F32) + gb_ref[...]
    la = (jnp.minimum(z, 0.0) - jnp.log(1.0 + jnp.exp(-jnp.abs(z)))) * (1.0 / GLA_TAU)
    ri = lax.broadcasted_iota(jnp.int32, (c, c), 0)
    ci = lax.broadcasted_iota(jnp.int32, (c, c), 1)
    tri = (ci >= ri) if reverse else (ci <= ri)
    trib = jnp.where(tri, 1.0, 0.0).astype(BF16)
    la3 = jnp.concatenate(_split_bf16(la, 3), axis=1)
    gw = la.shape[1]
    mid = c // 2 if reverse else c // 2 - 1
    last = 0 if reverse else c - 1
    scale = GLA_DK ** -0.5
    n_chunks = q_ref.shape[1] // c
    order = range(n_chunks - 1, -1, -1) if reverse else range(n_chunks)
    state = [st_ref[h] for h in range(GLA_HEADS)]
    heads = range(GLA_HEADS)
    hsl = [slice(h * GLA_DK_PAD, (h + 1) * GLA_DK_PAD) for h in heads]
    vsl = [slice(h * GLA_DV_PAD, (h + 1) * GLA_DV_PAD) for h in heads]
    nt_dims = (((1,), (1,)), ((), ()))
    tn_dims = (((0,), (0,)), ((), ()))

    def prepare(ch):
        sl = slice(ch * c, (ch + 1) * c)
        b3 = jnp.dot(trib, la3[sl], preferred_element_type=F32)
        bcum = b3[:, :gw] + b3[:, gw:2 * gw] + b3[:, 2 * gw:]
        b_mid = bcum[mid:mid + 1]
        b_last = bcum[last:last + 1]
        qc = q_ref[0, sl, :].astype(F32) * scale
        kc = k_ref[0, sl, :].astype(F32)
        return dict(sl=sl, vc=v_ref[0, sl, :],
                    qe=(qc * jnp.exp(bcum - b_mid)).astype(BF16), ke=(kc * jnp.exp(b_mid - bcum)).astype(BF16),
                    qs=(qc * jnp.exp(bcum)).astype(BF16), ks=(kc * jnp.exp(b_last - bcum)).astype(BF16),
                    dec=jnp.exp(b_last))

    order = list(order)
    nxt = prepare(order[0])
    for pos_in_step, ch in enumerate(order):
        cur = nxt
        if pos_in_step + 1 < len(order):
            nxt = prepare(order[pos_in_step + 1])
        sl, vc, dec = cur["sl"], cur["vc"], cur["dec"]
        scores = [lax.dot_general(cur["qe"][:, hsl[h]], cur["ke"][:, hsl[h]], nt_dims, preferred_element_type=F32)
                  for h in heads]
        kv_t = [lax.dot_general(vc[:, vsl[h]], cur["ks"][:, hsl[h]], tn_dims, preferred_element_type=F32)
                for h in heads]
        o_inter = [lax.dot_general(cur["qs"][:, hsl[h]], state[h].astype(BF16), nt_dims, preferred_element_type=F32)
                   for h in heads]
        masked = [jnp.where(tri, scores[h], 0.0).astype(BF16) for h in heads]
        o_intra = [jnp.dot(masked[h], vc[:, vsl[h]], preferred_element_type=F32) for h in heads]
        for h in heads:
            hs, vs = hsl[h], vsl[h]
            o = o_intra[h] + o_inter[h]
            state[h] = state[h] * dec[:, hs] + kv_t[h]
            if final:
                tot = ofwd_ref[0, sl, vs] + o
                ms = jnp.sum(tot * tot, axis=-1, keepdims=True) * (1.0 / GLA_DV)
                y = tot * lax.rsqrt(ms + NORM_EPS) * onorm_ref[:, vs]
                g = g_ref[0, sl, vs].astype(F32)
                o_ref[0, sl, vs] = (y * (g / (1.0 + jnp.exp(-g)))).astype(o_ref.dtype)
            else:
                o_ref[0, sl, vs] = o
    for h in range(GLA_HEADS):
        st_ref[h] = state[h]


def _gla_direction(u, gu, gb, *, reverse, tb, ofwd=None, onorm=None):
    b, t, _ = u.shape
    nt = t // tb
    final = ofwd is not None
    tix = (lambda ti: nt - 1 - ti) if reverse else (lambda ti: ti)
    qw = GLA_HEADS * GLA_DK_PAD
    vw = GLA_HEADS * GLA_DV_PAD
    in_specs = [pl.BlockSpec((1, tb, qw), lambda bi, ti: (bi, tix(ti), GLA_Q_OFF // qw)),
                pl.BlockSpec((1, tb, qw), lambda bi, ti: (bi, tix(ti), GLA_K_OFF // qw)),
                pl.BlockSpec((1, tb, vw), lambda bi, ti: (bi, tix(ti), GLA_V_OFF // vw)),
                pl.BlockSpec((1, tb, GLA_R_PAD), lambda bi, ti: (bi, tix(ti), GLA_R_OFF // GLA_R_PAD)),
                pl.BlockSpec((GLA_R_PAD, qw), lambda bi, ti: (0, 0)),
                pl.BlockSpec((1, qw), lambda bi, ti: (0, 0))]
    args = [u, u, u, u, gu, gb]
    if final:
        in_specs += [pl.BlockSpec((1, tb, vw), lambda bi, ti: (bi, tix(ti), 0)),
                     pl.BlockSpec((1, tb, vw), lambda bi, ti: (bi, tix(ti), GLA_G_OFF // vw)),
                     pl.BlockSpec((1, vw), lambda bi, ti: (0, 0))]
        args += [ofwd, u, onorm]
    return pl.pallas_call(
        functools.partial(_gla_kernel, reverse=reverse, final=final),
        grid=(b, nt),
        in_specs=in_specs,
        out_specs=pl.BlockSpec((1, tb, vw), lambda bi, ti: (bi, tix(ti), 0)),
        out_shape=jax.ShapeDtypeStruct((b, t, vw), BF16 if final else F32),
        scratch_shapes=[pltpu.VMEM((GLA_HEADS, GLA_DV_PAD, GLA_DK_PAD), F32)],
        compiler_params=_cparams(("parallel", "arbitrary")),
        name="gla_bwd_final" if final else "gla_fwd",
    )(*args)


def _gla_weights(w_in, gate_up, gate_bias, out_norm):
    d = w_in.shape[0]
    kd = GLA_HEADS * GLA_DK
    wq, wk, wv, wg, wr, wmq = jnp.split(
        w_in, np.cumsum([kd, kd, MIX_DIM, MIX_DIM, 2 * GLA_GATE_RANK]), axis=1)

    def pad_heads(w, dh, dh_pad):
        w = w.reshape(w.shape[0], GLA_HEADS, dh)
        w = jnp.pad(w, ((0, 0), (0, 0), (0, dh_pad - dh)))
        return w.reshape(w.shape[0], GLA_HEADS * dh_pad)

    w_pad = jnp.concatenate([
        pad_heads(wq, GLA_DK, GLA_DK_PAD), pad_heads(wk, GLA_DK, GLA_DK_PAD),
        pad_heads(wv, GLA_DV, GLA_DV_PAD), pad_heads(wg, GLA_DV, GLA_DV_PAD),
        jnp.pad(wr, ((0, 0), (0, GLA_R_PAD - 2 * GLA_GATE_RANK))), wmq], axis=1)
    gu = []
    for di in range(2):
        up = pad_heads(gate_up[di], GLA_DK, GLA_DK_PAD)
        gu.append(jnp.pad(up, ((di * GLA_GATE_RANK, GLA_R_PAD - (di + 1) * GLA_GATE_RANK), (0, 0))))
    gb = [pad_heads(gate_bias[di][None, :], GLA_DK, GLA_DK_PAD) for di in range(2)]
    onorm = pad_heads(out_norm[None, :].repeat(GLA_HEADS, 0).reshape(1, MIX_DIM), GLA_DV, GLA_DV_PAD)
    return w_pad.astype(BF16), [g.astype(BF16) for g in gu], [x.astype(F32) for x in gb], onorm.astype(F32)


def _post_kernel(x_ref, mix_ref, mq_ref, kbd_ref, vbd_ref, wom_ref, woa_ref, g_ref, wrt_ref, brt_ref,
                 x1_ref, h2_ref, rt_ref, base_ref):
    @pl.when((pl.program_id(0) == 0) & (pl.program_id(1) == 0))
    def _():
        base_ref[...] = jnp.zeros_like(base_ref)

    tm = x_ref.shape[1]
    hm = tm // POST_SUBTILES
    groups = [slice(i * hm, (i + 1) * hm) for i in range(POST_SUBTILES)]
    n_mem = kbd_ref.shape[2] // MEM_HEADS
    lane = lax.broadcasted_iota(jnp.int32, (hm, V7X_LANES), 1).astype(F32)
    neg = -jnp.inf
    big = 1e9

    s = [jnp.dot(mq_ref[0, g, :], kbd_ref[0], preferred_element_type=F32) * (HEAD_DIM ** -0.5)
         for g in groups]
    mixed = [jnp.dot(mix_ref[0, g, :], wom_ref[...], preferred_element_type=F32) for g in groups]
    probs = []
    for sg in s:
        ps = []
        for h in range(MEM_HEADS):
            seg = sg[:, h * n_mem:(h + 1) * n_mem]
            e = jnp.exp(seg - jnp.max(seg, axis=-1, keepdims=True))
            ps.append((e * (1.0 / jnp.sum(e, axis=-1, keepdims=True))).astype(BF16))
        probs.append(jnp.concatenate(ps, axis=1))
    att = [jnp.dot(p, vbd_ref[0], preferred_element_type=F32).astype(BF16) for p in probs]
    logits = []
    for i, g in enumerate(groups):
        x1 = x_ref[0, g, :] + mixed[i] + jnp.dot(att[i], woa_ref[...], preferred_element_type=F32)
        x1_ref[0, g, :] = x1
        ms = jnp.mean(x1 * x1, axis=-1, keepdims=True)
        h2 = x1 * lax.rsqrt(ms + NORM_EPS) * g_ref[...]
        _store_row_tiles(h2_ref.at[pl.ds(i * hm * ROW_TILE, hm * ROW_TILE)], h2)
        h_hi, h_lo = _split_bf16(h2, 2)
        hw = jnp.dot(h_hi, wrt_ref[...], preferred_element_type=F32)
        logits.append(hw[:, :V7X_LANES] + hw[:, V7X_LANES:]
                      + jnp.dot(h_lo, wrt_ref[:, :V7X_LANES], preferred_element_type=F32) + brt_ref[...])
    routed = []
    for lg in logits:
        gl = jnp.where(lane < N_GROUPS, lg, neg)
        gm = jnp.max(gl, axis=-1, keepdims=True)
        g_gate = 1.0 / jnp.sum(jnp.exp(gl - gm), axis=-1, keepdims=True)
        g_idx = jnp.min(jnp.where(gl == gm, lane, big), axis=-1, keepdims=True)
        lo = N_GROUPS + EXPERTS_PER_GROUP * g_idx
        v1 = jnp.where((lane >= lo) & (lane < lo + EXPERTS_PER_GROUP), lg, neg)
        m1 = jnp.max(v1, axis=-1, keepdims=True)
        i1 = jnp.min(jnp.where(v1 == m1, lane, big), axis=-1, keepdims=True)
        v2 = jnp.where(lane == i1, neg, v1)
        m2 = jnp.max(v2, axis=-1, keepdims=True)
        i2 = jnp.min(jnp.where(v2 == m2, lane, big), axis=-1, keepdims=True)
        t = jnp.exp(m2 - m1)
        routed.append((i1 - N_GROUPS, i2 - N_GROUPS, g_gate / (1.0 + t), g_gate * t / (1.0 + t)))
    ri = lax.broadcasted_iota(jnp.int32, (hm, hm), 0)
    ci = lax.broadcasted_iota(jnp.int32, (hm, hm), 1)
    before = jnp.where(ci < ri, 1.0, 0.0).astype(BF16)
    onehots = [jnp.where(lane == e1, 1.0, 0.0) + jnp.where(lane == e2, 1.0, 0.0)
               for e1, e2, _, _ in routed]
    earlier = [jnp.dot(before, oh.astype(BF16), preferred_element_type=F32) for oh in onehots]
    for i, g in enumerate(groups):
        e1, e2, w1, w2 = routed[i]
        ahead = earlier[i] + base_ref[...]
        pos1 = jnp.sum(jnp.where(lane == e1, ahead, 0.0), axis=-1, keepdims=True)
        pos2 = jnp.sum(jnp.where(lane == e2, ahead, 0.0), axis=-1, keepdims=True)
        base_ref[...] = base_ref[...] + jnp.sum(onehots[i], axis=0, keepdims=True)
        rt_ref[0, g, :] = jnp.where(
            lane == 0, e1, jnp.where(lane == 1, e2, jnp.where(lane == 2, w1, jnp.where(
                lane == 3, w2, jnp.where(lane == 4, pos1, jnp.where(lane == 5, pos2, 0.0))))))


def _post_mixer(x, mix, u, mq_off, kbd, vbd, w_out, g, w_rt, b_rt, *, tm):
    b, t, d = x.shape
    mixw = mix.shape[2]
    wom = w_out[0]
    woa = w_out[1]
    return pl.pallas_call(
        _post_kernel,
        grid=(b, t // tm),
        in_specs=[pl.BlockSpec((1, tm, d), lambda bi, ti: (bi, ti, 0)),
                  pl.BlockSpec((1, tm, mixw), lambda bi, ti: (bi, ti, 0)),
                  pl.BlockSpec((1, tm, MEM_DIM), lambda bi, ti: (bi, ti, mq_off // MEM_DIM)),
                  pl.BlockSpec((1,) + kbd.shape[1:], lambda bi, ti: (bi, 0, 0)),
                  pl.BlockSpec((1,) + vbd.shape[1:], lambda bi, ti: (bi, 0, 0)),
                  pl.BlockSpec(wom.shape, lambda bi, ti: (0, 0)),
                  pl.BlockSpec(woa.shape, lambda bi, ti: (0, 0)),
                  pl.BlockSpec((1, d), lambda bi, ti: (0, 0)),
                  pl.BlockSpec(w_rt.shape, lambda bi, ti: (0, 0)),
                  pl.BlockSpec((1, V7X_LANES), lambda bi, ti: (0, 0))],
        out_specs=[pl.BlockSpec((1, tm, d), lambda bi, ti: (bi, ti, 0)),
                   pl.BlockSpec((tm * ROW_TILE, V7X_LANES), lambda bi, ti: (bi * (t // tm) + ti, 0)),
                   pl.BlockSpec((1, tm, V7X_LANES), lambda bi, ti: (bi, ti, 0))],
        out_shape=[jax.ShapeDtypeStruct((b, t, d), F32),
                   jax.ShapeDtypeStruct((b * t * ROW_TILE, V7X_LANES), F32),
                   jax.ShapeDtypeStruct((b, t, V7X_LANES), F32)],
        scratch_shapes=[pltpu.VMEM((1, V7X_LANES), F32)],
        compiler_params=_cparams(("arbitrary", "arbitrary")),
        name="post_mixer",
    )(x, mix, u, kbd, vbd, wom, woa, g.reshape(1, d), w_rt, b_rt)


def _memory_kv_blockdiag(mkv):
    k, v = jnp.split(mkv, 2, axis=-1)
    head_of = np.arange(MEM_DIM) // HEAD_DIM
    sel = jnp.asarray(head_of[None, :] == np.arange(MEM_HEADS)[:, None], F32)
    kbd = jnp.einsum('bmd,hd->bdhm', k, sel).reshape(k.shape[0], MEM_DIM, -1)
    vbd = jnp.einsum('bmd,hd->bhmd', v, sel).reshape(v.shape[0], -1, MEM_DIM)
    return kbd.astype(BF16), vbd.astype(BF16)


def _moe_kernel(bexp_ref, brow_ref, bvalid_ref, bfirst_ref, xs_ref, w1_ref, w3_ref, w2_ref, y_ref, wb1, wb3, wb2):
    del bexp_ref, brow_ref
    blk = pl.program_id(0)

    @pl.when(bvalid_ref[blk] == 0)
    def _():
        y_ref[...] = jnp.zeros_like(y_ref)

    @pl.when(bvalid_ref[blk] != 0)
    def _():
        @pl.when(bfirst_ref[blk] != 0)
        def _():
            wb1[...] = w1_ref[0, 0].astype(BF16)
            wb3[...] = w3_ref[0, 0].astype(BF16)
            wb2[...] = w2_ref[0, 0].astype(BF16)

        xb = _load_row_tiles(xs_ref, MOE_BLOCK).astype(BF16)
        half = wb1.shape[1] // 2
        up = [(jnp.dot(xb, wb1[:, hf * half:(hf + 1) * half], preferred_element_type=F32),
               jnp.dot(xb, wb3[:, hf * half:(hf + 1) * half], preferred_element_type=F32)) for hf in range(2)]
        y = None
        for hf, (a, c) in enumerate(up):
            hid = (a / (1.0 + jnp.exp(-a)) * c).astype(BF16)
            part = jnp.dot(hid, wb2[hf * half:(hf + 1) * half, :], preferred_element_type=F32)
            y = part if y is None else y + part
        _store_row_tiles(y_ref, y)


def _moe_ffn(xs, block_exp, block_row, block_valid, block_first, w1, w3, w2, layer):
    n_steps = block_exp.shape[0]
    d = w1.shape[2]
    ff = w1.shape[3]
    blk_rows = MOE_BLOCK * ROW_TILE
    grid_spec = pltpu.PrefetchScalarGridSpec(
        num_scalar_prefetch=4,
        grid=(n_steps,),
        in_specs=[pl.BlockSpec((blk_rows, V7X_LANES), lambda i, be, br, bv, bf: (br[i], 0)),
                  pl.BlockSpec((1, 1, d, ff), lambda i, be, br, bv, bf: (layer, be[i], 0, 0)),
                  pl.BlockSpec((1, 1, d, ff), lambda i, be, br, bv, bf: (layer, be[i], 0, 0)),
                  pl.BlockSpec((1, 1, ff, d), lambda i, be, br, bv, bf: (layer, be[i], 0, 0))],
        out_specs=pl.BlockSpec((blk_rows, V7X_LANES), lambda i, be, br, bv, bf: (i, 0)),
        scratch_shapes=[pltpu.VMEM((d, ff), BF16), pltpu.VMEM((d, ff), BF16), pltpu.VMEM((ff, d), BF16)],
    )
    return pl.pallas_call(
        _moe_kernel,
        grid_spec=grid_spec,
        out_shape=jax.ShapeDtypeStruct(xs.shape, F32),
        compiler_params=_cparams(("arbitrary",)),
        name="moe_ffn",
    )(block_exp, block_row, block_valid, block_first, xs, w1, w3, w2)


def _row_slice(row, n_rows):
    return pl.ds(pl.multiple_of(row * ROW_TILE, ROW_TILE), n_rows * ROW_TILE)


def _pow2_pieces(n, largest, act):
    k = largest
    while k >= 1:
        shift = k.bit_length()

        @pl.when((n & k) != 0)
        def _(k=k, shift=shift):
            act((n >> shift) << shift, k)
        k //= 2


def _run_copies(runs_ref, make_copy, act):
    def per_expert(e, carry):
        n = runs_ref[0, 0, e]
        local_row = runs_ref[0, 1, e]
        sorted_row = runs_ref[0, 2, e]

        def chunk(i, c):
            act(make_copy(local_row + 8 * i, sorted_row + 8 * i, 8))
            return c

        lax.fori_loop(0, n >> 3, chunk, 0)
        _pow2_pieces(n, 4, lambda off, k: act(make_copy(local_row + off, sorted_row + off, k)))
        return carry

    lax.fori_loop(0, N_EXPERTS, per_expert, 0)


def _dispatch_kernel(seg_ref, lpos_ref, runs_ref, h_ref, xs_ref, local, zbuf, sem):
    tm = h_ref.shape[0] // ROW_TILE
    n_rows = xs_ref.shape[0] // ROW_TILE
    zrows = zbuf.shape[0] // ROW_TILE

    def place(tok, carry):
        row = h_ref[_row_slice(tok, 1), :]
        for k in range(TOP_K):
            local[_row_slice(lpos_ref[0, 0, TOP_K * tok + k], 1), :] = row
        return carry

    lax.fori_loop(0, tm, place, 0, unroll=8)

    def to_sorted(local_row, sorted_row, n):
        return pltpu.make_async_copy(local.at[_row_slice(local_row, n)],
                                     xs_ref.at[_row_slice(sorted_row, n)], sem)

    _run_copies(runs_ref, to_sorted, lambda cp: cp.start())
    _run_copies(runs_ref, to_sorted, lambda cp: cp.wait())

    @pl.when(pl.program_id(0) == pl.num_programs(0) - 1)
    def _():
        zbuf[...] = jnp.zeros_like(zbuf)

        def zero_copy(first_row, k):
            return pltpu.make_async_copy(zbuf.at[_row_slice(0, k)], xs_ref.at[_row_slice(first_row, k)], sem)

        def pad_copies(act):
            def per_expert(e, carry):
                cnt = seg_ref[0, e]
                first = seg_ref[1, e] + cnt
                _pow2_pieces((-cnt) & (MOE_BLOCK - 1), zrows, lambda off, k: act(zero_copy(first + off, k)))
                return carry

            lax.fori_loop(0, N_EXPERTS, per_expert, 0)
            used = seg_ref[2, 0]

            def tail(i, carry):
                act(zero_copy(used + i * zrows, zrows))
                return carry

            lax.fori_loop(0, (n_rows - used) // zrows, tail, 0)

        pad_copies(lambda cp: cp.start())
        pad_copies(lambda cp: cp.wait())


def _dispatch(h2_tiles, seg, lpos, runs, n_rows, *, tm):
    n_tok = h2_tiles.shape[0] // ROW_TILE
    nt = n_tok // tm
    grid_spec = pltpu.PrefetchScalarGridSpec(
        num_scalar_prefetch=1,
        grid=(nt,),
        in_specs=[pl.BlockSpec((1, 1, TOP_K * tm), lambda i, c: (i, 0, 0), memory_space=pltpu.SMEM),
                  pl.BlockSpec((1, 3, N_EXPERTS), lambda i, c: (i, 0, 0), memory_space=pltpu.SMEM),
                  pl.BlockSpec((tm * ROW_TILE, V7X_LANES), lambda i, c: (i, 0))],
        out_specs=pl.BlockSpec(memory_space=pl.ANY),
        scratch_shapes=[pltpu.VMEM((TOP_K * tm * ROW_TILE, V7X_LANES), F32),
                        pltpu.VMEM((MOE_BLOCK // 2 * ROW_TILE, V7X_LANES), F32),
                        pltpu.SemaphoreType.DMA(())],
    )
    return pl.pallas_call(
        _dispatch_kernel,
        grid_spec=grid_spec,
        out_shape=jax.ShapeDtypeStruct((n_rows * ROW_TILE, V7X_LANES), F32),
        compiler_params=_cparams(("arbitrary",)),
        name="moe_dispatch",
    )(seg, lpos, runs, h2_tiles)


def _combine_kernel(*refs, final):
    if final:
        lpos_ref, gate_ref, runs_ref, next_runs_ref, x1_ref, yb_ref, g_ref, o_ref, local, acc, sem = refs
    else:
        lpos_ref, gate_ref, runs_ref, next_runs_ref, x1_ref, yb_ref, o_ref, local, acc, sem = refs
    tm = x1_ref.shape[0]
    step = pl.program_id(0)
    cur = step % 2

    def from_sorted(slot):
        def make(local_row, sorted_row, n):
            return pltpu.make_async_copy(yb_ref.at[_row_slice(sorted_row, n)],
                                         local.at[slot, _row_slice(local_row, n)], sem.at[slot])
        return make

    @pl.when(step == 0)
    def _():
        _run_copies(runs_ref, from_sorted(cur), lambda cp: cp.start())

    @pl.when(step + 1 < pl.num_programs(0))
    def _():
        _run_copies(next_runs_ref, from_sorted(1 - cur), lambda cp: cp.start())

    _run_copies(runs_ref, from_sorted(cur), lambda cp: cp.wait())

    def mix(tok, carry):
        y = jnp.zeros((ROW_TILE, V7X_LANES), F32)
        for k in range(TOP_K):
            idx = TOP_K * tok + k
            y = y + gate_ref[0, 0, idx] * local[cur, _row_slice(lpos_ref[0, 0, idx], 1), :]
        acc[_row_slice(tok, 1), :] = y
        return carry

    lax.fori_loop(0, tm, mix, 0, unroll=8)
    x2 = x1_ref[...] + _load_row_tiles(acc, tm)
    if final:
        ms = jnp.mean(x2 * x2, axis=-1, keepdims=True)
        x2 = x2 * lax.rsqrt(ms + NORM_EPS) * g_ref[...]
    o_ref[...] = x2


def _combine(x1, yb, lpos, gates, runs, *, tm, final_g=None):
    n_tok, d = x1.shape
    nt = n_tok // tm
    final = final_g is not None
    in_specs = [pl.BlockSpec((1, 1, TOP_K * tm), lambda i: (i, 0, 0), memory_space=pltpu.SMEM),
                pl.BlockSpec((1, 1, TOP_K * tm), lambda i: (i, 0, 0), memory_space=pltpu.SMEM),
                pl.BlockSpec((1, 3, N_EXPERTS), lambda i: (i, 0, 0), memory_space=pltpu.SMEM),
                pl.BlockSpec((1, 3, N_EXPERTS), lambda i: (jnp.minimum(i + 1, nt - 1), 0, 0),
                             memory_space=pltpu.SMEM),
                pl.BlockSpec((tm, d), lambda i: (i, 0)),
                pl.BlockSpec(memory_space=pl.ANY)]
    args = [lpos, gates, runs, runs, x1, yb]
    if final:
        in_specs.append(pl.BlockSpec((1, d), lambda i: (0, 0)))
        args.append(final_g.reshape(1, d))
    return pl.pallas_call(
        functools.partial(_combine_kernel, final=final),
        grid=(nt,),
        in_specs=in_specs,
        out_specs=pl.BlockSpec((tm, d), lambda i: (i, 0)),
        out_shape=jax.ShapeDtypeStruct((n_tok, d), F32),
        scratch_shapes=[pltpu.VMEM((2, TOP_K * tm * ROW_TILE, V7X_LANES), F32),
                        pltpu.VMEM((tm * ROW_TILE, V7X_LANES), F32),
                        pltpu.SemaphoreType.DMA((2,))],
        compiler_params=_cparams(("arbitrary",)),
        name="moe_combine_final" if final else "moe_combine",
    )(*args)


def _moe_plan(rt, n_tok, tm):
    nt = n_tok // tm
    expert = rt[:, 0:TOP_K].astype(jnp.int32)
    gate = rt[:, TOP_K:2 * TOP_K]
    pos = rt[:, 2 * TOP_K:3 * TOP_K].astype(jnp.int32)
    is_e = (expert[:, :, None] == jnp.arange(N_EXPERTS, dtype=jnp.int32)).reshape(nt, tm, TOP_K, N_EXPERTS)
    cnt_tile = jnp.sum(is_e, axis=(1, 2), dtype=jnp.int32)
    seen_before = jnp.cumsum(cnt_tile, axis=0) - cnt_tile
    local_start = jnp.cumsum(cnt_tile, axis=1) - cnt_tile
    counts = jnp.sum(cnt_tile, axis=0)
    shift = local_start - seen_before
    lpos = pos.reshape(nt, tm, TOP_K) + jnp.sum(jnp.where(is_e, shift[:, None, None, :], 0), axis=-1)
    nblk = (counts + MOE_BLOCK - 1) // MOE_BLOCK
    bend = jnp.cumsum(nblk)
    seg_start = (bend - nblk) * MOE_BLOCK
    sorted_start = seg_start[None, :] + seen_before
    runs = jnp.stack([cnt_tile, local_start, sorted_start], axis=1)
    seg = jnp.stack([counts, seg_start, jnp.full((N_EXPERTS,), bend[-1] * MOE_BLOCK, jnp.int32)])
    n_steps = n_tok * TOP_K // MOE_BLOCK + N_EXPERTS
    step = jnp.arange(n_steps, dtype=jnp.int32)
    valid = step < bend[-1]
    brow = jnp.minimum(step, bend[-1] - 1)
    bexp = jnp.sum(brow[:, None] >= bend[None, :], axis=1).astype(jnp.int32)
    bfirst = jnp.concatenate([jnp.ones((1,), jnp.int32), (bexp[1:] != bexp[:-1]).astype(jnp.int32)])
    return dict(seg=seg.astype(jnp.int32), n_rows=n_steps * MOE_BLOCK,
                lpos=lpos.reshape(nt, 1, TOP_K * tm), gates=gate.reshape(nt, 1, TOP_K * tm),
                runs=runs, bexp=bexp, brow=brow.astype(jnp.int32), bvalid=valid.astype(jnp.int32), bfirst=bfirst)


def _moe_layer(x1, h2_tiles, rt, w1, w3, w2, layer, *, final_g=None):
    n_tok = x1.shape[0]
    tm = 1024
    plan = _moe_plan(rt, n_tok, tm)
    xs = _dispatch(h2_tiles, plan["seg"], plan["lpos"], plan["runs"], plan["n_rows"], tm=tm)
    yb = _moe_ffn(xs, plan["bexp"], plan["brow"], plan["bvalid"], plan["bfirst"], w1, w3, w2, layer)
    return _combine(x1, yb, plan["lpos"], plan["gates"], plan["runs"], tm=tm, final_g=final_g)


def kernel(x, mem, mem_norm, final_norm, norm_mix, norm_ffn, w_mem_kv, w_out, na_w_in, na_rpb, gla_w_in,
           gla_gate_up, gla_gate_bias, gla_out_norm, moe_w_group, moe_b_group, moe_w_router, moe_b_router,
           moe_w1, moe_w3, moe_w2):
    b, t, d = x.shape
    n = b * t
    depth = norm_mix.shape[0]
    n_mem = mem.shape[1]
    xf = x.reshape(n, d)
    for i in range(depth):
        j = i // 2
        mkv = _norm_matmul(mem.reshape(b * n_mem, d), mem_norm, w_mem_kv[i].astype(BF16),
                           tm=256, out_dtype=F32, name="mem_kv_proj").reshape(b, n_mem, 2 * MEM_DIM)
        kbd, vbd = _memory_kv_blockdiag(mkv)
        if i % 2 == 0:
            u = _norm_matmul(xf, norm_mix[i], na_w_in[j].astype(BF16), tm=512, out_dtype=BF16,
                             name="na_in_proj").reshape(b, t, -1)
            mix = _na_attention(u, na_rpb[j])
            mq_off = NA_MQ_OFF
            w_o = w_out[i].astype(BF16)
            w_o = (w_o[:MIX_DIM], w_o[MIX_DIM:])
        else:
            w_pad, gu, gb, onorm = _gla_weights(gla_w_in[j], gla_gate_up[j], gla_gate_bias[j], gla_out_norm[j])
            u = _norm_matmul(xf, norm_mix[i], w_pad, tm=512, out_dtype=BF16, name="gla_in_proj").reshape(b, t, -1)
            ofwd = _gla_direction(u, gu[0], gb[0], reverse=False, tb=512)
            mix = _gla_direction(u, gu[1], gb[1], reverse=True, tb=512, ofwd=ofwd, onorm=onorm)
            mq_off = GLA_MQ_OFF
            w_mix = w_out[i][:MIX_DIM].reshape(GLA_HEADS, GLA_DV, d)
            w_mix = jnp.pad(w_mix, ((0, 0), (0, GLA_DV_PAD - GLA_DV), (0, 0))).reshape(GLA_MIX_PAD, d)
            w_o = (w_mix.astype(BF16), w_out[i][MIX_DIM:].astype(BF16))
        w_rt = jnp.pad(jnp.concatenate([moe_w_group[i], moe_w_router[i]], axis=1),
                       ((0, 0), (0, V7X_LANES - N_GROUPS - N_EXPERTS))).astype(F32)
        w_rt = jnp.concatenate(_split_bf16(w_rt, 2), axis=1)
        b_rt = jnp.pad(jnp.concatenate([moe_b_group[i], moe_b_router[i]]),
                       (0, V7X_LANES - N_GROUPS - N_EXPERTS)).reshape(1, V7X_LANES)
        x1, h2_tiles, rt = _post_mixer(xf.reshape(b, t, d), mix, u, mq_off, kbd, vbd, w_o, norm_ffn[i],
                                       w_rt, b_rt.astype(F32), tm=POST_TILE)
        xf = _moe_layer(x1.reshape(n, d), h2_tiles, rt.reshape(n, V7X_LANES),
                        moe_w1, moe_w3, moe_w2, i,
                        final_g=final_norm if i == depth - 1 else None)
    return xf.reshape(b, t, d)
```

```python
import functools

import numpy as np
import jax
import jax.numpy as jnp
from jax import lax
from jax.experimental import pallas as pl
from jax.experimental.pallas import tpu as pltpu

F32 = jnp.float32
BF16 = jnp.bfloat16

D_MODEL = 1024
GRID_W = 64
HEAD_DIM = 64
NORM_EPS = 1e-6
MEM_HEADS = 4
MEM_DIM = MEM_HEADS * HEAD_DIM
MIX_DIM = D_MODEL - MEM_DIM
NA_HEADS = MIX_DIM // HEAD_DIM
NA_WIN_H = 8
NA_WIN_W = 16
GLA_HEADS = 4
GLA_DK = MIX_DIM // 2 // GLA_HEADS
GLA_DV = MIX_DIM // GLA_HEADS
GLA_GATE_RANK = 16
GLA_TAU = 16.0
GLA_CHUNK = 64
N_GROUPS = 4
EXPERTS_PER_GROUP = 8
N_EXPERTS = N_GROUPS * EXPERTS_PER_GROUP
TOP_K = 2
EXPERT_FF = 512
MOE_BLOCK = 256

V7X_LANES = 128
V7X_MXU_DIM = 256
V7X_VMEM_LIMIT_BYTES = 56 * 1024 * 1024

GLA_DK_PAD = 128
GLA_DV_PAD = 256
GLA_Q_OFF = 0
GLA_K_OFF = GLA_Q_OFF + GLA_HEADS * GLA_DK_PAD
GLA_V_OFF = GLA_K_OFF + GLA_HEADS * GLA_DK_PAD
GLA_G_OFF = GLA_V_OFF + GLA_HEADS * GLA_DV_PAD
GLA_R_OFF = GLA_G_OFF + GLA_HEADS * GLA_DV_PAD
GLA_R_PAD = 256
GLA_MQ_OFF = GLA_R_OFF + GLA_R_PAD
GLA_IN_PAD = GLA_MQ_OFF + MEM_DIM
GLA_MIX_PAD = GLA_HEADS * GLA_DV_PAD
NA_MQ_OFF = 3 * MIX_DIM
NA_ROWS_PER_STEP = 4
POST_TILE = 512
POST_SUBTILES = 2


def _cparams(semantics):
    return pltpu.CompilerParams(dimension_semantics=semantics, vmem_limit_bytes=V7X_VMEM_LIMIT_BYTES)


def _split_bf16(x, n_pieces):
    pieces = []
    rest = x
    for _ in range(n_pieces - 1):
        c = rest * 65537.0
        hi = c - (c - rest)
        pieces.append(hi.astype(BF16))
        rest = rest - hi
    pieces.append(rest.astype(BF16))
    return pieces


ROW_TILE = D_MODEL // V7X_LANES


def _store_row_tiles(ref, val):
    rows = val.shape[0]
    for s in range(ROW_TILE):
        ref[pl.ds(s, rows, stride=ROW_TILE), :] = val[:, s * V7X_LANES:(s + 1) * V7X_LANES]


def _load_row_tiles(ref, rows):
    return jnp.concatenate([ref[pl.ds(s, rows, stride=ROW_TILE), :] for s in range(ROW_TILE)], axis=1)


def _norm_matmul_kernel(x_ref, g_ref, w_ref, o_ref, *, col_chunk):
    x = x_ref[...]
    ms = jnp.mean(x * x, axis=-1, keepdims=True)
    y = (x * lax.rsqrt(ms + NORM_EPS) * g_ref[...]).astype(BF16)
    n_out = o_ref.shape[1]
    for c in range(0, n_out, col_chunk):
        o_ref[:, c:c + col_chunk] = jnp.dot(
            y, w_ref[:, c:c + col_chunk], preferred_element_type=F32).astype(o_ref.dtype)


def _norm_matmul(x, g, w, *, tm, out_dtype, name):
    n, d = x.shape
    n_out = w.shape[1]
    col_chunk = 512 if n_out % 512 == 0 else n_out
    return pl.pallas_call(
        functools.partial(_norm_matmul_kernel, col_chunk=col_chunk),
        grid=(n // tm,),
        in_specs=[pl.BlockSpec((tm, d), lambda i: (i, 0)),
                  pl.BlockSpec((1, d), lambda i: (0, 0)),
                  pl.BlockSpec((d, n_out), lambda i: (0, 0))],
        out_specs=pl.BlockSpec((tm, n_out), lambda i: (i, 0)),
        out_shape=jax.ShapeDtypeStruct((n, n_out), out_dtype),
        compiler_params=_cparams(("parallel",)),
        name=name,
    )(x, g.reshape(1, d), w)


def _na_kernel(q_ref, k_ref, v_ref, *rest):
    *bias_refs, o_ref = rest
    rows = k_ref.shape[1] // GRID_W
    n_keys = NA_WIN_H * GRID_W
    heads_per_slab = V7X_MXU_DIM // HEAD_DIM
    scale = HEAD_DIM ** -0.5
    lane_head = lax.broadcasted_iota(jnp.int32, (GRID_W, V7X_MXU_DIM), 1) // HEAD_DIM
    slabs = [slice(s * V7X_MXU_DIM, (s + 1) * V7X_MXU_DIM) for s in range(MIX_DIM // V7X_MXU_DIM)]

    def window_start(rr):
        r = pl.program_id(1) * len(bias_refs) + rr
        rs = jnp.clip(r - NA_WIN_H // 2, 0, rows - NA_WIN_H)
        return pl.multiple_of(rs * GRID_W, GRID_W)

    def scores(rr, cs):
        qq = q_ref[0, rr * GRID_W:(rr + 1) * GRID_W, cs] * scale
        kw = k_ref[0, pl.ds(window_start(rr), n_keys), cs]
        lhs = jnp.concatenate(
            [jnp.where(lane_head == i, qq, jnp.zeros_like(qq)) for i in range(heads_per_slab)], axis=0)
        sc = lax.dot_general(lhs, kw, (((1,), (1,)), ((), ())), preferred_element_type=F32)
        return sc + bias_refs[rr][0, cs, :]

    units = [(rr, cs) for rr in range(len(bias_refs)) for cs in slabs]
    nxt = scores(*units[0])
    for ui, (rr, cs) in enumerate(units):
        sc = nxt
        if ui + 1 < len(units):
            nxt = scores(*units[ui + 1])
        vw = v_ref[0, pl.ds(window_start(rr), n_keys), cs]
        m = jnp.max(sc, axis=-1, keepdims=True)
        p = jnp.exp(sc - m)
        l = jnp.sum(p, axis=-1, keepdims=True)
        o = jnp.dot(p.astype(BF16), vw, preferred_element_type=F32)
        o = o * (1.0 / l)
        acc = jnp.zeros((GRID_W, V7X_MXU_DIM), F32)
        for i in range(heads_per_slab):
            acc = acc + jnp.where(lane_head == i, o[i * GRID_W:(i + 1) * GRID_W], 0.0)
        o_ref[0, rr * GRID_W:(rr + 1) * GRID_W, cs] = acc.astype(o_ref.dtype)


def _na_bias_table(rpb):
    qc = np.arange(GRID_W)[:, None]
    kc = np.arange(GRID_W)[None, :]
    cstart = np.clip(qc - NA_WIN_W // 2, 0, GRID_W - NA_WIN_W)
    col_in = (kc >= cstart) & (kc < cstart + NA_WIN_W)
    dcol = np.clip(kc - qc, 1 - NA_WIN_W, NA_WIN_W - 1) + NA_WIN_W - 1
    pick = jnp.asarray(dcol[None] == np.arange(2 * NA_WIN_W - 1)[:, None, None], F32)
    rows = jnp.stack([rpb.astype(F32)[:, o:o + NA_WIN_H] for o in range(NA_WIN_H)])
    tbl = jnp.einsum('ohjd,dqk->ohqjk', rows, pick, precision=lax.Precision.HIGHEST)
    tbl = jnp.where(col_in[None, None, :, None, :], tbl, -jnp.inf)
    return tbl.reshape(NA_WIN_H, NA_HEADS * GRID_W, NA_WIN_H * GRID_W)


def _na_attention(u, rpb):
    b, t, _ = u.shape
    rows = t // GRID_W
    bias = _na_bias_table(rpb)

    def bias_spec(rr):
        def bias_idx(bi, g):
            r = g * NA_ROWS_PER_STEP + rr
            return (jnp.clip(r - NA_WIN_H // 2, 0, rows - NA_WIN_H) - r + NA_WIN_H - 1, 0, 0)
        return pl.BlockSpec((1, NA_HEADS * GRID_W, NA_WIN_H * GRID_W), bias_idx)

    qt = NA_ROWS_PER_STEP * GRID_W
    return pl.pallas_call(
        _na_kernel,
        grid=(b, rows // NA_ROWS_PER_STEP),
        in_specs=[pl.BlockSpec((1, qt, MIX_DIM), lambda bi, g: (bi, g, 0)),
                  pl.BlockSpec((1, t, MIX_DIM), lambda bi, g: (bi, 0, 1)),
                  pl.BlockSpec((1, t, MIX_DIM), lambda bi, g: (bi, 0, 2))]
        + [bias_spec(rr) for rr in range(NA_ROWS_PER_STEP)],
        out_specs=pl.BlockSpec((1, qt, MIX_DIM), lambda bi, g: (bi, g, 0)),
        out_shape=jax.ShapeDtypeStruct((b, t, MIX_DIM), BF16),
        compiler_params=_cparams(("parallel", "arbitrary")),
        name="na_attention",
    )(u, u, u, *([bias] * NA_ROWS_PER_STEP))


def _gla_kernel(*refs, reverse, final):
    if final:
        (q_ref, k_ref, v_ref, r_ref, gu_ref, gb_ref, ofwd_ref, g_ref, onorm_ref, o_ref, st_ref) = refs
    else:
        (q_ref, k_ref, v_ref, r_ref, gu_ref, gb_ref, o_ref, st_ref) = refs
    c = GLA_CHUNK

    @pl.when(pl.program_id(1) == 0)
    def _():
        st_ref[...] = jnp.zeros_like(st_ref)

    z = jnp.dot(r_ref[0], gu_ref[...], preferred_element_type=F32) + gb_ref[...]
    la = (jnp.minimum(z, 0.0) - jnp.log(1.0 + jnp.exp(-jnp.abs(z)))) * (1.0 / GLA_TAU)
    ri = lax.broadcasted_iota(jnp.int32, (c, c), 0)
    ci = lax.broadcasted_iota(jnp.int32, (c, c), 1)
    tri = (ci >= ri) if reverse else (ci <= ri)
    trib = jnp.where(tri, 1.0, 0.0).astype(BF16)
    la3 = jnp.concatenate(_split_bf16(la, 3), axis=1)
    gw = la.shape[1]
    mid = c // 2 if reverse else c // 2 - 1
    last = 0 if reverse else c - 1
    scale = GLA_DK ** -0.5
    n_chunks = q_ref.shape[1] // c
    order = range(n_chunks - 1, -1, -1) if reverse else range(n_chunks)
    state = [st_ref[h] for h in range(GLA_HEADS)]
    heads = range(GLA_HEADS)
    hsl = [slice(h * GLA_DK_PAD, (h + 1) * GLA_DK_PAD) for h in heads]
    vsl = [slice(h * GLA_DV_PAD, (h + 1) * GLA_DV_PAD) for h in heads]
    nt_dims = (((1,), (1,)), ((), ()))
    tn_dims = (((0,), (0,)), ((), ()))

    def prepare(ch):
        sl = slice(ch * c, (ch + 1) * c)
        b3 = jnp.dot(trib, la3[sl], preferred_element_type=F32)
        bcum = b3[:, :gw] + b3[:, gw:2 * gw] + b3[:, 2 * gw:]
        b_mid = bcum[mid:mid + 1]
        b_last = bcum[last:last + 1]
        qc = q_ref[0, sl, :].astype(F32) * scale
        kc = k_ref[0, sl, :].astype(F32)
        return dict(sl=sl, vc=v_ref[0, sl, :],
                    qe=(qc * jnp.exp(bcum - b_mid)).astype(BF16), ke=(kc * jnp.exp(b_mid - bcum)).astype(BF16),
                    qs=(qc * jnp.exp(bcum)).astype(BF16), ks=(kc * jnp.exp(b_last - bcum)).astype(BF16),
                    dec=jnp.exp(b_last))

    order = list(order)
    nxt = prepare(order[0])
    for pos_in_step, ch in enumerate(order):
        cur = nxt
        if pos_in_step + 1 < len(order):
            nxt = prepare(order[pos_in_step + 1])
        sl, vc, dec = cur["sl"], cur["vc"], cur["dec"]
        scores = [lax.dot_general(cur["qe"][:, hsl[h]], cur["ke"][:, hsl[h]], nt_dims, preferred_element_type=F32)
                  for h in heads]
        kv_t = [lax.dot_general(vc[:, vsl[h]], cur["ks"][:, hsl[h]], tn_dims, preferred_element_type=F32)
                for h in heads]
        o_inter = [lax.dot_general(cur["qs"][:, hsl[h]], state[h].astype(BF16), nt_dims, preferred_element_type=F32)
                   for h in heads]
        masked = [jnp.where(tri, scores[h], 0.0).astype(BF16) for h in heads]
        o_intra = [jnp.dot(masked[h], vc[:, vsl[h]], preferred_element_type=F32) for h in heads]
        for h in heads:
            hs, vs = hsl[h], vsl[h]
            o = o_intra[h] + o_inter[h]
            state[h] = state[h] * dec[:, hs] + kv_t[h]
            if final:
                tot = ofwd_ref[0, sl, vs] + o
                ms = jnp.sum(tot * tot, axis=-1, keepdims=True) * (1.0 / GLA_DV)
                y = tot * lax.rsqrt(ms + NORM_EPS) * onorm_ref[:, vs]
                g = g_ref[0, sl, vs].astype(F32)
                o_ref[0, sl, vs] = (y * (g / (1.0 + jnp.exp(-g)))).astype(o_ref.dtype)
            else:
                o_ref[0, sl, vs] = o
    for h in range(GLA_HEADS):
        st_ref[h] = state[h]


def _gla_direction(u, gu, gb, *, reverse, tb, ofwd=None, onorm=None):
    b, t, _ = u.shape
    nt = t // tb
    final = ofwd is not None
    tix = (lambda ti: nt - 1 - ti) if reverse else (lambda ti: ti)
    qw = GLA_HEADS * GLA_DK_PAD
    vw = GLA_HEADS * GLA_DV_PAD
    in_specs = [pl.BlockSpec((1, tb, qw), lambda bi, ti: (bi, tix(ti), GLA_Q_OFF // qw)),
                pl.BlockSpec((1, tb, qw), lambda bi, ti: (bi, tix(ti), GLA_K_OFF // qw)),
                pl.BlockSpec((1, tb, vw), lambda bi, ti: (bi, tix(ti), GLA_V_OFF // vw)),
                pl.BlockSpec((1, tb, GLA_R_PAD), lambda bi, ti: (bi, tix(ti), GLA_R_OFF // GLA_R_PAD)),
                pl.BlockSpec((GLA_R_PAD, qw), lambda bi, ti: (0, 0)),
                pl.BlockSpec((1, qw), lambda bi, ti: (0, 0))]
    args = [u, u, u, u, gu, gb]
    if final:
        in_specs += [pl.BlockSpec((1, tb, vw), lambda bi, ti: (bi, tix(ti), 0)),
                     pl.BlockSpec((1, tb, vw), lambda bi, ti: (bi, tix(ti), GLA_G_OFF // vw)),
                     pl.BlockSpec((1, vw), lambda bi, ti: (0, 0))]
        args += [ofwd, u, onorm]
    return pl.pallas_call(
        functools.partial(_gla_kernel, reverse=reverse, final=final),
        grid=(b, nt),
        in_specs=in_specs,
        out_specs=pl.BlockSpec((1, tb, vw), lambda bi, ti: (bi, tix(ti), 0)),
        out_shape=jax.ShapeDtypeStruct((b, t, vw), BF16 if final else F32),
        scratch_shapes=[pltpu.VMEM((GLA_HEADS, GLA_DV_PAD, GLA_DK_PAD), F32)],
        compiler_params=_cparams(("parallel", "arbitrary")),
        name="gla_bwd_final" if final else "gla_fwd",
    )(*args)


def _gla_weights(w_in, gate_up, gate_bias, out_norm):
    d = w_in.shape[0]
    kd = GLA_HEADS * GLA_DK
    wq, wk, wv, wg, wr, wmq = jnp.split(
        w_in, np.cumsum([kd, kd, MIX_DIM, MIX_DIM, 2 * GLA_GATE_RANK]), axis=1)

    def pad_heads(w, dh, dh_pad):
        w = w.reshape(w.shape[0], GLA_HEADS, dh)
        w = jnp.pad(w, ((0, 0), (0, 0), (0, dh_pad - dh)))
        return w.reshape(w.shape[0], GLA_HEADS * dh_pad)

    w_pad = jnp.concatenate([
        pad_heads(wq, GLA_DK, GLA_DK_PAD), pad_heads(wk, GLA_DK, GLA_DK_PAD),
        pad_heads(wv, GLA_DV, GLA_DV_PAD), pad_heads(wg, GLA_DV, GLA_DV_PAD),
        jnp.pad(wr, ((0, 0), (0, GLA_R_PAD - 2 * GLA_GATE_RANK))), wmq], axis=1)
    gu = []
    for di in range(2):
        up = pad_heads(gate_up[di], GLA_DK, GLA_DK_PAD)
        gu.append(jnp.pad(up, ((di * GLA_GATE_RANK, GLA_R_PAD - (di + 1) * GLA_GATE_RANK), (0, 0))))
    gb = [pad_heads(gate_bias[di][None, :], GLA_DK, GLA_DK_PAD) for di in range(2)]
    onorm = pad_heads(out_norm[None, :].repeat(GLA_HEADS, 0).reshape(1, MIX_DIM), GLA_DV, GLA_DV_PAD)
    return w_pad.astype(BF16), [g.astype(BF16) for g in gu], [x.astype(F32) for x in gb], onorm.astype(F32)


def _post_kernel(x_ref, mix_ref, mq_ref, kbd_ref, vbd_ref, wom_ref, woa_ref, g_ref, wrt_ref, brt_ref,
                 x1_ref, h2_ref, rt_ref, base_ref):
    @pl.when((pl.program_id(0) == 0) & (pl.program_id(1) == 0))
    def _():
        base_ref[...] = jnp.zeros_like(base_ref)

    tm = x_ref.shape[1]
    hm = tm // POST_SUBTILES
    groups = [slice(i * hm, (i + 1) * hm) for i in range(POST_SUBTILES)]
    n_mem = kbd_ref.shape[2] // MEM_HEADS
    lane = lax.broadcasted_iota(jnp.int32, (hm, V7X_LANES), 1).astype(F32)
    neg = -jnp.inf
    big = 1e9

    s = [jnp.dot(mq_ref[0, g, :], kbd_ref[0], preferred_element_type=F32) * (HEAD_DIM ** -0.5)
         for g in groups]
    mixed = [jnp.dot(mix_ref[0, g, :], wom_ref[...], preferred_element_type=F32) for g in groups]
    probs = []
    for sg in s:
        ps = []
        for h in range(MEM_HEADS):
            seg = sg[:, h * n_mem:(h + 1) * n_mem]
            e = jnp.exp(seg - jnp.max(seg, axis=-1, keepdims=True))
            ps.append((e * (1.0 / jnp.sum(e, axis=-1, keepdims=True))).astype(BF16))
        probs.append(jnp.concatenate(ps, axis=1))
    att = [jnp.dot(p, vbd_ref[0], preferred_element_type=F32).astype(BF16) for p in probs]
    logits = []
    for i, g in enumerate(groups):
        x1 = x_ref[0, g, :] + mixed[i] + jnp.dot(att[i], woa_ref[...], preferred_element_type=F32)
        x1_ref[0, g, :] = x1
        ms = jnp.mean(x1 * x1, axis=-1, keepdims=True)
        h2 = x1 * lax.rsqrt(ms + NORM_EPS) * g_ref[...]
        _store_row_tiles(h2_ref.at[pl.ds(i * hm * ROW_TILE, hm * ROW_TILE)], h2)
        h_hi, h_lo = _split_bf16(h2, 2)
        hw = jnp.dot(h_hi, wrt_ref[...], preferred_element_type=F32)
        logits.append(hw[:, :V7X_LANES] + hw[:, V7X_LANES:]
                      + jnp.dot(h_lo, wrt_ref[:, :V7X_LANES], preferred_element_type=F32) + brt_ref[...])
    routed = []
    for lg in logits:
        gl = jnp.where(lane < N_GROUPS, lg, neg)
        gm = jnp.max(gl, axis=-1, keepdims=True)
        g_gate = 1.0 / jnp.sum(jnp.exp(gl - gm), axis=-1, keepdims=True)
        g_idx = jnp.min(jnp.where(gl == gm, lane, big), axis=-1, keepdims=True)
        lo = N_GROUPS + EXPERTS_PER_GROUP * g_idx
        v1 = jnp.where((lane >= lo) & (lane < lo + EXPERTS_PER_GROUP), lg, neg)
        m1 = jnp.max(v1, axis=-1, keepdims=True)
        i1 = jnp.min(jnp.where(v1 == m1, lane, big), axis=-1, keepdims=True)
        v2 = jnp.where(lane == i1, neg, v1)
        m2 = jnp.max(v2, axis=-1, keepdims=True)
        i2 = jnp.min(jnp.where(v2 == m2, lane, big), axis=-1, keepdims=True)
        t = jnp.exp(m2 - m1)
        routed.append((i1 - N_GROUPS, i2 - N_GROUPS, g_gate / (1.0 + t), g_gate * t / (1.0 + t)))
    ri = lax.broadcasted_iota(jnp.int32, (hm, hm), 0)
    ci = lax.broadcasted_iota(jnp.int32, (hm, hm), 1)
    before = jnp.where(ci < ri, 1.0, 0.0).astype(BF16)
    onehots = [jnp.where(lane == e1, 1.0, 0.0) + jnp.where(lane == e2, 1.0, 0.0)
               for e1, e2, _, _ in routed]
    earlier = [jnp.dot(before, oh.astype(BF16), preferred_element_type=F32) for oh in onehots]
    for i, g in enumerate(groups):
        e1, e2, w1, w2 = routed[i]
        ahead = earlier[i] + base_ref[...]
        pos1 = jnp.sum(jnp.where(lane == e1, ahead, 0.0), axis=-1, keepdims=True)
        pos2 = jnp.sum(jnp.where(lane == e2, ahead, 0.0), axis=-1, keepdims=True)
        base_ref[...] = base_ref[...] + jnp.sum(onehots[i], axis=0, keepdims=True)
        rt_ref[0, g, :] = jnp.where(
            lane == 0, e1, jnp.where(lane == 1, e2, jnp.where(lane == 2, w1, jnp.where(
                lane == 3, w2, jnp.where(lane == 4, pos1, jnp.where(lane == 5, pos2, 0.0))))))


def _post_mixer(x, mix, u, mq_off, kbd, vbd, w_out, g, w_rt, b_rt, *, tm):
    b, t, d = x.shape
    mixw = mix.shape[2]
    wom = w_out[0]
    woa = w_out[1]
    return pl.pallas_call(
        _post_kernel,
        grid=(b, t // tm),
        in_specs=[pl.BlockSpec((1, tm, d), lambda bi, ti: (bi, ti, 0)),
                  pl.BlockSpec((1, tm, mixw), lambda bi, ti: (bi, ti, 0)),
                  pl.BlockSpec((1, tm, MEM_DIM), lambda bi, ti: (bi, ti, mq_off // MEM_DIM)),
                  pl.BlockSpec((1,) + kbd.shape[1:], lambda bi, ti: (bi, 0, 0)),
                  pl.BlockSpec((1,) + vbd.shape[1:], lambda bi, ti: (bi, 0, 0)),
                  pl.BlockSpec(wom.shape, lambda bi, ti: (0, 0)),
                  pl.BlockSpec(woa.shape, lambda bi, ti: (0, 0)),
                  pl.BlockSpec((1, d), lambda bi, ti: (0, 0)),
                  pl.BlockSpec(w_rt.shape, lambda bi, ti: (0, 0)),
                  pl.BlockSpec((1, V7X_LANES), lambda bi, ti: (0, 0))],
        out_specs=[pl.BlockSpec((1, tm, d), lambda bi, ti: (bi, ti, 0)),
                   pl.BlockSpec((tm * ROW_TILE, V7X_LANES), lambda bi, ti: (bi * (t // tm) + ti, 0)),
                   pl.BlockSpec((1, tm, V7X_LANES), lambda bi, ti: (bi, ti, 0))],
        out_shape=[jax.ShapeDtypeStruct((b, t, d), F32),
                   jax.ShapeDtypeStruct((b * t * ROW_TILE, V7X_LANES), F32),
                   jax.ShapeDtypeStruct((b, t, V7X_LANES), F32)],
        scratch_shapes=[pltpu.VMEM((1, V7X_LANES), F32)],
        compiler_params=_cparams(("arbitrary", "arbitrary")),
        name="post_mixer",
    )(x, mix, u, kbd, vbd, wom, woa, g.reshape(1, d), w_rt, b_rt)


def _memory_kv_blockdiag(mkv):
    k, v = jnp.split(mkv, 2, axis=-1)
    head_of = np.arange(MEM_DIM) // HEAD_DIM
    sel = jnp.asarray(head_of[None, :] == np.arange(MEM_HEADS)[:, None], F32)
    kbd = jnp.einsum('bmd,hd->bdhm', k, sel).reshape(k.shape[0], MEM_DIM, -1)
    vbd = jnp.einsum('bmd,hd->bhmd', v, sel).reshape(v.shape[0], -1, MEM_DIM)
    return kbd.astype(BF16), vbd.astype(BF16)


FFN_X_SLOTS = 3
FFN_Y_SLOTS = 2
FFN_W_SLOTS = 2


def _moe_kernel(nblk_ref, bexp_ref, bfirst_ref, nexp_ref, wslot_ref, w1_ref, w3_ref, w2_ref, xs_ref, y_ref,
                wf1, wf3, wf2, wb1, wb3, wb2, xbuf, ybuf, wsem, xsem, ysem, *, layer):
    del xs_ref
    n_blk = nblk_ref[0]

    def rows_of(blk):
        return _row_slice(blk * MOE_BLOCK, MOE_BLOCK)

    def fetch(blk):
        slot = blk % FFN_X_SLOTS
        return pltpu.make_async_copy(y_ref.at[rows_of(blk)], xbuf.at[slot], xsem.at[slot])

    def writeback(blk):
        slot = blk % FFN_Y_SLOTS
        return pltpu.make_async_copy(ybuf.at[slot], y_ref.at[rows_of(blk)], ysem.at[slot])

    def weight_copies(expert, slot):
        return [pltpu.make_async_copy(src.at[layer, expert], dst.at[slot], wsem.at[slot, j])
                for j, (src, dst) in enumerate(((w1_ref, wf1), (w3_ref, wf3), (w2_ref, wf2)))]

    for cp in weight_copies(bexp_ref[0], 0):
        cp.start()
    for ahead in range(FFN_X_SLOTS - 1):
        @pl.when(ahead < n_blk)
        def _(ahead=ahead):
            fetch(ahead).start()

    def per_block(blk, carry):
        @pl.when(blk + FFN_X_SLOTS - 1 < n_blk)
        def _():
            fetch(blk + FFN_X_SLOTS - 1).start()

        @pl.when(bfirst_ref[blk] != 0)
        def _():
            slot = wslot_ref[blk]

            @pl.when(nexp_ref[blk] >= 0)
            def _():
                for cp in weight_copies(nexp_ref[blk], 1 - slot):
                    cp.start()

            for cp in weight_copies(bexp_ref[blk], slot):
                cp.wait()
            wb1[...] = wf1[slot].astype(BF16)
            wb3[...] = wf3[slot].astype(BF16)
            wb2[...] = wf2[slot].astype(BF16)

        fetch(blk).wait()

        @pl.when(blk >= FFN_Y_SLOTS)
        def _():
            writeback(blk - FFN_Y_SLOTS).wait()

        xb = _load_row_tiles(xbuf.at[blk % FFN_X_SLOTS], MOE_BLOCK).astype(BF16)
        half = wb1.shape[1] // 2
        up = [(jnp.dot(xb, wb1[:, hf * half:(hf + 1) * half], preferred_element_type=F32),
               jnp.dot(xb, wb3[:, hf * half:(hf + 1) * half], preferred_element_type=F32)) for hf in range(2)]
        y = None
        for hf, (a, c) in enumerate(up):
            hid = (a / (1.0 + jnp.exp(-a)) * c).astype(BF16)
            part = jnp.dot(hid, wb2[hf * half:(hf + 1) * half, :], preferred_element_type=F32)
            y = part if y is None else y + part
        _store_row_tiles(ybuf.at[blk % FFN_Y_SLOTS], y)
        writeback(blk).start()
        return carry

    lax.fori_loop(0, n_blk, per_block, 0)
    for back in range(FFN_Y_SLOTS, 0, -1):
        @pl.when(n_blk >= back)
        def _(back=back):
            writeback(n_blk - back).wait()


def _moe_ffn(xs, sched, w1, w3, w2, layer):
    d = w1.shape[2]
    ff = w1.shape[3]
    blk_rows = MOE_BLOCK * ROW_TILE
    any_spec = pl.BlockSpec(memory_space=pl.ANY)
    grid_spec = pltpu.PrefetchScalarGridSpec(
        num_scalar_prefetch=5,
        grid=(1,),
        in_specs=[any_spec, any_spec, any_spec, any_spec],
        out_specs=any_spec,
        scratch_shapes=[pltpu.VMEM((FFN_W_SLOTS, d, ff), F32), pltpu.VMEM((FFN_W_SLOTS, d, ff), F32),
                        pltpu.VMEM((FFN_W_SLOTS, ff, d), F32),
                        pltpu.VMEM((d, ff), BF16), pltpu.VMEM((d, ff), BF16), pltpu.VMEM((ff, d), BF16),
                        pltpu.VMEM((FFN_X_SLOTS, blk_rows, V7X_LANES), F32),
                        pltpu.VMEM((FFN_Y_SLOTS, blk_rows, V7X_LANES), F32),
                        pltpu.SemaphoreType.DMA((FFN_W_SLOTS, 3)),
                        pltpu.SemaphoreType.DMA((FFN_X_SLOTS,)),
                        pltpu.SemaphoreType.DMA((FFN_Y_SLOTS,))],
    )
    return pl.pallas_call(
        functools.partial(_moe_kernel, layer=layer),
        grid_spec=grid_spec,
        out_shape=jax.ShapeDtypeStruct(xs.shape, F32),
        input_output_aliases={8: 0},
        compiler_params=_cparams(("arbitrary",)),
        name="moe_ffn",
    )(sched["nblk"], sched["bexp"], sched["bfirst"], sched["nexp"], sched["wslot"], w1, w3, w2, xs)


def _row_slice(row, n_rows):
    return pl.ds(pl.multiple_of(row * ROW_TILE, ROW_TILE), n_rows * ROW_TILE)


def _pow2_pieces(n, largest, act):
    k = largest
    while k >= 1:
        shift = k.bit_length()

        @pl.when((n & k) != 0)
        def _(k=k, shift=shift):
            act((n >> shift) << shift, k)
        k //= 2


def _run_copies(runs_ref, make_copy, act):
    def per_expert(e, carry):
        n = runs_ref[0, 0, e]
        local_row = runs_ref[0, 1, e]
        sorted_row = runs_ref[0, 2, e]

        def chunk(i, c):
            act(make_copy(local_row + 8 * i, sorted_row + 8 * i, 8))
            return c

        lax.fori_loop(0, n >> 3, chunk, 0)
        _pow2_pieces(n, 4, lambda off, k: act(make_copy(local_row + off, sorted_row + off, k)))
        return carry

    lax.fori_loop(0, N_EXPERTS, per_expert, 0)


def _dispatch_kernel(seg_ref, lpos_ref, runs_ref, h_ref, xs_ref, local, zbuf, sem):
    tm = h_ref.shape[0] // ROW_TILE
    n_rows = xs_ref.shape[0] // ROW_TILE
    zrows = zbuf.shape[0] // ROW_TILE

    def place(tok, carry):
        row = h_ref[_row_slice(tok, 1), :]
        for k in range(TOP_K):
            local[_row_slice(lpos_ref[0, 0, TOP_K * tok + k], 1), :] = row
        return carry

    lax.fori_loop(0, tm, place, 0, unroll=8)

    def to_sorted(local_row, sorted_row, n):
        return pltpu.make_async_copy(local.at[_row_slice(local_row, n)],
                                     xs_ref.at[_row_slice(sorted_row, n)], sem)

    _run_copies(runs_ref, to_sorted, lambda cp: cp.start())
    _run_copies(runs_ref, to_sorted, lambda cp: cp.wait())

    @pl.when(pl.program_id(0) == pl.num_programs(0) - 1)
    def _():
        zbuf[...] = jnp.zeros_like(zbuf)

        def zero_copy(first_row, k):
            return pltpu.make_async_copy(zbuf.at[_row_slice(0, k)], xs_ref.at[_row_slice(first_row, k)], sem)

        def pad_copies(act):
            def per_expert(e, carry):
                cnt = seg_ref[0, e]
                first = seg_ref[1, e] + cnt
                _pow2_pieces((-cnt) & (MOE_BLOCK - 1), zrows, lambda off, k: act(zero_copy(first + off, k)))
                return carry

            lax.fori_loop(0, N_EXPERTS, per_expert, 0)
            used = seg_ref[2, 0]

            def tail(i, carry):
                act(zero_copy(used + i * zrows, zrows))
                return carry

            lax.fori_loop(0, (n_rows - used) // zrows, tail, 0)

        pad_copies(lambda cp: cp.start())
        pad_copies(lambda cp: cp.wait())


def _dispatch(h2_tiles, seg, lpos, runs, n_rows, *, tm):
    n_tok = h2_tiles.shape[0] // ROW_TILE
    nt = n_tok // tm
    grid_spec = pltpu.PrefetchScalarGridSpec(
        num_scalar_prefetch=1,
        grid=(nt,),
        in_specs=[pl.BlockSpec((1, 1, TOP_K * tm), lambda i, c: (i, 0, 0), memory_space=pltpu.SMEM),
                  pl.BlockSpec((1, 3, N_EXPERTS), lambda i, c: (i, 0, 0), memory_space=pltpu.SMEM),
                  pl.BlockSpec((tm * ROW_TILE, V7X_LANES), lambda i, c: (i, 0))],
        out_specs=pl.BlockSpec(memory_space=pl.ANY),
        scratch_shapes=[pltpu.VMEM((TOP_K * tm * ROW_TILE, V7X_LANES), F32),
                        pltpu.VMEM((MOE_BLOCK // 2 * ROW_TILE, V7X_LANES), F32),
                        pltpu.SemaphoreType.DMA(())],
    )
    return pl.pallas_call(
        _dispatch_kernel,
        grid_spec=grid_spec,
        out_shape=jax.ShapeDtypeStruct((n_rows * ROW_TILE, V7X_LANES), F32),
        compiler_params=_cparams(("arbitrary",)),
        name="moe_dispatch",
    )(seg, lpos, runs, h2_tiles)


def _combine_kernel(*refs, final):
    if final:
        lpos_ref, gate_ref, runs_ref, next_runs_ref, x1_ref, yb_ref, g_ref, o_ref, local, acc, sem = refs
    else:
        lpos_ref, gate_ref, runs_ref, next_runs_ref, x1_ref, yb_ref, o_ref, local, acc, sem = refs
    tm = x1_ref.shape[0]
    step = pl.program_id(0)
    cur = step % 2

    def from_sorted(slot):
        def make(local_row, sorted_row, n):
            return pltpu.make_async_copy(yb_ref.at[_row_slice(sorted_row, n)],
                                         local.at[slot, _row_slice(local_row, n)], sem.at[slot])
        return make

    @pl.when(step == 0)
    def _():
        _run_copies(runs_ref, from_sorted(cur), lambda cp: cp.start())

    @pl.when(step + 1 < pl.num_programs(0))
    def _():
        _run_copies(next_runs_ref, from_sorted(1 - cur), lambda cp: cp.start())

    _run_copies(runs_ref, from_sorted(cur), lambda cp: cp.wait())

    def mix(tok, carry):
        y = jnp.zeros((ROW_TILE, V7X_LANES), F32)
        for k in range(TOP_K):
            idx = TOP_K * tok + k
            y = y + gate_ref[0, 0, idx] * local[cur, _row_slice(lpos_ref[0, 0, idx], 1), :]
        acc[_row_slice(tok, 1), :] = y
        return carry

    lax.fori_loop(0, tm, mix, 0, unroll=8)
    x2 = x1_ref[...] + _load_row_tiles(acc, tm)
    if final:
        ms = jnp.mean(x2 * x2, axis=-1, keepdims=True)
        x2 = x2 * lax.rsqrt(ms + NORM_EPS) * g_ref[...]
    o_ref[...] = x2


def _combine(x1, yb, lpos, gates, runs, *, tm, final_g=None):
    n_tok, d = x1.shape
    nt = n_tok // tm
    final = final_g is not None
    in_specs = [pl.BlockSpec((1, 1, TOP_K * tm), lambda i: (i, 0, 0), memory_space=pltpu.SMEM),
                pl.BlockSpec((1, 1, TOP_K * tm), lambda i: (i, 0, 0), memory_space=pltpu.SMEM),
                pl.BlockSpec((1, 3, N_EXPERTS), lambda i: (i, 0, 0), memory_space=pltpu.SMEM),
                pl.BlockSpec((1, 3, N_EXPERTS), lambda i: (jnp.minimum(i + 1, nt - 1), 0, 0),
                             memory_space=pltpu.SMEM),
                pl.BlockSpec((tm, d), lambda i: (i, 0)),
                pl.BlockSpec(memory_space=pl.ANY)]
    args = [lpos, gates, runs, runs, x1, yb]
    if final:
        in_specs.append(pl.BlockSpec((1, d), lambda i: (0, 0)))
        args.append(final_g.reshape(1, d))
    return pl.pallas_call(
        functools.partial(_combine_kernel, final=final),
        grid=(nt,),
        in_specs=in_specs,
        out_specs=pl.BlockSpec((tm, d), lambda i: (i, 0)),
        out_shape=jax.ShapeDtypeStruct((n_tok, d), F32),
        scratch_shapes=[pltpu.VMEM((2, TOP_K * tm * ROW_TILE, V7X_LANES), F32),
                        pltpu.VMEM((tm * ROW_TILE, V7X_LANES), F32),
                        pltpu.SemaphoreType.DMA((2,))],
        compiler_params=_cparams(("arbitrary",)),
        name="moe_combine_final" if final else "moe_combine",
    )(*args)


def _moe_plan(rt, n_tok, tm):
    nt = n_tok // tm
    expert = rt[:, 0:TOP_K].astype(jnp.int32)
    gate = rt[:, TOP_K:2 * TOP_K]
    pos = rt[:, 2 * TOP_K:3 * TOP_K].astype(jnp.int32)
    is_e = (expert[:, :, None] == jnp.arange(N_EXPERTS, dtype=jnp.int32)).reshape(nt, tm, TOP_K, N_EXPERTS)
    cnt_tile = jnp.sum(is_e, axis=(1, 2), dtype=jnp.int32)
    seen_before = jnp.cumsum(cnt_tile, axis=0) - cnt_tile
    local_start = jnp.cumsum(cnt_tile, axis=1) - cnt_tile
    counts = jnp.sum(cnt_tile, axis=0)
    shift = local_start - seen_before
    lpos = pos.reshape(nt, tm, TOP_K) + jnp.sum(jnp.where(is_e, shift[:, None, None, :], 0), axis=-1)
    nblk = (counts + MOE_BLOCK - 1) // MOE_BLOCK
    bend = jnp.cumsum(nblk)
    seg_start = (bend - nblk) * MOE_BLOCK
    sorted_start = seg_start[None, :] + seen_before
    runs = jnp.stack([cnt_tile, local_start, sorted_start], axis=1)
    seg = jnp.stack([counts, seg_start, jnp.full((N_EXPERTS,), bend[-1] * MOE_BLOCK, jnp.int32)])
    n_steps = n_tok * TOP_K // MOE_BLOCK + N_EXPERTS
    step = jnp.arange(n_steps, dtype=jnp.int32)
    bexp = jnp.sum(jnp.minimum(step, bend[-1] - 1)[:, None] >= bend[None, :], axis=1).astype(jnp.int32)
    bfirst = jnp.concatenate([jnp.ones((1,), jnp.int32), (bexp[1:] != bexp[:-1]).astype(jnp.int32)])
    wslot = (jnp.cumsum(bfirst) - 1) % FFN_W_SLOTS
    switch_at = jnp.where(bfirst != 0, step, n_steps)
    next_switch = jnp.concatenate([lax.cummin(switch_at[::-1])[::-1][1:], jnp.full((1,), n_steps, jnp.int32)])
    nexp = jnp.sum(jnp.where(next_switch[:, None] == step[None, :], bexp[None, :] + 1, 0), axis=1) - 1
    sched = dict(nblk=bend[-1:].astype(jnp.int32), bexp=bexp, bfirst=bfirst,
                 nexp=nexp.astype(jnp.int32), wslot=wslot.astype(jnp.int32))
    return dict(seg=seg.astype(jnp.int32), n_rows=n_steps * MOE_BLOCK, sched=sched,
                lpos=lpos.reshape(nt, 1, TOP_K * tm), gates=gate.reshape(nt, 1, TOP_K * tm), runs=runs)


def _moe_layer(x1, h2_tiles, rt, w1, w3, w2, layer, *, final_g=None):
    n_tok = x1.shape[0]
    tm = 1024
    plan = _moe_plan(rt, n_tok, tm)
    xs = _dispatch(h2_tiles, plan["seg"], plan["lpos"], plan["runs"], plan["n_rows"], tm=tm)
    yb = _moe_ffn(xs, plan["sched"], w1, w3, w2, layer)
    return _combine(x1, yb, plan["lpos"], plan["gates"], plan["runs"], tm=tm, final_g=final_g)


def kernel(x, mem, mem_norm, final_norm, norm_mix, norm_ffn, w_mem_kv, w_out, na_w_in, na_rpb, gla_w_in,
           gla_gate_up, gla_gate_bias, gla_out_norm, moe_w_group, moe_b_group, moe_w_router, moe_b_router,
           moe_w1, moe_w3, moe_w2):
    b, t, d = x.shape
    n = b * t
    depth = norm_mix.shape[0]
    n_mem = mem.shape[1]
    xf = x.reshape(n, d)
    for i in range(depth):
        j = i // 2
        mkv = _norm_matmul(mem.reshape(b * n_mem, d), mem_norm, w_mem_kv[i].astype(BF16),
                           tm=256, out_dtype=F32, name="mem_kv_proj").reshape(b, n_mem, 2 * MEM_DIM)
        kbd, vbd = _memory_kv_blockdiag(mkv)
        if i % 2 == 0:
            u = _norm_matmul(xf, norm_mix[i], na_w_in[j].astype(BF16), tm=512, out_dtype=BF16,
                             name="na_in_proj").reshape(b, t, -1)
            mix = _na_attention(u, na_rpb[j])
            mq_off = NA_MQ_OFF
            w_o = w_out[i].astype(BF16)
            w_o = (w_o[:MIX_DIM], w_o[MIX_DIM:])
        else:
            w_pad, gu, gb, onorm = _gla_weights(gla_w_in[j], gla_gate_up[j], gla_gate_bias[j], gla_out_norm[j])
            u = _norm_matmul(xf, norm_mix[i], w_pad, tm=512, out_dtype=BF16, name="gla_in_proj").reshape(b, t, -1)
            ofwd = _gla_direction(u, gu[0], gb[0], reverse=False, tb=512)
            mix = _gla_direction(u, gu[1], gb[1], reverse=True, tb=512, ofwd=ofwd, onorm=onorm)
            mq_off = GLA_MQ_OFF
            w_mix = w_out[i][:MIX_DIM].reshape(GLA_HEADS, GLA_DV, d)
            w_mix = jnp.pad(w_mix, ((0, 0), (0, GLA_DV_PAD - GLA_DV), (0, 0))).reshape(GLA_MIX_PAD, d)
            w_o = (w_mix.astype(BF16), w_out[i][MIX_DIM:].astype(BF16))
        w_rt = jnp.pad(jnp.concatenate([moe_w_group[i], moe_w_router[i]], axis=1),
                       ((0, 0), (0, V7X_LANES - N_GROUPS - N_EXPERTS))).astype(F32)
        w_rt = jnp.concatenate(_split_bf16(w_rt, 2), axis=1)
        b_rt = jnp.pad(jnp.concatenate([moe_b_group[i], moe_b_router[i]]),
                       (0, V7X_LANES - N_GROUPS - N_EXPERTS)).reshape(1, V7X_LANES)
        x1, h2_tiles, rt = _post_mixer(xf.reshape(b, t, d), mix, u, mq_off, kbd, vbd, w_o, norm_ffn[i],
                                       w_rt, b_rt.astype(F32), tm=POST_TILE)
        xf = _moe_layer(x1.reshape(n, d), h2_tiles, rt.reshape(n, V7X_LANES),
                        moe_w1, moe_w3, moe_w2, i,
                        final_g=final_norm if i == depth - 1 else None)
    return xf.reshape(b, t, d)
```

```python
import functools

import numpy as np
import jax
import jax.numpy as jnp
from jax import lax
from jax.experimental import pallas as pl
from jax.experimental.pallas import tpu as pltpu

F32 = jnp.float32
BF16 = jnp.bfloat16

D_MODEL = 1024
GRID_W = 64
HEAD_DIM = 64
NORM_EPS = 1e-6
MEM_HEADS = 4
MEM_DIM = MEM_HEADS * HEAD_DIM
MIX_DIM = D_MODEL - MEM_DIM
NA_HEADS = MIX_DIM // HEAD_DIM
NA_WIN_H = 8
NA_WIN_W = 16
GLA_HEADS = 4
GLA_DK = MIX_DIM // 2 // GLA_HEADS
GLA_DV = MIX_DIM // GLA_HEADS
GLA_GATE_RANK = 16
GLA_TAU = 16.0
GLA_CHUNK = 64
N_GROUPS = 4
EXPERTS_PER_GROUP = 8
N_EXPERTS = N_GROUPS * EXPERTS_PER_GROUP
TOP_K = 2
EXPERT_FF = 512
MOE_BLOCK = 256

V7X_LANES = 128
V7X_MXU_DIM = 256
V7X_VMEM_LIMIT_BYTES = 56 * 1024 * 1024

GLA_DK_PAD = 128
GLA_DV_PAD = 256
GLA_Q_OFF = 0
GLA_K_OFF = GLA_Q_OFF + GLA_HEADS * GLA_DK_PAD
GLA_V_OFF = GLA_K_OFF + GLA_HEADS * GLA_DK_PAD
GLA_G_OFF = GLA_V_OFF + GLA_HEADS * GLA_DV_PAD
GLA_R_OFF = GLA_G_OFF + GLA_HEADS * GLA_DV_PAD
GLA_R_PAD = 256
GLA_MQ_OFF = GLA_R_OFF + GLA_R_PAD
GLA_IN_PAD = GLA_MQ_OFF + MEM_DIM
GLA_MIX_PAD = GLA_HEADS * GLA_DV_PAD
NA_MQ_OFF = 3 * MIX_DIM
NA_ROWS_PER_STEP = 4
POST_TILE = 512
POST_SUBTILES = 2


def _cparams(semantics):
    return pltpu.CompilerParams(dimension_semantics=semantics, vmem_limit_bytes=V7X_VMEM_LIMIT_BYTES)


def _split_bf16(x, n_pieces):
    pieces = []
    rest = x
    for _ in range(n_pieces - 1):
        c = rest * 65537.0
        hi = c - (c - rest)
        pieces.append(hi.astype(BF16))
        rest = rest - hi
    pieces.append(rest.astype(BF16))
    return pieces


ROW_TILE = D_MODEL // V7X_LANES


def _store_row_tiles(ref, val):
    rows = val.shape[0]
    for s in range(ROW_TILE):
        ref[pl.ds(s, rows, stride=ROW_TILE), :] = val[:, s * V7X_LANES:(s + 1) * V7X_LANES]


def _load_row_tiles(ref, rows):
    return jnp.concatenate([ref[pl.ds(s, rows, stride=ROW_TILE), :] for s in range(ROW_TILE)], axis=1)


def _norm_matmul_kernel(x_ref, g_ref, w_ref, o_ref, *, col_chunk):
    x = x_ref[...]
    ms = jnp.mean(x * x, axis=-1, keepdims=True)
    y = (x * lax.rsqrt(ms + NORM_EPS) * g_ref[...]).astype(BF16)
    n_out = o_ref.shape[1]
    for c in range(0, n_out, col_chunk):
        o_ref[:, c:c + col_chunk] = jnp.dot(
            y, w_ref[:, c:c + col_chunk], preferred_element_type=F32).astype(o_ref.dtype)


def _norm_matmul(x, g, w, *, tm, out_dtype, name):
    n, d = x.shape
    n_out = w.shape[1]
    col_chunk = 512 if n_out % 512 == 0 else n_out
    return pl.pallas_call(
        functools.partial(_norm_matmul_kernel, col_chunk=col_chunk),
        grid=(n // tm,),
        in_specs=[pl.BlockSpec((tm, d), lambda i: (i, 0)),
                  pl.BlockSpec((1, d), lambda i: (0, 0)),
                  pl.BlockSpec((d, n_out), lambda i: (0, 0))],
        out_specs=pl.BlockSpec((tm, n_out), lambda i: (i, 0)),
        out_shape=jax.ShapeDtypeStruct((n, n_out), out_dtype),
        compiler_params=_cparams(("parallel",)),
        name=name,
    )(x, g.reshape(1, d), w)


def _na_kernel(q_ref, k_ref, v_ref, *rest):
    *bias_refs, o_ref = rest
    rows = k_ref.shape[1] // GRID_W
    n_keys = NA_WIN_H * GRID_W
    heads_per_slab = V7X_MXU_DIM // HEAD_DIM
    scale = HEAD_DIM ** -0.5
    lane_head = lax.broadcasted_iota(jnp.int32, (GRID_W, V7X_MXU_DIM), 1) // HEAD_DIM
    slabs = [slice(s * V7X_MXU_DIM, (s + 1) * V7X_MXU_DIM) for s in range(MIX_DIM // V7X_MXU_DIM)]

    def window_start(rr):
        r = pl.program_id(1) * len(bias_refs) + rr
        rs = jnp.clip(r - NA_WIN_H // 2, 0, rows - NA_WIN_H)
        return pl.multiple_of(rs * GRID_W, GRID_W)

    def scores(rr, cs):
        qq = q_ref[0, rr * GRID_W:(rr + 1) * GRID_W, cs] * scale
        kw = k_ref[0, pl.ds(window_start(rr), n_keys), cs]
        lhs = jnp.concatenate(
            [jnp.where(lane_head == i, qq, jnp.zeros_like(qq)) for i in range(heads_per_slab)], axis=0)
        sc = lax.dot_general(lhs, kw, (((1,), (1,)), ((), ())), preferred_element_type=F32)
        return sc + bias_refs[rr][0, cs, :]

    units = [(rr, cs) for rr in range(len(bias_refs)) for cs in slabs]
    nxt = scores(*units[0])
    for ui, (rr, cs) in enumerate(units):
        sc = nxt
        if ui + 1 < len(units):
            nxt = scores(*units[ui + 1])
        vw = v_ref[0, pl.ds(window_start(rr), n_keys), cs]
        m = jnp.max(sc, axis=-1, keepdims=True)
        p = jnp.exp(sc - m)
        l = jnp.sum(p, axis=-1, keepdims=True)
        o = jnp.dot(p.astype(BF16), vw, preferred_element_type=F32)
        o = o * (1.0 / l)
        acc = jnp.zeros((GRID_W, V7X_MXU_DIM), F32)
        for i in range(heads_per_slab):
            acc = acc + jnp.where(lane_head == i, o[i * GRID_W:(i + 1) * GRID_W], 0.0)
        o_ref[0, rr * GRID_W:(rr + 1) * GRID_W, cs] = acc.astype(o_ref.dtype)


def _na_bias_table(rpb):
    qc = np.arange(GRID_W)[:, None]
    kc = np.arange(GRID_W)[None, :]
    cstart = np.clip(qc - NA_WIN_W // 2, 0, GRID_W - NA_WIN_W)
    col_in = (kc >= cstart) & (kc < cstart + NA_WIN_W)
    dcol = np.clip(kc - qc, 1 - NA_WIN_W, NA_WIN_W - 1) + NA_WIN_W - 1
    pick = jnp.asarray(dcol[None] == np.arange(2 * NA_WIN_W - 1)[:, None, None], F32)
    per_row = jnp.einsum('hrd,dqk->hrqk', rpb.astype(F32), pick, precision=lax.Precision.HIGHEST)
    per_row = jnp.where(col_in[None, None], per_row, -jnp.inf)
    tbl = jnp.stack([per_row[:, o:o + NA_WIN_H] for o in range(NA_WIN_H)])
    return tbl.transpose(0, 1, 3, 2, 4).reshape(NA_WIN_H, NA_HEADS * GRID_W, NA_WIN_H * GRID_W)


def _na_attention(u, rpb):
    b, t, _ = u.shape
    rows = t // GRID_W
    bias = _na_bias_table(rpb)

    def bias_spec(rr):
        def bias_idx(bi, g):
            r = g * NA_ROWS_PER_STEP + rr
            return (jnp.clip(r - NA_WIN_H // 2, 0, rows - NA_WIN_H) - r + NA_WIN_H - 1, 0, 0)
        return pl.BlockSpec((1, NA_HEADS * GRID_W, NA_WIN_H * GRID_W), bias_idx)

    qt = NA_ROWS_PER_STEP * GRID_W
    return pl.pallas_call(
        _na_kernel,
        grid=(b, rows // NA_ROWS_PER_STEP),
        in_specs=[pl.BlockSpec((1, qt, MIX_DIM), lambda bi, g: (bi, g, 0)),
                  pl.BlockSpec((1, t, MIX_DIM), lambda bi, g: (bi, 0, 1)),
                  pl.BlockSpec((1, t, MIX_DIM), lambda bi, g: (bi, 0, 2))]
        + [bias_spec(rr) for rr in range(NA_ROWS_PER_STEP)],
        out_specs=pl.BlockSpec((1, qt, MIX_DIM), lambda bi, g: (bi, g, 0)),
        out_shape=jax.ShapeDtypeStruct((b, t, MIX_DIM), BF16),
        compiler_params=_cparams(("parallel", "arbitrary")),
        name="na_attention",
    )(u, u, u, *([bias] * NA_ROWS_PER_STEP))


def _gla_kernel(*refs, reverse, final):
    if final:
        (q_ref, k_ref, v_ref, r_ref, gu_ref, gb_ref, ofwd_ref, g_ref, onorm_ref, o_ref, st_ref) = refs
    else:
        (q_ref, k_ref, v_ref, r_ref, gu_ref, gb_ref, o_ref, st_ref) = refs
    c = GLA_CHUNK

    @pl.when(pl.program_id(1) == 0)
    def _():
        st_ref[...] = jnp.zeros_like(st_ref)

    z = jnp.dot(r_ref[0], gu_ref[...], preferred_element_type=F32) + gb_ref[...]
    la = (jnp.minimum(z, 0.0) - jnp.log(1.0 + jnp.exp(-jnp.abs(z)))) * (1.0 / GLA_TAU)
    ri = lax.broadcasted_iota(jnp.int32, (c, c), 0)
    ci = lax.broadcasted_iota(jnp.int32, (c, c), 1)
    tri = (ci >= ri) if reverse else (ci <= ri)
    trib = jnp.where(tri, 1.0, 0.0).astype(BF16)
    la3 = jnp.concatenate(_split_bf16(la, 3), axis=1)
    gw = la.shape[1]
    mid = c // 2 if reverse else c // 2 - 1
    last = 0 if reverse else c - 1
    scale = GLA_DK ** -0.5
    n_chunks = q_ref.shape[1] // c
    order = range(n_chunks - 1, -1, -1) if reverse else range(n_chunks)
    state = [st_ref[h] for h in range(GLA_HEADS)]
    heads = range(GLA_HEADS)
    hsl = [slice(h * GLA_DK_PAD, (h + 1) * GLA_DK_PAD) for h in heads]
    vsl = [slice(h * GLA_DV_PAD, (h + 1) * GLA_DV_PAD) for h in heads]
    nt_dims = (((1,), (1,)), ((), ()))
    tn_dims = (((0,), (0,)), ((), ()))

    def prepare(ch):
        sl = slice(ch * c, (ch + 1) * c)
        b3 = jnp.dot(trib, la3[sl], preferred_element_type=F32)
        bcum = b3[:, :gw] + b3[:, gw:2 * gw] + b3[:, 2 * gw:]
        b_mid = bcum[mid:mid + 1]
        b_last = bcum[last:last + 1]
        qc = q_ref[0, sl, :].astype(F32) * scale
        kc = k_ref[0, sl, :].astype(F32)
        return dict(sl=sl, vc=v_ref[0, sl, :],
                    qe=(qc * jnp.exp(bcum - b_mid)).astype(BF16), ke=(kc * jnp.exp(b_mid - bcum)).astype(BF16),
                    qs=(qc * jnp.exp(bcum)).astype(BF16), ks=(kc * jnp.exp(b_last - bcum)).astype(BF16),
                    dec=jnp.exp(b_last))

    order = list(order)
    nxt = prepare(order[0])
    for pos_in_step, ch in enumerate(order):
        cur = nxt
        if pos_in_step + 1 < len(order):
            nxt = prepare(order[pos_in_step + 1])
        sl, vc, dec = cur["sl"], cur["vc"], cur["dec"]
        scores = [lax.dot_general(cur["qe"][:, hsl[h]], cur["ke"][:, hsl[h]], nt_dims, preferred_element_type=F32)
                  for h in heads]
        kv_t = [lax.dot_general(vc[:, vsl[h]], cur["ks"][:, hsl[h]], tn_dims, preferred_element_type=F32)
                for h in heads]
        o_inter = [lax.dot_general(cur["qs"][:, hsl[h]], state[h].astype(BF16), nt_dims, preferred_element_type=F32)
                   for h in heads]
        masked = [jnp.where(tri, scores[h], 0.0).astype(BF16) for h in heads]
        o_intra = [jnp.dot(masked[h], vc[:, vsl[h]], preferred_element_type=F32) for h in heads]
        for h in heads:
            hs, vs = hsl[h], vsl[h]
            o = o_intra[h] + o_inter[h]
            state[h] = state[h] * dec[:, hs] + kv_t[h]
            if final:
                tot = ofwd_ref[0, sl, vs] + o
                ms = jnp.sum(tot * tot, axis=-1, keepdims=True) * (1.0 / GLA_DV)
                y = tot * lax.rsqrt(ms + NORM_EPS) * onorm_ref[:, vs]
                g = g_ref[0, sl, vs].astype(F32)
                o_ref[0, sl, vs] = (y * (g / (1.0 + jnp.exp(-g)))).astype(o_ref.dtype)
            else:
                o_ref[0, sl, vs] = o
    for h in range(GLA_HEADS):
        st_ref[h] = state[h]


def _gla_direction(u, gu, gb, *, reverse, tb, ofwd=None, onorm=None):
    b, t, _ = u.shape
    nt = t // tb
    final = ofwd is not None
    tix = (lambda ti: nt - 1 - ti) if reverse else (lambda ti: ti)
    qw = GLA_HEADS * GLA_DK_PAD
    vw = GLA_HEADS * GLA_DV_PAD
    in_specs = [pl.BlockSpec((1, tb, qw), lambda bi, ti: (bi, tix(ti), GLA_Q_OFF // qw)),
                pl.BlockSpec((1, tb, qw), lambda bi, ti: (bi, tix(ti), GLA_K_OFF // qw)),
                pl.BlockSpec((1, tb, vw), lambda bi, ti: (bi, tix(ti), GLA_V_OFF // vw)),
                pl.BlockSpec((1, tb, GLA_R_PAD), lambda bi, ti: (bi, tix(ti), GLA_R_OFF // GLA_R_PAD)),
                pl.BlockSpec((GLA_R_PAD, qw), lambda bi, ti: (0, 0)),
                pl.BlockSpec((1, qw), lambda bi, ti: (0, 0))]
    args = [u, u, u, u, gu, gb]
    if final:
        in_specs += [pl.BlockSpec((1, tb, vw), lambda bi, ti: (bi, tix(ti), 0)),
                     pl.BlockSpec((1, tb, vw), lambda bi, ti: (bi, tix(ti), GLA_G_OFF // vw)),
                     pl.BlockSpec((1, vw), lambda bi, ti: (0, 0))]
        args += [ofwd, u, onorm]
    return pl.pallas_call(
        functools.partial(_gla_kernel, reverse=reverse, final=final),
        grid=(b, nt),
        in_specs=in_specs,
        out_specs=pl.BlockSpec((1, tb, vw), lambda bi, ti: (bi, tix(ti), 0)),
        out_shape=jax.ShapeDtypeStruct((b, t, vw), BF16 if final else F32),
        scratch_shapes=[pltpu.VMEM((GLA_HEADS, GLA_DV_PAD, GLA_DK_PAD), F32)],
        compiler_params=_cparams(("parallel", "arbitrary")),
        name="gla_bwd_final" if final else "gla_fwd",
    )(*args)


def _gla_weights(w_in, gate_up, gate_bias, out_norm):
    d = w_in.shape[0]
    kd = GLA_HEADS * GLA_DK
    wq, wk, wv, wg, wr, wmq = jnp.split(
        w_in, np.cumsum([kd, kd, MIX_DIM, MIX_DIM, 2 * GLA_GATE_RANK]), axis=1)

    def pad_heads(w, dh, dh_pad):
        w = w.reshape(w.shape[0], GLA_HEADS, dh)
        w = jnp.pad(w, ((0, 0), (0, 0), (0, dh_pad - dh)))
        return w.reshape(w.shape[0], GLA_HEADS * dh_pad)

    w_pad = jnp.concatenate([
        pad_heads(wq, GLA_DK, GLA_DK_PAD), pad_heads(wk, GLA_DK, GLA_DK_PAD),
        pad_heads(wv, GLA_DV, GLA_DV_PAD), pad_heads(wg, GLA_DV, GLA_DV_PAD),
        jnp.pad(wr, ((0, 0), (0, GLA_R_PAD - 2 * GLA_GATE_RANK))), wmq], axis=1)
    gu = []
    for di in range(2):
        up = pad_heads(gate_up[di], GLA_DK, GLA_DK_PAD)
        gu.append(jnp.pad(up, ((di * GLA_GATE_RANK, GLA_R_PAD - (di + 1) * GLA_GATE_RANK), (0, 0))))
    gb = [pad_heads(gate_bias[di][None, :], GLA_DK, GLA_DK_PAD) for di in range(2)]
    onorm = pad_heads(out_norm[None, :].repeat(GLA_HEADS, 0).reshape(1, MIX_DIM), GLA_DV, GLA_DV_PAD)
    return w_pad.astype(BF16), [g.astype(BF16) for g in gu], [x.astype(F32) for x in gb], onorm.astype(F32)


def _post_kernel(x_ref, mix_ref, mq_ref, kbd_ref, vbd_ref, wom_ref, woa_ref, g_ref, wrt_ref, brt_ref,
                 x1_ref, h2_ref, rt_ref, base_ref):
    @pl.when((pl.program_id(0) == 0) & (pl.program_id(1) == 0))
    def _():
        base_ref[...] = jnp.zeros_like(base_ref)

    tm = x_ref.shape[1]
    hm = tm // POST_SUBTILES
    groups = [slice(i * hm, (i + 1) * hm) for i in range(POST_SUBTILES)]
    n_mem = kbd_ref.shape[2] // MEM_HEADS
    lane = lax.broadcasted_iota(jnp.int32, (hm, V7X_LANES), 1).astype(F32)
    neg = -jnp.inf
    big = 1e9

    s = [jnp.dot(mq_ref[0, g, :], kbd_ref[0], preferred_element_type=F32) * (HEAD_DIM ** -0.5)
         for g in groups]
    mixed = [jnp.dot(mix_ref[0, g, :], wom_ref[...], preferred_element_type=F32) for g in groups]
    probs = []
    for sg in s:
        ps = []
        for h in range(MEM_HEADS):
            seg = sg[:, h * n_mem:(h + 1) * n_mem]
            e = jnp.exp(seg - jnp.max(seg, axis=-1, keepdims=True))
            ps.append((e * (1.0 / jnp.sum(e, axis=-1, keepdims=True))).astype(BF16))
        probs.append(jnp.concatenate(ps, axis=1))
    att = [jnp.dot(p, vbd_ref[0], preferred_element_type=F32).astype(BF16) for p in probs]
    logits = []
    for i, g in enumerate(groups):
        x1 = x_ref[0, g, :] + mixed[i] + jnp.dot(att[i], woa_ref[...], preferred_element_type=F32)
        x1_ref[0, g, :] = x1
        ms = jnp.mean(x1 * x1, axis=-1, keepdims=True)
        h2 = x1 * lax.rsqrt(ms + NORM_EPS) * g_ref[...]
        _store_row_tiles(h2_ref.at[pl.ds(i * hm * ROW_TILE, hm * ROW_TILE)], h2)
        h_hi, h_lo = _split_bf16(h2, 2)
        hw = jnp.dot(h_hi, wrt_ref[...], preferred_element_type=F32)
        logits.append(hw[:, :V7X_LANES] + hw[:, V7X_LANES:]
                      + jnp.dot(h_lo, wrt_ref[:, :V7X_LANES], preferred_element_type=F32) + brt_ref[...])
    routed = []
    for lg in logits:
        gl = jnp.where(lane < N_GROUPS, lg, neg)
        gm = jnp.max(gl, axis=-1, keepdims=True)
        g_gate = 1.0 / jnp.sum(jnp.exp(gl - gm), axis=-1, keepdims=True)
        g_idx = jnp.min(jnp.where(gl == gm, lane, big), axis=-1, keepdims=True)
        lo = N_GROUPS + EXPERTS_PER_GROUP * g_idx
        v1 = jnp.where((lane >= lo) & (lane < lo + EXPERTS_PER_GROUP), lg, neg)
        m1 = jnp.max(v1, axis=-1, keepdims=True)
        i1 = jnp.min(jnp.where(v1 == m1, lane, big), axis=-1, keepdims=True)
        v2 = jnp.where(lane == i1, neg, v1)
        m2 = jnp.max(v2, axis=-1, keepdims=True)
        i2 = jnp.min(jnp.where(v2 == m2, lane, big), axis=-1, keepdims=True)
        t = jnp.exp(m2 - m1)
        routed.append((i1 - N_GROUPS, i2 - N_GROUPS, g_gate / (1.0 + t), g_gate * t / (1.0 + t)))
    ri = lax.broadcasted_iota(jnp.int32, (hm, hm), 0)
    ci = lax.broadcasted_iota(jnp.int32, (hm, hm), 1)
    before = jnp.where(ci < ri, 1.0, 0.0).astype(BF16)
    onehots = [jnp.where(lane == e1, 1.0, 0.0) + jnp.where(lane == e2, 1.0, 0.0)
               for e1, e2, _, _ in routed]
    earlier = [jnp.dot(before, oh.astype(BF16), preferred_element_type=F32) for oh in onehots]
    for i, g in enumerate(groups):
        e1, e2, w1, w2 = routed[i]
        ahead = earlier[i] + base_ref[...]
        pos1 = jnp.sum(jnp.where(lane == e1, ahead, 0.0), axis=-1, keepdims=True)
        pos2 = jnp.sum(jnp.where(lane == e2, ahead, 0.0), axis=-1, keepdims=True)
        base_ref[...] = base_ref[...] + jnp.sum(onehots[i], axis=0, keepdims=True)
        rt_ref[0, g, :] = jnp.where(
            lane == 0, e1, jnp.where(lane == 1, e2, jnp.where(lane == 2, w1, jnp.where(
                lane == 3, w2, jnp.where(lane == 4, pos1, jnp.where(lane == 5, pos2, 0.0))))))


def _post_mixer(x, mix, u, mq_off, kbd, vbd, w_out, g, w_rt, b_rt, *, tm):
    b, t, d = x.shape
    mixw = mix.shape[2]
    wom = w_out[0]
    woa = w_out[1]
    return pl.pallas_call(
        _post_kernel,
        grid=(b, t // tm),
        in_specs=[pl.BlockSpec((1, tm, d), lambda bi, ti: (bi, ti, 0)),
                  pl.BlockSpec((1, tm, mixw), lambda bi, ti: (bi, ti, 0)),
                  pl.BlockSpec((1, tm, MEM_DIM), lambda bi, ti: (bi, ti, mq_off // MEM_DIM)),
                  pl.BlockSpec((1,) + kbd.shape[1:], lambda bi, ti: (bi, 0, 0)),
                  pl.BlockSpec((1,) + vbd.shape[1:], lambda bi, ti: (bi, 0, 0)),
                  pl.BlockSpec(wom.shape, lambda bi, ti: (0, 0)),
                  pl.BlockSpec(woa.shape, lambda bi, ti: (0, 0)),
                  pl.BlockSpec((1, d), lambda bi, ti: (0, 0)),
                  pl.BlockSpec(w_rt.shape, lambda bi, ti: (0, 0)),
                  pl.BlockSpec((1, V7X_LANES), lambda bi, ti: (0, 0))],
        out_specs=[pl.BlockSpec((1, tm, d), lambda bi, ti: (bi, ti, 0)),
                   pl.BlockSpec((tm * ROW_TILE, V7X_LANES), lambda bi, ti: (bi * (t // tm) + ti, 0)),
                   pl.BlockSpec((1, tm, V7X_LANES), lambda bi, ti: (bi, ti, 0))],
        out_shape=[jax.ShapeDtypeStruct((b, t, d), F32),
                   jax.ShapeDtypeStruct((b * t * ROW_TILE, V7X_LANES), F32),
                   jax.ShapeDtypeStruct((b, t, V7X_LANES), F32)],
        scratch_shapes=[pltpu.VMEM((1, V7X_LANES), F32)],
        compiler_params=_cparams(("arbitrary", "arbitrary")),
        name="post_mixer",
    )(x, mix, u, kbd, vbd, wom, woa, g.reshape(1, d), w_rt, b_rt)


def _memory_kv_blockdiag(mkv):
    k, v = jnp.split(mkv, 2, axis=-1)
    head_of = np.arange(MEM_DIM) // HEAD_DIM
    sel = jnp.asarray(head_of[None, :] == np.arange(MEM_HEADS)[:, None], F32)
    kbd = jnp.einsum('bmd,hd->bdhm', k, sel).reshape(k.shape[0], MEM_DIM, -1)
    vbd = jnp.einsum('bmd,hd->bhmd', v, sel).reshape(v.shape[0], -1, MEM_DIM)
    return kbd.astype(BF16), vbd.astype(BF16)


FFN_X_SLOTS = 4
FFN_Y_SLOTS = 2
FFN_W_SLOTS = 2


def _moe_kernel(nblk_ref, bexp_ref, bfirst_ref, nexp_ref, wslot_ref, w1_ref, w3_ref, w2_ref, xs_ref, y_ref,
                wf1, wf3, wf2, wb1, wb3, wb2, xbuf, ybuf, wsem, xsem, ysem, *, layer):
    del xs_ref
    n_blk = nblk_ref[0]

    def rows_of(blk):
        return _row_slice(blk * MOE_BLOCK, MOE_BLOCK)

    def fetch(blk):
        slot = blk % FFN_X_SLOTS
        return pltpu.make_async_copy(y_ref.at[rows_of(blk)], xbuf.at[slot], xsem.at[slot])

    def writeback(blk):
        slot = blk % FFN_Y_SLOTS
        return pltpu.make_async_copy(ybuf.at[slot], y_ref.at[rows_of(blk)], ysem.at[slot])

    def weight_copies(expert, slot):
        return [pltpu.make_async_copy(src.at[layer, expert], dst.at[slot], wsem.at[slot, j])
                for j, (src, dst) in enumerate(((w1_ref, wf1), (w3_ref, wf3), (w2_ref, wf2)))]

    for cp in weight_copies(bexp_ref[0], 0):
        cp.start()
    for ahead in range(FFN_X_SLOTS - 1):
        @pl.when(ahead < n_blk)
        def _(ahead=ahead):
            fetch(ahead).start()

    def per_block(blk, carry):
        @pl.when(blk + FFN_X_SLOTS - 1 < n_blk)
        def _():
            fetch(blk + FFN_X_SLOTS - 1).start()

        @pl.when(bfirst_ref[blk] != 0)
        def _():
            slot = wslot_ref[blk]

            @pl.when(nexp_ref[blk] >= 0)
            def _():
                for cp in weight_copies(nexp_ref[blk], 1 - slot):
                    cp.start()

            for cp in weight_copies(bexp_ref[blk], slot):
                cp.wait()
            wb1[...] = wf1[slot].astype(BF16)
            wb3[...] = wf3[slot].astype(BF16)
            wb2[...] = wf2[slot].astype(BF16)

        fetch(blk).wait()

        @pl.when(blk >= FFN_Y_SLOTS)
        def _():
            writeback(blk - FFN_Y_SLOTS).wait()

        xb = _load_row_tiles(xbuf.at[blk % FFN_X_SLOTS], MOE_BLOCK).astype(BF16)
        half = wb1.shape[1] // 2
        up = [(jnp.dot(xb, wb1[:, hf * half:(hf + 1) * half], preferred_element_type=F32),
               jnp.dot(xb, wb3[:, hf * half:(hf + 1) * half], preferred_element_type=F32)) for hf in range(2)]
        y = None
        for hf, (a, c) in enumerate(up):
            hid = (a / (1.0 + jnp.exp(-a)) * c).astype(BF16)
            part = jnp.dot(hid, wb2[hf * half:(hf + 1) * half, :], preferred_element_type=F32)
            y = part if y is None else y + part
        _store_row_tiles(ybuf.at[blk % FFN_Y_SLOTS], y)
        writeback(blk).start()
        return carry

    lax.fori_loop(0, n_blk, per_block, 0)
    for back in range(FFN_Y_SLOTS, 0, -1):
        @pl.when(n_blk >= back)
        def _(back=back):
            writeback(n_blk - back).wait()


def _moe_ffn(xs, sched, w1, w3, w2, layer):
    d = w1.shape[2]
    ff = w1.shape[3]
    blk_rows = MOE_BLOCK * ROW_TILE
    any_spec = pl.BlockSpec(memory_space=pl.ANY)
    grid_spec = pltpu.PrefetchScalarGridSpec(
        num_scalar_prefetch=5,
        grid=(1,),
        in_specs=[any_spec, any_spec, any_spec, any_spec],
        out_specs=any_spec,
        scratch_shapes=[pltpu.VMEM((FFN_W_SLOTS, d, ff), F32), pltpu.VMEM((FFN_W_SLOTS, d, ff), F32),
                        pltpu.VMEM((FFN_W_SLOTS, ff, d), F32),
                        pltpu.VMEM((d, ff), BF16), pltpu.VMEM((d, ff), BF16), pltpu.VMEM((ff, d), BF16),
                        pltpu.VMEM((FFN_X_SLOTS, blk_rows, V7X_LANES), F32),
                        pltpu.VMEM((FFN_Y_SLOTS, blk_rows, V7X_LANES), F32),
                        pltpu.SemaphoreType.DMA((FFN_W_SLOTS, 3)),
                        pltpu.SemaphoreType.DMA((FFN_X_SLOTS,)),
                        pltpu.SemaphoreType.DMA((FFN_Y_SLOTS,))],
    )
    return pl.pallas_call(
        functools.partial(_moe_kernel, layer=layer),
        grid_spec=grid_spec,
        out_shape=jax.ShapeDtypeStruct(xs.shape, F32),
        input_output_aliases={8: 0},
        compiler_params=_cparams(("arbitrary",)),
        name="moe_ffn",
    )(sched["nblk"], sched["bexp"], sched["bfirst"], sched["nexp"], sched["wslot"], w1, w3, w2, xs)


def _row_slice(row, n_rows):
    return pl.ds(pl.multiple_of(row * ROW_TILE, ROW_TILE), n_rows * ROW_TILE)


def _pow2_pieces(n, largest, act):
    k = largest
    while k >= 1:
        shift = k.bit_length()

        @pl.when((n & k) != 0)
        def _(k=k, shift=shift):
            act((n >> shift) << shift, k)
        k //= 2


def _run_copies(runs_ref, make_copy, act):
    def per_expert(e, carry):
        n = runs_ref[0, 0, e]
        local_row = runs_ref[0, 1, e]
        sorted_row = runs_ref[0, 2, e]

        def chunk(i, c):
            act(make_copy(local_row + 8 * i, sorted_row + 8 * i, 8))
            return c

        lax.fori_loop(0, n >> 3, chunk, 0)
        _pow2_pieces(n, 4, lambda off, k: act(make_copy(local_row + off, sorted_row + off, k)))
        return carry

    lax.fori_loop(0, N_EXPERTS, per_expert, 0)


def _dispatch_kernel(seg_ref, lpos_ref, runs_ref, h_ref, xs_ref, local, zbuf, sem):
    tm = h_ref.shape[0] // ROW_TILE
    n_rows = xs_ref.shape[0] // ROW_TILE
    zrows = zbuf.shape[0] // ROW_TILE

    def place(tok, carry):
        row = h_ref[_row_slice(tok, 1), :]
        for k in range(TOP_K):
            local[_row_slice(lpos_ref[0, 0, TOP_K * tok + k], 1), :] = row
        return carry

    lax.fori_loop(0, tm, place, 0, unroll=8)

    def to_sorted(local_row, sorted_row, n):
        return pltpu.make_async_copy(local.at[_row_slice(local_row, n)],
                                     xs_ref.at[_row_slice(sorted_row, n)], sem)

    _run_copies(runs_ref, to_sorted, lambda cp: cp.start())
    pltpu.make_async_copy(local, xs_ref.at[pl.ds(0, local.shape[0])], sem).wait()

    @pl.when(pl.program_id(0) == pl.num_programs(0) - 1)
    def _():
        zbuf[...] = jnp.zeros_like(zbuf)

        def zero_copy(first_row, k):
            return pltpu.make_async_copy(zbuf.at[_row_slice(0, k)], xs_ref.at[_row_slice(first_row, k)], sem)

        def pad_copies(act):
            def per_expert(e, carry):
                cnt = seg_ref[0, e]
                first = seg_ref[1, e] + cnt
                _pow2_pieces((-cnt) & (MOE_BLOCK - 1), zrows, lambda off, k: act(zero_copy(first + off, k)))
                return carry

            lax.fori_loop(0, N_EXPERTS, per_expert, 0)
            used = seg_ref[2, 0]

            def tail(i, carry):
                act(zero_copy(used + i * zrows, zrows))
                return carry

            lax.fori_loop(0, (n_rows - used) // zrows, tail, 0)

        pad_copies(lambda cp: cp.start())
        pad_copies(lambda cp: cp.wait())


def _dispatch(h2_tiles, seg, lpos, runs, n_rows, *, tm):
    n_tok = h2_tiles.shape[0] // ROW_TILE
    nt = n_tok // tm
    grid_spec = pltpu.PrefetchScalarGridSpec(
        num_scalar_prefetch=1,
        grid=(nt,),
        in_specs=[pl.BlockSpec((1, 1, TOP_K * tm), lambda i, c: (i, 0, 0), memory_space=pltpu.SMEM),
                  pl.BlockSpec((1, 3, N_EXPERTS), lambda i, c: (i, 0, 0), memory_space=pltpu.SMEM),
                  pl.BlockSpec((tm * ROW_TILE, V7X_LANES), lambda i, c: (i, 0))],
        out_specs=pl.BlockSpec(memory_space=pl.ANY),
        scratch_shapes=[pltpu.VMEM((TOP_K * tm * ROW_TILE, V7X_LANES), F32),
                        pltpu.VMEM((MOE_BLOCK // 2 * ROW_TILE, V7X_LANES), F32),
                        pltpu.SemaphoreType.DMA(())],
    )
    return pl.pallas_call(
        _dispatch_kernel,
        grid_spec=grid_spec,
        out_shape=jax.ShapeDtypeStruct((n_rows * ROW_TILE, V7X_LANES), F32),
        compiler_params=_cparams(("arbitrary",)),
        name="moe_dispatch",
    )(seg, lpos, runs, h2_tiles)


def _combine_kernel(*refs, final):
    if final:
        lpos_ref, gate_ref, runs_ref, next_runs_ref, x1_ref, yb_ref, g_ref, o_ref, local, acc, sem = refs
    else:
        lpos_ref, gate_ref, runs_ref, next_runs_ref, x1_ref, yb_ref, o_ref, local, acc, sem = refs
    tm = x1_ref.shape[0]
    step = pl.program_id(0)
    cur = step % 2

    def from_sorted(slot):
        def make(local_row, sorted_row, n):
            return pltpu.make_async_copy(yb_ref.at[_row_slice(sorted_row, n)],
                                         local.at[slot, _row_slice(local_row, n)], sem.at[slot])
        return make

    @pl.when(step == 0)
    def _():
        _run_copies(runs_ref, from_sorted(cur), lambda cp: cp.start())

    @pl.when(step + 1 < pl.num_programs(0))
    def _():
        _run_copies(next_runs_ref, from_sorted(1 - cur), lambda cp: cp.start())

    pltpu.make_async_copy(yb_ref.at[pl.ds(0, local.shape[1])], local.at[cur], sem.at[cur]).wait()

    def mix(tok, carry):
        y = jnp.zeros((ROW_TILE, V7X_LANES), F32)
        for k in range(TOP_K):
            idx = TOP_K * tok + k
            y = y + gate_ref[0, 0, idx] * local[cur, _row_slice(lpos_ref[0, 0, idx], 1), :]
        acc[_row_slice(tok, 1), :] = y
        return carry

    lax.fori_loop(0, tm, mix, 0, unroll=8)
    x2 = x1_ref[...] + _load_row_tiles(acc, tm)
    if final:
        ms = jnp.mean(x2 * x2, axis=-1, keepdims=True)
        x2 = x2 * lax.rsqrt(ms + NORM_EPS) * g_ref[...]
    o_ref[...] = x2


def _combine(x1, yb, lpos, gates, runs, *, tm, final_g=None):
    n_tok, d = x1.shape
    nt = n_tok // tm
    final = final_g is not None
    in_specs = [pl.BlockSpec((1, 1, TOP_K * tm), lambda i: (i, 0, 0), memory_space=pltpu.SMEM),
                pl.BlockSpec((1, 1, TOP_K * tm), lambda i: (i, 0, 0), memory_space=pltpu.SMEM),
                pl.BlockSpec((1, 3, N_EXPERTS), lambda i: (i, 0, 0), memory_space=pltpu.SMEM),
                pl.BlockSpec((1, 3, N_EXPERTS), lambda i: (jnp.minimum(i + 1, nt - 1), 0, 0),
                             memory_space=pltpu.SMEM),
                pl.BlockSpec((tm, d), lambda i: (i, 0)),
                pl.BlockSpec(memory_space=pl.ANY)]
    args = [lpos, gates, runs, runs, x1, yb]
    if final:
        in_specs.append(pl.BlockSpec((1, d), lambda i: (0, 0)))
        args.append(final_g.reshape(1, d))
    return pl.pallas_call(
        functools.partial(_combine_kernel, final=final),
        grid=(nt,),
        in_specs=in_specs,
        out_specs=pl.BlockSpec((tm, d), lambda i: (i, 0)),
        out_shape=jax.ShapeDtypeStruct((n_tok, d), F32),
        scratch_shapes=[pltpu.VMEM((2, TOP_K * tm * ROW_TILE, V7X_LANES), F32),
                        pltpu.VMEM((tm * ROW_TILE, V7X_LANES), F32),
                        pltpu.SemaphoreType.DMA((2,))],
        compiler_params=_cparams(("arbitrary",)),
        name="moe_combine_final" if final else "moe_combine",
    )(*args)


def _moe_plan(rt, n_tok, tm):
    nt = n_tok // tm
    expert = rt[:, 0:TOP_K].astype(jnp.int32)
    gate = rt[:, TOP_K:2 * TOP_K]
    pos = rt[:, 2 * TOP_K:3 * TOP_K].astype(jnp.int32)
    is_e = (expert[:, :, None] == jnp.arange(N_EXPERTS, dtype=jnp.int32)).reshape(nt, tm, TOP_K, N_EXPERTS)
    cnt_tile = jnp.sum(is_e, axis=(1, 2), dtype=jnp.int32)
    seen_before = jnp.cumsum(cnt_tile, axis=0) - cnt_tile
    local_start = jnp.cumsum(cnt_tile, axis=1) - cnt_tile
    counts = jnp.sum(cnt_tile, axis=0)
    shift = local_start - seen_before
    lpos = pos.reshape(nt, tm, TOP_K) + jnp.sum(jnp.where(is_e, shift[:, None, None, :], 0), axis=-1)
    nblk = (counts + MOE_BLOCK - 1) // MOE_BLOCK
    bend = jnp.cumsum(nblk)
    seg_start = (bend - nblk) * MOE_BLOCK
    sorted_start = seg_start[None, :] + seen_before
    runs = jnp.stack([cnt_tile, local_start, sorted_start], axis=1)
    seg = jnp.stack([counts, seg_start, jnp.full((N_EXPERTS,), bend[-1] * MOE_BLOCK, jnp.int32)])
    n_steps = n_tok * TOP_K // MOE_BLOCK + N_EXPERTS
    step = jnp.arange(n_steps, dtype=jnp.int32)
    bexp = jnp.sum(jnp.minimum(step, bend[-1] - 1)[:, None] >= bend[None, :], axis=1).astype(jnp.int32)
    bfirst = jnp.concatenate([jnp.ones((1,), jnp.int32), (bexp[1:] != bexp[:-1]).astype(jnp.int32)])
    wslot = (jnp.cumsum(bfirst) - 1) % FFN_W_SLOTS
    switch_at = jnp.where(bfirst != 0, step, n_steps)
    next_switch = jnp.concatenate([lax.cummin(switch_at[::-1])[::-1][1:], jnp.full((1,), n_steps, jnp.int32)])
    nexp = jnp.sum(jnp.where(next_switch[:, None] == step[None, :], bexp[None, :] + 1, 0), axis=1) - 1
    sched = dict(nblk=bend[-1:].astype(jnp.int32), bexp=bexp, bfirst=bfirst,
                 nexp=nexp.astype(jnp.int32), wslot=wslot.astype(jnp.int32))
    return dict(seg=seg.astype(jnp.int32), n_rows=n_steps * MOE_BLOCK, sched=sched,
                lpos=lpos.reshape(nt, 1, TOP_K * tm), gates=gate.reshape(nt, 1, TOP_K * tm), runs=runs)


def _moe_layer(x1, h2_tiles, rt, w1, w3, w2, layer, *, final_g=None):
    n_tok = x1.shape[0]
    tm = 1024
    plan = _moe_plan(rt, n_tok, tm)
    xs = _dispatch(h2_tiles, plan["seg"], plan["lpos"], plan["runs"], plan["n_rows"], tm=tm)
    yb = _moe_ffn(xs, plan["sched"], w1, w3, w2, layer)
    return _combine(x1, yb, plan["lpos"], plan["gates"], plan["runs"], tm=tm, final_g=final_g)


def kernel(x, mem, mem_norm, final_norm, norm_mix, norm_ffn, w_mem_kv, w_out, na_w_in, na_rpb, gla_w_in,
           gla_gate_up, gla_gate_bias, gla_out_norm, moe_w_group, moe_b_group, moe_w_router, moe_b_router,
           moe_w1, moe_w3, moe_w2):
    b, t, d = x.shape
    n = b * t
    depth = norm_mix.shape[0]
    n_mem = mem.shape[1]
    xf = x.reshape(n, d)
    for i in range(depth):
        j = i // 2
        mkv = _norm_matmul(mem.reshape(b * n_mem, d), mem_norm, w_mem_kv[i].astype(BF16),
                           tm=256, out_dtype=F32, name="mem_kv_proj").reshape(b, n_mem, 2 * MEM_DIM)
        kbd, vbd = _memory_kv_blockdiag(mkv)
        if i % 2 == 0:
            u = _norm_matmul(xf, norm_mix[i], na_w_in[j].astype(BF16), tm=512, out_dtype=BF16,
                             name="na_in_proj").reshape(b, t, -1)
            mix = _na_attention(u, na_rpb[j])
            mq_off = NA_MQ_OFF
            w_o = w_out[i].astype(BF16)
            w_o = (w_o[:MIX_DIM], w_o[MIX_DIM:])
        else:
            w_pad, gu, gb, onorm = _gla_weights(gla_w_in[j], gla_gate_up[j], gla_gate_bias[j], gla_out_norm[j])
            u = _norm_matmul(xf, norm_mix[i], w_pad, tm=512, out_dtype=BF16, name="gla_in_proj").reshape(b, t, -1)
            ofwd = _gla_direction(u, gu[0], gb[0], reverse=False, tb=512)
            mix = _gla_direction(u, gu[1], gb[1], reverse=True, tb=512, ofwd=ofwd, onorm=onorm)
            mq_off = GLA_MQ_OFF
            w_mix = w_out[i][:MIX_DIM].reshape(GLA_HEADS, GLA_DV, d)
            w_mix = jnp.pad(w_mix, ((0, 0), (0, GLA_DV_PAD - GLA_DV), (0, 0))).reshape(GLA_MIX_PAD, d)
            w_o = (w_mix.astype(BF16), w_out[i][MIX_DIM:].astype(BF16))
        w_rt = jnp.pad(jnp.concatenate([moe_w_group[i], moe_w_router[i]], axis=1),
                       ((0, 0), (0, V7X_LANES - N_GROUPS - N_EXPERTS))).astype(F32)
        w_rt = jnp.concatenate(_split_bf16(w_rt, 2), axis=1)
        b_rt = jnp.pad(jnp.concatenate([moe_b_group[i], moe_b_router[i]]),
                       (0, V7X_LANES - N_GROUPS - N_EXPERTS)).reshape(1, V7X_LANES)
        x1, h2_tiles, rt = _post_mixer(xf.reshape(b, t, d), mix, u, mq_off, kbd, vbd, w_o, norm_ffn[i],
                                       w_rt, b_rt.astype(F32), tm=POST_TILE)
        xf = _moe_layer(x1.reshape(n, d), h2_tiles, rt.reshape(n, V7X_LANES),
                        moe_w1, moe_w3, moe_w2, i,
                        final_g=final_norm if i == depth - 1 else None)
    return xf.reshape(b, t, d)
```

```python
import functools

import numpy as np
import jax
import jax.numpy as jnp
from jax import lax
from jax.experimental import pallas as pl
from jax.experimental.pallas import tpu as pltpu

F32 = jnp.float32
BF16 = jnp.bfloat16

D_MODEL = 1024
GRID_W = 64
HEAD_DIM = 64
NORM_EPS = 1e-6
MEM_HEADS = 4
MEM_DIM = MEM_HEADS * HEAD_DIM
MIX_DIM = D_MODEL - MEM_DIM
NA_HEADS = MIX_DIM // HEAD_DIM
NA_WIN_H = 8
NA_WIN_W = 16
GLA_HEADS = 4
GLA_DK = MIX_DIM // 2 // GLA_HEADS
GLA_DV = MIX_DIM // GLA_HEADS
GLA_GATE_RANK = 16
GLA_TAU = 16.0
GLA_CHUNK = 64
N_GROUPS = 4
EXPERTS_PER_GROUP = 8
N_EXPERTS = N_GROUPS * EXPERTS_PER_GROUP
TOP_K = 2
EXPERT_FF = 512
MOE_BLOCK = 256

V7X_LANES = 128
V7X_MXU_DIM = 256
V7X_VMEM_LIMIT_BYTES = 56 * 1024 * 1024

GLA_DK_PAD = 128
GLA_DV_PAD = 256
GLA_Q_OFF = 0
GLA_K_OFF = GLA_Q_OFF + GLA_HEADS * GLA_DK_PAD
GLA_V_OFF = GLA_K_OFF + GLA_HEADS * GLA_DK_PAD
GLA_G_OFF = GLA_V_OFF + GLA_HEADS * GLA_DV_PAD
GLA_R_OFF = GLA_G_OFF + GLA_HEADS * GLA_DV_PAD
GLA_R_PAD = 256
GLA_MQ_OFF = GLA_R_OFF + GLA_R_PAD
GLA_IN_PAD = GLA_MQ_OFF + MEM_DIM
GLA_MIX_PAD = GLA_HEADS * GLA_DV_PAD
NA_MQ_OFF = 3 * MIX_DIM
NA_ROWS_PER_STEP = 4
GLA_TIME_BLOCK = 1024
POST_TILE = 1024
POST_SUBTILES = 4


def _cparams(semantics):
    return pltpu.CompilerParams(dimension_semantics=semantics, vmem_limit_bytes=V7X_VMEM_LIMIT_BYTES)


def _split_bf16(x, n_pieces):
    pieces = []
    rest = x
    for _ in range(n_pieces - 1):
        c = rest * 65537.0
        hi = c - (c - rest)
        pieces.append(hi.astype(BF16))
        rest = rest - hi
    pieces.append(rest.astype(BF16))
    return pieces


ROW_TILE = D_MODEL // V7X_LANES


def _store_row_tiles(ref, val):
    rows = val.shape[0]
    for s in range(ROW_TILE):
        ref[pl.ds(s, rows, stride=ROW_TILE), :] = val[:, s * V7X_LANES:(s + 1) * V7X_LANES]


def _load_row_tiles(ref, rows):
    return jnp.concatenate([ref[pl.ds(s, rows, stride=ROW_TILE), :] for s in range(ROW_TILE)], axis=1)


def _norm_matmul_kernel(x_ref, g_ref, w_ref, o_ref, *, col_chunk):
    x = x_ref[...]
    ms = jnp.mean(x * x, axis=-1, keepdims=True)
    y = (x * lax.rsqrt(ms + NORM_EPS) * g_ref[...]).astype(BF16)
    n_out = o_ref.shape[1]
    for c in range(0, n_out, col_chunk):
        o_ref[:, c:c + col_chunk] = jnp.dot(
            y, w_ref[:, c:c + col_chunk], preferred_element_type=F32).astype(o_ref.dtype)


def _norm_matmul(x, g, w, *, tm, out_dtype, name):
    n, d = x.shape
    n_out = w.shape[1]
    col_chunk = 512 if n_out % 512 == 0 else n_out
    return pl.pallas_call(
        functools.partial(_norm_matmul_kernel, col_chunk=col_chunk),
        grid=(n // tm,),
        in_specs=[pl.BlockSpec((tm, d), lambda i: (i, 0)),
                  pl.BlockSpec((1, d), lambda i: (0, 0)),
                  pl.BlockSpec((d, n_out), lambda i: (0, 0))],
        out_specs=pl.BlockSpec((tm, n_out), lambda i: (i, 0)),
        out_shape=jax.ShapeDtypeStruct((n, n_out), out_dtype),
        compiler_params=_cparams(("parallel",)),
        name=name,
    )(x, g.reshape(1, d), w)


def _na_kernel(q_ref, k_ref, v_ref, *rest):
    *bias_refs, o_ref = rest
    rows = k_ref.shape[1] // GRID_W
    n_keys = NA_WIN_H * GRID_W
    heads_per_slab = V7X_MXU_DIM // HEAD_DIM
    scale = HEAD_DIM ** -0.5
    lane_head = lax.broadcasted_iota(jnp.int32, (GRID_W, V7X_MXU_DIM), 1) // HEAD_DIM
    slabs = [slice(s * V7X_MXU_DIM, (s + 1) * V7X_MXU_DIM) for s in range(MIX_DIM // V7X_MXU_DIM)]

    def window_start(rr):
        r = pl.program_id(1) * len(bias_refs) + rr
        rs = jnp.clip(r - NA_WIN_H // 2, 0, rows - NA_WIN_H)
        return pl.multiple_of(rs * GRID_W, GRID_W)

    def scores(rr, cs):
        qq = q_ref[0, rr * GRID_W:(rr + 1) * GRID_W, cs] * scale
        kw = k_ref[0, pl.ds(window_start(rr), n_keys), cs]
        lhs = jnp.concatenate(
            [jnp.where(lane_head == i, qq, jnp.zeros_like(qq)) for i in range(heads_per_slab)], axis=0)
        sc = lax.dot_general(lhs, kw, (((1,), (1,)), ((), ())), preferred_element_type=F32)
        return sc + bias_refs[rr][0, cs, :]

    units = [(rr, cs) for rr in range(len(bias_refs)) for cs in slabs]
    nxt = scores(*units[0])
    for ui, (rr, cs) in enumerate(units):
        sc = nxt
        if ui + 1 < len(units):
            nxt = scores(*units[ui + 1])
        vw = v_ref[0, pl.ds(window_start(rr), n_keys), cs]
        m = jnp.max(sc, axis=-1, keepdims=True)
        p = jnp.exp(sc - m)
        l = jnp.sum(p, axis=-1, keepdims=True)
        o = jnp.dot(p.astype(BF16), vw, preferred_element_type=F32)
        o = o * (1.0 / l)
        acc = jnp.zeros((GRID_W, V7X_MXU_DIM), F32)
        for i in range(heads_per_slab):
            acc = acc + jnp.where(lane_head == i, o[i * GRID_W:(i + 1) * GRID_W], 0.0)
        o_ref[0, rr * GRID_W:(rr + 1) * GRID_W, cs] = acc.astype(o_ref.dtype)


def _na_bias_table(rpb):
    qc = np.arange(GRID_W)[:, None]
    kc = np.arange(GRID_W)[None, :]
    cstart = np.clip(qc - NA_WIN_W // 2, 0, GRID_W - NA_WIN_W)
    col_in = (kc >= cstart) & (kc < cstart + NA_WIN_W)
    dcol = np.clip(kc - qc, 1 - NA_WIN_W, NA_WIN_W - 1) + NA_WIN_W - 1
    pick = jnp.asarray(dcol[None] == np.arange(2 * NA_WIN_W - 1)[:, None, None], F32)
    per_row = jnp.einsum('hrd,dqk->hrqk', rpb.astype(F32), pick, precision=lax.Precision.HIGHEST)
    per_row = jnp.where(col_in[None, None], per_row, -jnp.inf)
    tbl = jnp.stack([per_row[:, o:o + NA_WIN_H] for o in range(NA_WIN_H)])
    return tbl.transpose(0, 1, 3, 2, 4).reshape(NA_WIN_H, NA_HEADS * GRID_W, NA_WIN_H * GRID_W)


def _na_attention(u, rpb):
    b, t, _ = u.shape
    rows = t // GRID_W
    bias = _na_bias_table(rpb)

    def bias_spec(rr):
        def bias_idx(bi, g):
            r = g * NA_ROWS_PER_STEP + rr
            return (jnp.clip(r - NA_WIN_H // 2, 0, rows - NA_WIN_H) - r + NA_WIN_H - 1, 0, 0)
        return pl.BlockSpec((1, NA_HEADS * GRID_W, NA_WIN_H * GRID_W), bias_idx)

    qt = NA_ROWS_PER_STEP * GRID_W
    return pl.pallas_call(
        _na_kernel,
        grid=(b, rows // NA_ROWS_PER_STEP),
        in_specs=[pl.BlockSpec((1, qt, MIX_DIM), lambda bi, g: (bi, g, 0)),
                  pl.BlockSpec((1, t, MIX_DIM), lambda bi, g: (bi, 0, 1)),
                  pl.BlockSpec((1, t, MIX_DIM), lambda bi, g: (bi, 0, 2))]
        + [bias_spec(rr) for rr in range(NA_ROWS_PER_STEP)],
        out_specs=pl.BlockSpec((1, qt, MIX_DIM), lambda bi, g: (bi, g, 0)),
        out_shape=jax.ShapeDtypeStruct((b, t, MIX_DIM), BF16),
        compiler_params=_cparams(("parallel", "arbitrary")),
        name="na_attention",
    )(u, u, u, *([bias] * NA_ROWS_PER_STEP))


def _gla_kernel(*refs, reverse, final):
    if final:
        (q_ref, k_ref, v_ref, r_ref, gu_ref, gb_ref, ofwd_ref, g_ref, onorm_ref, o_ref, st_ref) = refs
    else:
        (q_ref, k_ref, v_ref, r_ref, gu_ref, gb_ref, o_ref, st_ref) = refs
    c = GLA_CHUNK

    @pl.when(pl.program_id(1) == 0)
    def _():
        st_ref[...] = jnp.zeros_like(st_ref)

    z = jnp.dot(r_ref[0], gu_ref[...], preferred_element_type=F32) + gb_ref[...]
    la = (jnp.minimum(z, 0.0) - jnp.log(1.0 + jnp.exp(-jnp.abs(z)))) * (1.0 / GLA_TAU)
    ri = lax.broadcasted_iota(jnp.int32, (c, c), 0)
    ci = lax.broadcasted_iota(jnp.int32, (c, c), 1)
    tri = (ci >= ri) if reverse else (ci <= ri)
    trib = jnp.where(tri, 1.0, 0.0).astype(BF16)
    la3 = jnp.concatenate(_split_bf16(la, 3), axis=1)
    gw = la.shape[1]
    mid = c // 2 if reverse else c // 2 - 1
    last = 0 if reverse else c - 1
    scale = GLA_DK ** -0.5
    n_chunks = q_ref.shape[1] // c
    order = range(n_chunks - 1, -1, -1) if reverse else range(n_chunks)
    state = [st_ref[h] for h in range(GLA_HEADS)]
    heads = range(GLA_HEADS)
    hsl = [slice(h * GLA_DK_PAD, (h + 1) * GLA_DK_PAD) for h in heads]
    vsl = [slice(h * GLA_DV_PAD, (h + 1) * GLA_DV_PAD) for h in heads]
    nt_dims = (((1,), (1,)), ((), ()))
    tn_dims = (((0,), (0,)), ((), ()))

    def prepare(ch):
        sl = slice(ch * c, (ch + 1) * c)
        b3 = jnp.dot(trib, la3[sl], preferred_element_type=F32)
        bcum = b3[:, :gw] + b3[:, gw:2 * gw] + b3[:, 2 * gw:]
        b_mid = bcum[mid:mid + 1]
        b_last = bcum[last:last + 1]
        qc = q_ref[0, sl, :].astype(F32) * scale
        kc = k_ref[0, sl, :].astype(F32)
        return dict(sl=sl, vc=v_ref[0, sl, :],
                    qe=(qc * jnp.exp(bcum - b_mid)).astype(BF16), ke=(kc * jnp.exp(b_mid - bcum)).astype(BF16),
                    qs=(qc * jnp.exp(bcum)).astype(BF16), ks=(kc * jnp.exp(b_last - bcum)).astype(BF16),
                    dec=jnp.exp(b_last))

    order = list(order)
    nxt = prepare(order[0])
    for pos_in_step, ch in enumerate(order):
        cur = nxt
        if pos_in_step + 1 < len(order):
            nxt = prepare(order[pos_in_step + 1])
        sl, vc, dec = cur["sl"], cur["vc"], cur["dec"]
        scores = [lax.dot_general(cur["qe"][:, hsl[h]], cur["ke"][:, hsl[h]], nt_dims, preferred_element_type=F32)
                  for h in heads]
        kv_t = [lax.dot_general(vc[:, vsl[h]], cur["ks"][:, hsl[h]], tn_dims, preferred_element_type=F32)
                for h in heads]
        o_inter = [lax.dot_general(cur["qs"][:, hsl[h]], state[h].astype(BF16), nt_dims, preferred_element_type=F32)
                   for h in heads]
        masked = [jnp.where(tri, scores[h], 0.0).astype(BF16) for h in heads]
        o_intra = [jnp.dot(masked[h], vc[:, vsl[h]], preferred_element_type=F32) for h in heads]
        for h in heads:
            hs, vs = hsl[h], vsl[h]
            o = o_intra[h] + o_inter[h]
            state[h] = state[h] * dec[:, hs] + kv_t[h]
            if final:
                tot = ofwd_ref[0, sl, vs] + o
                ms = jnp.sum(tot * tot, axis=-1, keepdims=True) * (1.0 / GLA_DV)
                y = tot * lax.rsqrt(ms + NORM_EPS) * onorm_ref[:, vs]
                g = g_ref[0, sl, vs].astype(F32)
                o_ref[0, sl, vs] = (y * (g / (1.0 + jnp.exp(-g)))).astype(o_ref.dtype)
            else:
                o_ref[0, sl, vs] = o
    for h in range(GLA_HEADS):
        st_ref[h] = state[h]


def _gla_direction(u, gu, gb, *, reverse, tb, ofwd=None, onorm=None):
    b, t, _ = u.shape
    nt = t // tb
    final = ofwd is not None
    tix = (lambda ti: nt - 1 - ti) if reverse else (lambda ti: ti)
    qw = GLA_HEADS * GLA_DK_PAD
    vw = GLA_HEADS * GLA_DV_PAD
    in_specs = [pl.BlockSpec((1, tb, qw), lambda bi, ti: (bi, tix(ti), GLA_Q_OFF // qw)),
                pl.BlockSpec((1, tb, qw), lambda bi, ti: (bi, tix(ti), GLA_K_OFF // qw)),
                pl.BlockSpec((1, tb, vw), lambda bi, ti: (bi, tix(ti), GLA_V_OFF // vw)),
                pl.BlockSpec((1, tb, GLA_R_PAD), lambda bi, ti: (bi, tix(ti), GLA_R_OFF // GLA_R_PAD)),
                pl.BlockSpec((GLA_R_PAD, qw), lambda bi, ti: (0, 0)),
                pl.BlockSpec((1, qw), lambda bi, ti: (0, 0))]
    args = [u, u, u, u, gu, gb]
    if final:
        in_specs += [pl.BlockSpec((1, tb, vw), lambda bi, ti: (bi, tix(ti), 0)),
                     pl.BlockSpec((1, tb, vw), lambda bi, ti: (bi, tix(ti), GLA_G_OFF // vw)),
                     pl.BlockSpec((1, vw), lambda bi, ti: (0, 0))]
        args += [ofwd, u, onorm]
    return pl.pallas_call(
        functools.partial(_gla_kernel, reverse=reverse, final=final),
        grid=(b, nt),
        in_specs=in_specs,
        out_specs=pl.BlockSpec((1, tb, vw), lambda bi, ti: (bi, tix(ti), 0)),
        out_shape=jax.ShapeDtypeStruct((b, t, vw), BF16 if final else F32),
        scratch_shapes=[pltpu.VMEM((GLA_HEADS, GLA_DV_PAD, GLA_DK_PAD), F32)],
        compiler_params=_cparams(("parallel", "arbitrary")),
        name="gla_bwd_final" if final else "gla_fwd",
    )(*args)


def _gla_weights(w_in, gate_up, gate_bias, out_norm):
    d = w_in.shape[0]
    kd = GLA_HEADS * GLA_DK
    wq, wk, wv, wg, wr, wmq = jnp.split(
        w_in, np.cumsum([kd, kd, MIX_DIM, MIX_DIM, 2 * GLA_GATE_RANK]), axis=1)

    def pad_heads(w, dh, dh_pad):
        w = w.reshape(w.shape[0], GLA_HEADS, dh)
        w = jnp.pad(w, ((0, 0), (0, 0), (0, dh_pad - dh)))
        return w.reshape(w.shape[0], GLA_HEADS * dh_pad)

    w_pad = jnp.concatenate([
        pad_heads(wq, GLA_DK, GLA_DK_PAD), pad_heads(wk, GLA_DK, GLA_DK_PAD),
        pad_heads(wv, GLA_DV, GLA_DV_PAD), pad_heads(wg, GLA_DV, GLA_DV_PAD),
        jnp.pad(wr, ((0, 0), (0, GLA_R_PAD - 2 * GLA_GATE_RANK))), wmq], axis=1)
    gu = []
    for di in range(2):
        up = pad_heads(gate_up[di], GLA_DK, GLA_DK_PAD)
        gu.append(jnp.pad(up, ((di * GLA_GATE_RANK, GLA_R_PAD - (di + 1) * GLA_GATE_RANK), (0, 0))))
    gb = [pad_heads(gate_bias[di][None, :], GLA_DK, GLA_DK_PAD) for di in range(2)]
    onorm = pad_heads(out_norm[None, :].repeat(GLA_HEADS, 0).reshape(1, MIX_DIM), GLA_DV, GLA_DV_PAD)
    return w_pad.astype(BF16), [g.astype(BF16) for g in gu], [x.astype(F32) for x in gb], onorm.astype(F32)


def _post_kernel(x_ref, mix_ref, mq_ref, kbd_ref, vbd_ref, wom_ref, woa_ref, g_ref, wrt_ref, brt_ref,
                 x1_ref, h2_ref, rt_ref, base_ref):
    @pl.when((pl.program_id(0) == 0) & (pl.program_id(1) == 0))
    def _():
        base_ref[...] = jnp.zeros_like(base_ref)

    tm = x_ref.shape[1]
    hm = tm // POST_SUBTILES
    groups = [slice(i * hm, (i + 1) * hm) for i in range(POST_SUBTILES)]
    n_mem = kbd_ref.shape[2] // MEM_HEADS
    lane = lax.broadcasted_iota(jnp.int32, (hm, V7X_LANES), 1).astype(F32)
    neg = -jnp.inf
    big = 1e9

    s = [jnp.dot(mq_ref[0, g, :], kbd_ref[0], preferred_element_type=F32) * (HEAD_DIM ** -0.5)
         for g in groups]
    mixed = [jnp.dot(mix_ref[0, g, :], wom_ref[...], preferred_element_type=F32) for g in groups]
    probs = []
    for sg in s:
        ps = []
        for h in range(MEM_HEADS):
            seg = sg[:, h * n_mem:(h + 1) * n_mem]
            e = jnp.exp(seg - jnp.max(seg, axis=-1, keepdims=True))
            ps.append((e * (1.0 / jnp.sum(e, axis=-1, keepdims=True))).astype(BF16))
        probs.append(jnp.concatenate(ps, axis=1))
    att = [jnp.dot(p, vbd_ref[0], preferred_element_type=F32).astype(BF16) for p in probs]
    logits = []
    for i, g in enumerate(groups):
        x1 = x_ref[0, g, :] + mixed[i] + jnp.dot(att[i], woa_ref[...], preferred_element_type=F32)
        x1_ref[0, g, :] = x1
        ms = jnp.mean(x1 * x1, axis=-1, keepdims=True)
        h2 = x1 * lax.rsqrt(ms + NORM_EPS) * g_ref[...]
        _store_row_tiles(h2_ref.at[pl.ds(i * hm * ROW_TILE, hm * ROW_TILE)], h2)
        h_hi, h_lo = _split_bf16(h2, 2)
        hw = jnp.dot(h_hi, wrt_ref[...], preferred_element_type=F32)
        logits.append(hw[:, :V7X_LANES] + hw[:, V7X_LANES:]
                      + jnp.dot(h_lo, wrt_ref[:, :V7X_LANES], preferred_element_type=F32) + brt_ref[...])
    routed = []
    for lg in logits:
        gl = jnp.where(lane < N_GROUPS, lg, neg)
        gm = jnp.max(gl, axis=-1, keepdims=True)
        g_gate = 1.0 / jnp.sum(jnp.exp(gl - gm), axis=-1, keepdims=True)
        g_idx = jnp.min(jnp.where(gl == gm, lane, big), axis=-1, keepdims=True)
        lo = N_GROUPS + EXPERTS_PER_GROUP * g_idx
        v1 = jnp.where((lane >= lo) & (lane < lo + EXPERTS_PER_GROUP), lg, neg)
        m1 = jnp.max(v1, axis=-1, keepdims=True)
        i1 = jnp.min(jnp.where(v1 == m1, lane, big), axis=-1, keepdims=True)
        v2 = jnp.where(lane == i1, neg, v1)
        m2 = jnp.max(v2, axis=-1, keepdims=True)
        i2 = jnp.min(jnp.where(v2 == m2, lane, big), axis=-1, keepdims=True)
        t = jnp.exp(m2 - m1)
        routed.append((i1 - N_GROUPS, i2 - N_GROUPS, g_gate / (1.0 + t), g_gate * t / (1.0 + t)))
    ri = lax.broadcasted_iota(jnp.int32, (hm, hm), 0)
    ci = lax.broadcasted_iota(jnp.int32, (hm, hm), 1)
    before = jnp.where(ci < ri, 1.0, 0.0).astype(BF16)
    onehots = [jnp.where(lane == e1, 1.0, 0.0) + jnp.where(lane == e2, 1.0, 0.0)
               for e1, e2, _, _ in routed]
    earlier = [jnp.dot(before, oh.astype(BF16), preferred_element_type=F32) for oh in onehots]
    for i, g in enumerate(groups):
        e1, e2, w1, w2 = routed[i]
        ahead = earlier[i] + base_ref[...]
        pos1 = jnp.sum(jnp.where(lane == e1, ahead, 0.0), axis=-1, keepdims=True)
        pos2 = jnp.sum(jnp.where(lane == e2, ahead, 0.0), axis=-1, keepdims=True)
        base_ref[...] = base_ref[...] + jnp.sum(onehots[i], axis=0, keepdims=True)
        rt_ref[0, g, :] = jnp.where(
            lane == 0, e1, jnp.where(lane == 1, e2, jnp.where(lane == 2, w1, jnp.where(
                lane == 3, w2, jnp.where(lane == 4, pos1, jnp.where(lane == 5, pos2, 0.0))))))


def _post_mixer(x, mix, u, mq_off, kbd, vbd, w_out, g, w_rt, b_rt, *, tm):
    b, t, d = x.shape
    mixw = mix.shape[2]
    wom = w_out[0]
    woa = w_out[1]
    return pl.pallas_call(
        _post_kernel,
        grid=(b, t // tm),
        in_specs=[pl.BlockSpec((1, tm, d), lambda bi, ti: (bi, ti, 0)),
                  pl.BlockSpec((1, tm, mixw), lambda bi, ti: (bi, ti, 0)),
                  pl.BlockSpec((1, tm, MEM_DIM), lambda bi, ti: (bi, ti, mq_off // MEM_DIM)),
                  pl.BlockSpec((1,) + kbd.shape[1:], lambda bi, ti: (bi, 0, 0)),
                  pl.BlockSpec((1,) + vbd.shape[1:], lambda bi, ti: (bi, 0, 0)),
                  pl.BlockSpec(wom.shape, lambda bi, ti: (0, 0)),
                  pl.BlockSpec(woa.shape, lambda bi, ti: (0, 0)),
                  pl.BlockSpec((1, d), lambda bi, ti: (0, 0)),
                  pl.BlockSpec(w_rt.shape, lambda bi, ti: (0, 0)),
                  pl.BlockSpec((1, V7X_LANES), lambda bi, ti: (0, 0))],
        out_specs=[pl.BlockSpec((1, tm, d), lambda bi, ti: (bi, ti, 0)),
                   pl.BlockSpec((tm * ROW_TILE, V7X_LANES), lambda bi, ti: (bi * (t // tm) + ti, 0)),
                   pl.BlockSpec((1, tm, V7X_LANES), lambda bi, ti: (bi, ti, 0))],
        out_shape=[jax.ShapeDtypeStruct((b, t, d), F32),
                   jax.ShapeDtypeStruct((b * t * ROW_TILE, V7X_LANES), F32),
                   jax.ShapeDtypeStruct((b, t, V7X_LANES), F32)],
        scratch_shapes=[pltpu.VMEM((1, V7X_LANES), F32)],
        compiler_params=_cparams(("arbitrary", "arbitrary")),
        name="post_mixer",
    )(x, mix, u, kbd, vbd, wom, woa, g.reshape(1, d), w_rt, b_rt)


def _memory_kv_blockdiag(mkv):
    k, v = jnp.split(mkv, 2, axis=-1)
    head_of = np.arange(MEM_DIM) // HEAD_DIM
    sel = jnp.asarray(head_of[None, :] == np.arange(MEM_HEADS)[:, None], F32)
    kbd = jnp.einsum('bmd,hd->bdhm', k, sel).reshape(k.shape[0], MEM_DIM, -1)
    vbd = jnp.einsum('bmd,hd->bhmd', v, sel).reshape(v.shape[0], -1, MEM_DIM)
    return kbd.astype(BF16), vbd.astype(BF16)


FFN_X_SLOTS = 4
FFN_Y_SLOTS = 2
FFN_W_SLOTS = 2


def _moe_kernel(nblk_ref, bexp_ref, bfirst_ref, nexp_ref, wslot_ref, w1_ref, w3_ref, w2_ref, xs_ref, y_ref,
                wf1, wf3, wf2, wb1, wb3, wb2, xbuf, ybuf, wsem, xsem, ysem, *, layer):
    del xs_ref
    n_blk = nblk_ref[0]

    def rows_of(blk):
        return _row_slice(blk * MOE_BLOCK, MOE_BLOCK)

    def fetch(blk):
        slot = blk % FFN_X_SLOTS
        return pltpu.make_async_copy(y_ref.at[rows_of(blk)], xbuf.at[slot], xsem.at[slot])

    def writeback(blk):
        slot = blk % FFN_Y_SLOTS
        return pltpu.make_async_copy(ybuf.at[slot], y_ref.at[rows_of(blk)], ysem.at[slot])

    def weight_copies(expert, slot):
        return [pltpu.make_async_copy(src.at[layer, expert], dst.at[slot], wsem.at[slot, j])
                for j, (src, dst) in enumerate(((w1_ref, wf1), (w3_ref, wf3), (w2_ref, wf2)))]

    for cp in weight_copies(bexp_ref[0], 0):
        cp.start()
    for ahead in range(FFN_X_SLOTS - 1):
        @pl.when(ahead < n_blk)
        def _(ahead=ahead):
            fetch(ahead).start()

    def per_block(blk, carry):
        @pl.when(blk + FFN_X_SLOTS - 1 < n_blk)
        def _():
            fetch(blk + FFN_X_SLOTS - 1).start()

        @pl.when(bfirst_ref[blk] != 0)
        def _():
            slot = wslot_ref[blk]

            @pl.when(nexp_ref[blk] >= 0)
            def _():
                for cp in weight_copies(nexp_ref[blk], 1 - slot):
                    cp.start()

            for cp in weight_copies(bexp_ref[blk], slot):
                cp.wait()
            wb1[...] = wf1[slot].astype(BF16)
            wb3[...] = wf3[slot].astype(BF16)
            wb2[...] = wf2[slot].astype(BF16)

        fetch(blk).wait()

        @pl.when(blk >= FFN_Y_SLOTS)
        def _():
            writeback(blk - FFN_Y_SLOTS).wait()

        xb = _load_row_tiles(xbuf.at[blk % FFN_X_SLOTS], MOE_BLOCK).astype(BF16)
        half = wb1.shape[1] // 2
        up = [(jnp.dot(xb, wb1[:, hf * half:(hf + 1) * half], preferred_element_type=F32),
               jnp.dot(xb, wb3[:, hf * half:(hf + 1) * half], preferred_element_type=F32)) for hf in range(2)]
        y = None
        for hf, (a, c) in enumerate(up):
            hid = (a / (1.0 + jnp.exp(-a)) * c).astype(BF16)
            part = jnp.dot(hid, wb2[hf * half:(hf + 1) * half, :], preferred_element_type=F32)
            y = part if y is None else y + part
        _store_row_tiles(ybuf.at[blk % FFN_Y_SLOTS], y)
        writeback(blk).start()
        return carry

    lax.fori_loop(0, n_blk, per_block, 0)
    for back in range(FFN_Y_SLOTS, 0, -1):
        @pl.when(n_blk >= back)
        def _(back=back):
            writeback(n_blk - back).wait()


def _moe_ffn(xs, sched, w1, w3, w2, layer):
    d = w1.shape[2]
    ff = w1.shape[3]
    blk_rows = MOE_BLOCK * ROW_TILE
    any_spec = pl.BlockSpec(memory_space=pl.ANY)
    grid_spec = pltpu.PrefetchScalarGridSpec(
        num_scalar_prefetch=5,
        grid=(1,),
        in_specs=[any_spec, any_spec, any_spec, any_spec],
        out_specs=any_spec,
        scratch_shapes=[pltpu.VMEM((FFN_W_SLOTS, d, ff), F32), pltpu.VMEM((FFN_W_SLOTS, d, ff), F32),
                        pltpu.VMEM((FFN_W_SLOTS, ff, d), F32),
                        pltpu.VMEM((d, ff), BF16), pltpu.VMEM((d, ff), BF16), pltpu.VMEM((ff, d), BF16),
                        pltpu.VMEM((FFN_X_SLOTS, blk_rows, V7X_LANES), F32),
                        pltpu.VMEM((FFN_Y_SLOTS, blk_rows, V7X_LANES), F32),
                        pltpu.SemaphoreType.DMA((FFN_W_SLOTS, 3)),
                        pltpu.SemaphoreType.DMA((FFN_X_SLOTS,)),
                        pltpu.SemaphoreType.DMA((FFN_Y_SLOTS,))],
    )
    return pl.pallas_call(
        functools.partial(_moe_kernel, layer=layer),
        grid_spec=grid_spec,
        out_shape=jax.ShapeDtypeStruct(xs.shape, F32),
        input_output_aliases={8: 0},
        compiler_params=_cparams(("arbitrary",)),
        name="moe_ffn",
    )(sched["nblk"], sched["bexp"], sched["bfirst"], sched["nexp"], sched["wslot"], w1, w3, w2, xs)


def _row_slice(row, n_rows):
    return pl.ds(pl.multiple_of(row * ROW_TILE, ROW_TILE), n_rows * ROW_TILE)


def _pow2_pieces(n, largest, act):
    k = largest
    while k >= 1:
        shift = k.bit_length()

        @pl.when((n & k) != 0)
        def _(k=k, shift=shift):
            act((n >> shift) << shift, k)
        k //= 2


def _run_copies(runs_ref, make_copy, act):
    def per_expert(e, carry):
        n = runs_ref[0, 0, e]
        local_row = runs_ref[0, 1, e]
        sorted_row = runs_ref[0, 2, e]

        def chunk(i, c):
            act(make_copy(local_row + 8 * i, sorted_row + 8 * i, 8))
            return c

        lax.fori_loop(0, n >> 3, chunk, 0)
        _pow2_pieces(n, 4, lambda off, k: act(make_copy(local_row + off, sorted_row + off, k)))
        return carry

    lax.fori_loop(0, N_EXPERTS, per_expert, 0)


def _dispatch_kernel(seg_ref, lpos_ref, runs_ref, h_ref, xs_ref, local, zbuf, sem):
    tm = h_ref.shape[0] // ROW_TILE
    n_rows = xs_ref.shape[0] // ROW_TILE
    zrows = zbuf.shape[0] // ROW_TILE

    def place(tok, carry):
        row = h_ref[_row_slice(tok, 1), :]
        for k in range(TOP_K):
            local[_row_slice(lpos_ref[0, 0, k * tm + tok], 1), :] = row
        return carry

    lax.fori_loop(0, tm, place, 0, unroll=8)

    def to_sorted(local_row, sorted_row, n):
        return pltpu.make_async_copy(local.at[_row_slice(local_row, n)],
                                     xs_ref.at[_row_slice(sorted_row, n)], sem)

    _run_copies(runs_ref, to_sorted, lambda cp: cp.start())
    pltpu.make_async_copy(local, xs_ref.at[pl.ds(0, local.shape[0])], sem).wait()

    @pl.when(pl.program_id(0) == pl.num_programs(0) - 1)
    def _():
        zbuf[...] = jnp.zeros_like(zbuf)

        def zero_copy(first_row, k):
            return pltpu.make_async_copy(zbuf.at[_row_slice(0, k)], xs_ref.at[_row_slice(first_row, k)], sem)

        def pad_copies(act):
            def per_expert(e, carry):
                cnt = seg_ref[0, e]
                first = seg_ref[1, e] + cnt
                _pow2_pieces((-cnt) & (MOE_BLOCK - 1), zrows, lambda off, k: act(zero_copy(first + off, k)))
                return carry

            lax.fori_loop(0, N_EXPERTS, per_expert, 0)
            used = seg_ref[2, 0]

            def tail(i, carry):
                act(zero_copy(used + i * zrows, zrows))
                return carry

            lax.fori_loop(0, (n_rows - used) // zrows, tail, 0)

        pad_copies(lambda cp: cp.start())
        pad_copies(lambda cp: cp.wait())


def _dispatch(h2_tiles, seg, lpos, runs, n_rows, *, tm):
    n_tok = h2_tiles.shape[0] // ROW_TILE
    nt = n_tok // tm
    grid_spec = pltpu.PrefetchScalarGridSpec(
        num_scalar_prefetch=1,
        grid=(nt,),
        in_specs=[pl.BlockSpec((1, 1, TOP_K * tm), lambda i, c: (i, 0, 0), memory_space=pltpu.SMEM),
                  pl.BlockSpec((1, 3, N_EXPERTS), lambda i, c: (i, 0, 0), memory_space=pltpu.SMEM),
                  pl.BlockSpec((tm * ROW_TILE, V7X_LANES), lambda i, c: (i, 0))],
        out_specs=pl.BlockSpec(memory_space=pl.ANY),
        scratch_shapes=[pltpu.VMEM((TOP_K * tm * ROW_TILE, V7X_LANES), F32),
                        pltpu.VMEM((MOE_BLOCK // 2 * ROW_TILE, V7X_LANES), F32),
                        pltpu.SemaphoreType.DMA(())],
    )
    return pl.pallas_call(
        _dispatch_kernel,
        grid_spec=grid_spec,
        out_shape=jax.ShapeDtypeStruct((n_rows * ROW_TILE, V7X_LANES), F32),
        compiler_params=_cparams(("arbitrary",)),
        name="moe_dispatch",
    )(seg, lpos, runs, h2_tiles)


def _combine_kernel(*refs, final):
    if final:
        lpos_ref, gate_ref, runs_ref, next_runs_ref, x1_ref, yb_ref, g_ref, o_ref, local, acc, sem = refs
    else:
        lpos_ref, gate_ref, runs_ref, next_runs_ref, x1_ref, yb_ref, o_ref, local, acc, sem = refs
    tm = x1_ref.shape[0]
    step = pl.program_id(0)
    cur = step % 2

    def from_sorted(slot):
        def make(local_row, sorted_row, n):
            return pltpu.make_async_copy(yb_ref.at[_row_slice(sorted_row, n)],
                                         local.at[slot, _row_slice(local_row, n)], sem.at[slot])
        return make

    @pl.when(step == 0)
    def _():
        _run_copies(runs_ref, from_sorted(cur), lambda cp: cp.start())

    @pl.when(step + 1 < pl.num_programs(0))
    def _():
        _run_copies(next_runs_ref, from_sorted(1 - cur), lambda cp: cp.start())

    pltpu.make_async_copy(yb_ref.at[pl.ds(0, local.shape[1])], local.at[cur], sem.at[cur]).wait()

    def mix(tok, carry):
        y = jnp.zeros((ROW_TILE, V7X_LANES), F32)
        for k in range(TOP_K):
            idx = k * tm + tok
            y = y + gate_ref[0, 0, idx] * local[cur, _row_slice(lpos_ref[0, 0, idx], 1), :]
        acc[_row_slice(tok, 1), :] = y
        return carry

    lax.fori_loop(0, tm, mix, 0, unroll=8)
    x2 = x1_ref[...] + _load_row_tiles(acc, tm)
    if final:
        ms = jnp.mean(x2 * x2, axis=-1, keepdims=True)
        x2 = x2 * lax.rsqrt(ms + NORM_EPS) * g_ref[...]
    o_ref[...] = x2


def _combine(x1, yb, lpos, gates, runs, *, tm, final_g=None):
    n_tok, d = x1.shape
    nt = n_tok // tm
    final = final_g is not None
    in_specs = [pl.BlockSpec((1, 1, TOP_K * tm), lambda i: (i, 0, 0), memory_space=pltpu.SMEM),
                pl.BlockSpec((1, 1, TOP_K * tm), lambda i: (i, 0, 0), memory_space=pltpu.SMEM),
                pl.BlockSpec((1, 3, N_EXPERTS), lambda i: (i, 0, 0), memory_space=pltpu.SMEM),
                pl.BlockSpec((1, 3, N_EXPERTS), lambda i: (jnp.minimum(i + 1, nt - 1), 0, 0),
                             memory_space=pltpu.SMEM),
                pl.BlockSpec((tm, d), lambda i: (i, 0)),
                pl.BlockSpec(memory_space=pl.ANY)]
    args = [lpos, gates, runs, runs, x1, yb]
    if final:
        in_specs.append(pl.BlockSpec((1, d), lambda i: (0, 0)))
        args.append(final_g.reshape(1, d))
    return pl.pallas_call(
        functools.partial(_combine_kernel, final=final),
        grid=(nt,),
        in_specs=in_specs,
        out_specs=pl.BlockSpec((tm, d), lambda i: (i, 0)),
        out_shape=jax.ShapeDtypeStruct((n_tok, d), F32),
        scratch_shapes=[pltpu.VMEM((2, TOP_K * tm * ROW_TILE, V7X_LANES), F32),
                        pltpu.VMEM((tm * ROW_TILE, V7X_LANES), F32),
                        pltpu.SemaphoreType.DMA((2,))],
        compiler_params=_cparams(("arbitrary",)),
        name="moe_combine_final" if final else "moe_combine",
    )(*args)


def _moe_plan(rt, n_tok, tm):
    nt = n_tok // tm
    rec = rt[:, :ROW_TILE].T.reshape(ROW_TILE, nt, tm)
    expert = rec[0:TOP_K].astype(jnp.int32)
    gate = rec[TOP_K:2 * TOP_K]
    pos = rec[2 * TOP_K:3 * TOP_K].astype(jnp.int32)
    is_e = expert[None] == jnp.arange(N_EXPERTS, dtype=jnp.int32)[:, None, None, None]
    cnt_tile = jnp.sum(is_e, axis=(1, 3), dtype=jnp.int32).T
    seen_before = jnp.cumsum(cnt_tile, axis=0) - cnt_tile
    local_start = jnp.cumsum(cnt_tile, axis=1) - cnt_tile
    counts = jnp.sum(cnt_tile, axis=0)
    shift = (local_start - seen_before).T
    lpos = pos + jnp.sum(jnp.where(is_e, shift[:, None, :, None], 0), axis=0)
    by_tile = lambda a: a.transpose(1, 0, 2).reshape(nt, 1, TOP_K * tm)
    nblk = (counts + MOE_BLOCK - 1) // MOE_BLOCK
    bend = jnp.cumsum(nblk)
    seg_start = (bend - nblk) * MOE_BLOCK
    sorted_start = seg_start[None, :] + seen_before
    runs = jnp.stack([cnt_tile, local_start, sorted_start], axis=1)
    seg = jnp.stack([counts, seg_start, jnp.full((N_EXPERTS,), bend[-1] * MOE_BLOCK, jnp.int32)])
    n_steps = n_tok * TOP_K // MOE_BLOCK + N_EXPERTS
    step = jnp.arange(n_steps, dtype=jnp.int32)
    bexp = jnp.sum(jnp.minimum(step, bend[-1] - 1)[:, None] >= bend[None, :], axis=1).astype(jnp.int32)
    bfirst = jnp.concatenate([jnp.ones((1,), jnp.int32), (bexp[1:] != bexp[:-1]).astype(jnp.int32)])
    wslot = (jnp.cumsum(bfirst) - 1) % FFN_W_SLOTS
    switch_at = jnp.where(bfirst != 0, step, n_steps)
    next_switch = jnp.concatenate([lax.cummin(switch_at[::-1])[::-1][1:], jnp.full((1,), n_steps, jnp.int32)])
    nexp = jnp.sum(jnp.where(next_switch[:, None] == step[None, :], bexp[None, :] + 1, 0), axis=1) - 1
    sched = dict(nblk=bend[-1:].astype(jnp.int32), bexp=bexp, bfirst=bfirst,
                 nexp=nexp.astype(jnp.int32), wslot=wslot.astype(jnp.int32))
    return dict(seg=seg.astype(jnp.int32), n_rows=n_steps * MOE_BLOCK, sched=sched,
                lpos=by_tile(lpos), gates=by_tile(gate), runs=runs)


def _moe_layer(x1, h2_tiles, rt, w1, w3, w2, layer, *, final_g=None):
    n_tok = x1.shape[0]
    tm = 1024
    plan = _moe_plan(rt, n_tok, tm)
    xs = _dispatch(h2_tiles, plan["seg"], plan["lpos"], plan["runs"], plan["n_rows"], tm=tm)
    yb = _moe_ffn(xs, plan["sched"], w1, w3, w2, layer)
    return _combine(x1, yb, plan["lpos"], plan["gates"], plan["runs"], tm=tm, final_g=final_g)


def kernel(x, mem, mem_norm, final_norm, norm_mix, norm_ffn, w_mem_kv, w_out, na_w_in, na_rpb, gla_w_in,
           gla_gate_up, gla_gate_bias, gla_out_norm, moe_w_group, moe_b_group, moe_w_router, moe_b_router,
           moe_w1, moe_w3, moe_w2):
    b, t, d = x.shape
    n = b * t
    depth = norm_mix.shape[0]
    n_mem = mem.shape[1]
    xf = x.reshape(n, d)
    for i in range(depth):
        j = i // 2
        mkv = _norm_matmul(mem.reshape(b * n_mem, d), mem_norm, w_mem_kv[i].astype(BF16),
                           tm=256, out_dtype=F32, name="mem_kv_proj").reshape(b, n_mem, 2 * MEM_DIM)
        kbd, vbd = _memory_kv_blockdiag(mkv)
        if i % 2 == 0:
            u = _norm_matmul(xf, norm_mix[i], na_w_in[j].astype(BF16), tm=512, out_dtype=BF16,
                             name="na_in_proj").reshape(b, t, -1)
            mix = _na_attention(u, na_rpb[j])
            mq_off = NA_MQ_OFF
            w_o = w_out[i].astype(BF16)
            w_o = (w_o[:MIX_DIM], w_o[MIX_DIM:])
        else:
            w_pad, gu, gb, onorm = _gla_weights(gla_w_in[j], gla_gate_up[j], gla_gate_bias[j], gla_out_norm[j])
            u = _norm_matmul(xf, norm_mix[i], w_pad, tm=512, out_dtype=BF16, name="gla_in_proj").reshape(b, t, -1)
            ofwd = _gla_direction(u, gu[0], gb[0], reverse=False, tb=GLA_TIME_BLOCK)
            mix = _gla_direction(u, gu[1], gb[1], reverse=True, tb=GLA_TIME_BLOCK, ofwd=ofwd, onorm=onorm)
            mq_off = GLA_MQ_OFF
            w_mix = w_out[i][:MIX_DIM].reshape(GLA_HEADS, GLA_DV, d)
            w_mix = jnp.pad(w_mix, ((0, 0), (0, GLA_DV_PAD - GLA_DV), (0, 0))).reshape(GLA_MIX_PAD, d)
            w_o = (w_mix.astype(BF16), w_out[i][MIX_DIM:].astype(BF16))
        w_rt = jnp.pad(jnp.concatenate([moe_w_group[i], moe_w_router[i]], axis=1),
                       ((0, 0), (0, V7X_LANES - N_GROUPS - N_EXPERTS))).astype(F32)
        w_rt = jnp.concatenate(_split_bf16(w_rt, 2), axis=1)
        b_rt = jnp.pad(jnp.concatenate([moe_b_group[i], moe_b_router[i]]),
                       (0, V7X_LANES - N_GROUPS - N_EXPERTS)).reshape(1, V7X_LANES)
        x1, h2_tiles, rt = _post_mixer(xf.reshape(b, t, d), mix, u, mq_off, kbd, vbd, w_o, norm_ffn[i],
                                       w_rt, b_rt.astype(F32), tm=POST_TILE)
        xf = _moe_layer(x1.reshape(n, d), h2_tiles, rt.reshape(n, V7X_LANES),
                        moe_w1, moe_w3, moe_w2, i,
                        final_g=final_norm if i == depth - 1 else None)
    return xf.reshape(b, t, d)
```

```python
import functools

import numpy as np
import jax
import jax.numpy as jnp
from jax import lax
from jax.experimental import pallas as pl
from jax.experimental.pallas import tpu as pltpu

F32 = jnp.float32
BF16 = jnp.bfloat16

D_MODEL = 1024
GRID_W = 64
HEAD_DIM = 64
NORM_EPS = 1e-6
MEM_HEADS = 4
MEM_DIM = MEM_HEADS * HEAD_DIM
MIX_DIM = D_MODEL - MEM_DIM
NA_HEADS = MIX_DIM // HEAD_DIM
NA_WIN_H = 8
NA_WIN_W = 16
GLA_HEADS = 4
GLA_DK = MIX_DIM // 2 // GLA_HEADS
GLA_DV = MIX_DIM // GLA_HEADS
GLA_GATE_RANK = 16
GLA_TAU = 16.0
GLA_CHUNK = 64
N_GROUPS = 4
EXPERTS_PER_GROUP = 8
N_EXPERTS = N_GROUPS * EXPERTS_PER_GROUP
TOP_K = 2
EXPERT_FF = 512
MOE_BLOCK = 256

V7X_LANES = 128
V7X_MXU_DIM = 256
V7X_VMEM_LIMIT_BYTES = 56 * 1024 * 1024

GLA_DK_PAD = 128
GLA_DV_PAD = 256
GLA_Q_OFF = 0
GLA_K_OFF = GLA_Q_OFF + GLA_HEADS * GLA_DK_PAD
GLA_V_OFF = GLA_K_OFF + GLA_HEADS * GLA_DK_PAD
GLA_G_OFF = GLA_V_OFF + GLA_HEADS * GLA_DV_PAD
GLA_R_OFF = GLA_G_OFF + GLA_HEADS * GLA_DV_PAD
GLA_R_PAD = 256
GLA_MQ_OFF = GLA_R_OFF + GLA_R_PAD
GLA_IN_PAD = GLA_MQ_OFF + MEM_DIM
GLA_MIX_PAD = GLA_HEADS * GLA_DV_PAD
NA_MQ_OFF = 3 * MIX_DIM
NA_ROWS_PER_STEP = 4
GLA_TIME_BLOCK = 1024
POST_TILE = 1024
POST_SUBTILES = 4


def _cparams(semantics):
    return pltpu.CompilerParams(dimension_semantics=semantics, vmem_limit_bytes=V7X_VMEM_LIMIT_BYTES)


def _rms_norm_rows(x, g):
    ms = jnp.mean(x * x, axis=-1, keepdims=True)
    return x * lax.rsqrt(ms + NORM_EPS) * g


def _split_bf16(x, n_pieces):
    pieces = []
    rest = x
    for _ in range(n_pieces - 1):
        c = rest * 65537.0
        hi = c - (c - rest)
        pieces.append(hi.astype(BF16))
        rest = rest - hi
    pieces.append(rest.astype(BF16))
    return pieces


ROW_TILE = D_MODEL // V7X_LANES


def _store_row_tiles(ref, val):
    rows = val.shape[0]
    for s in range(ROW_TILE):
        ref[pl.ds(s, rows, stride=ROW_TILE), :] = val[:, s * V7X_LANES:(s + 1) * V7X_LANES]


def _load_row_tiles(ref, rows):
    return jnp.concatenate([ref[pl.ds(s, rows, stride=ROW_TILE), :] for s in range(ROW_TILE)], axis=1)


def _norm_matmul_kernel(x_ref, g_ref, w_ref, o_ref, *, col_chunk):
    x = x_ref[...]
    ms = jnp.mean(x * x, axis=-1, keepdims=True)
    y = (x * lax.rsqrt(ms + NORM_EPS) * g_ref[...]).astype(BF16)
    n_out = o_ref.shape[1]
    for c in range(0, n_out, col_chunk):
        o_ref[:, c:c + col_chunk] = jnp.dot(
            y, w_ref[:, c:c + col_chunk], preferred_element_type=F32).astype(o_ref.dtype)


def _norm_matmul(x, g, w, *, tm, out_dtype, name):
    n, d = x.shape
    n_out = w.shape[1]
    col_chunk = 512 if n_out % 512 == 0 else n_out
    return pl.pallas_call(
        functools.partial(_norm_matmul_kernel, col_chunk=col_chunk),
        grid=(n // tm,),
        in_specs=[pl.BlockSpec((tm, d), lambda i: (i, 0)),
                  pl.BlockSpec((1, d), lambda i: (0, 0)),
                  pl.BlockSpec((d, n_out), lambda i: (0, 0))],
        out_specs=pl.BlockSpec((tm, n_out), lambda i: (i, 0)),
        out_shape=jax.ShapeDtypeStruct((n, n_out), out_dtype),
        compiler_params=_cparams(("parallel",)),
        name=name,
    )(x, g.reshape(1, d), w)


def _na_kernel(q_ref, k_ref, v_ref, *rest):
    *bias_refs, o_ref = rest
    rows = k_ref.shape[1] // GRID_W
    n_keys = NA_WIN_H * GRID_W
    heads_per_slab = V7X_MXU_DIM // HEAD_DIM
    scale = HEAD_DIM ** -0.5
    lane_head = lax.broadcasted_iota(jnp.int32, (GRID_W, V7X_MXU_DIM), 1) // HEAD_DIM
    slabs = [slice(s * V7X_MXU_DIM, (s + 1) * V7X_MXU_DIM) for s in range(MIX_DIM // V7X_MXU_DIM)]

    def window_start(rr):
        r = pl.program_id(1) * len(bias_refs) + rr
        rs = jnp.clip(r - NA_WIN_H // 2, 0, rows - NA_WIN_H)
        return pl.multiple_of(rs * GRID_W, GRID_W)

    def scores(rr, cs):
        qq = q_ref[0, rr * GRID_W:(rr + 1) * GRID_W, cs] * scale
        kw = k_ref[0, pl.ds(window_start(rr), n_keys), cs]
        lhs = jnp.concatenate(
            [jnp.where(lane_head == i, qq, jnp.zeros_like(qq)) for i in range(heads_per_slab)], axis=0)
        sc = lax.dot_general(lhs, kw, (((1,), (1,)), ((), ())), preferred_element_type=F32)
        return sc + bias_refs[rr][0, cs, :]

    units = [(rr, cs) for rr in range(len(bias_refs)) for cs in slabs]
    nxt = scores(*units[0])
    for ui, (rr, cs) in enumerate(units):
        sc = nxt
        if ui + 1 < len(units):
            nxt = scores(*units[ui + 1])
        vw = v_ref[0, pl.ds(window_start(rr), n_keys), cs]
        m = jnp.max(sc, axis=-1, keepdims=True)
        p = jnp.exp(sc - m)
        l = jnp.sum(p, axis=-1, keepdims=True)
        o = jnp.dot(p.astype(BF16), vw, preferred_element_type=F32)
        o = o * (1.0 / l)
        acc = jnp.zeros((GRID_W, V7X_MXU_DIM), F32)
        for i in range(heads_per_slab):
            acc = acc + jnp.where(lane_head == i, o[i * GRID_W:(i + 1) * GRID_W], 0.0)
        o_ref[0, rr * GRID_W:(rr + 1) * GRID_W, cs] = acc.astype(o_ref.dtype)


def _na_bias_table(rpb):
    qc = np.arange(GRID_W)[:, None]
    kc = np.arange(GRID_W)[None, :]
    cstart = np.clip(qc - NA_WIN_W // 2, 0, GRID_W - NA_WIN_W)
    col_in = (kc >= cstart) & (kc < cstart + NA_WIN_W)
    dcol = np.clip(kc - qc, 1 - NA_WIN_W, NA_WIN_W - 1) + NA_WIN_W - 1
    pick = jnp.asarray(dcol[None] == np.arange(2 * NA_WIN_W - 1)[:, None, None], F32)
    per_row = jnp.einsum('hrd,dqk->hrqk', rpb.astype(F32), pick, precision=lax.Precision.HIGHEST)
    per_row = jnp.where(col_in[None, None], per_row, -jnp.inf)
    tbl = jnp.stack([per_row[:, o:o + NA_WIN_H] for o in range(NA_WIN_H)])
    return tbl.transpose(0, 1, 3, 2, 4).reshape(NA_WIN_H, NA_HEADS * GRID_W, NA_WIN_H * GRID_W)


def _na_attention(u, rpb):
    b, t, _ = u.shape
    rows = t // GRID_W
    bias = _na_bias_table(rpb)

    def bias_spec(rr):
        def bias_idx(bi, g):
            r = g * NA_ROWS_PER_STEP + rr
            return (jnp.clip(r - NA_WIN_H // 2, 0, rows - NA_WIN_H) - r + NA_WIN_H - 1, 0, 0)
        return pl.BlockSpec((1, NA_HEADS * GRID_W, NA_WIN_H * GRID_W), bias_idx)

    qt = NA_ROWS_PER_STEP * GRID_W
    return pl.pallas_call(
        _na_kernel,
        grid=(b, rows // NA_ROWS_PER_STEP),
        in_specs=[pl.BlockSpec((1, qt, MIX_DIM), lambda bi, g: (bi, g, 0)),
                  pl.BlockSpec((1, t, MIX_DIM), lambda bi, g: (bi, 0, 1)),
                  pl.BlockSpec((1, t, MIX_DIM), lambda bi, g: (bi, 0, 2))]
        + [bias_spec(rr) for rr in range(NA_ROWS_PER_STEP)],
        out_specs=pl.BlockSpec((1, qt, MIX_DIM), lambda bi, g: (bi, g, 0)),
        out_shape=jax.ShapeDtypeStruct((b, t, MIX_DIM), BF16),
        compiler_params=_cparams(("parallel", "arbitrary")),
        name="na_attention",
    )(u, u, u, *([bias] * NA_ROWS_PER_STEP))


def _gla_kernel(*refs, reverse, final):
    if final:
        (q_ref, k_ref, v_ref, r_ref, gu_ref, gb_ref, ofwd_ref, g_ref, onorm_ref, o_ref, st_ref) = refs
    else:
        (q_ref, k_ref, v_ref, r_ref, gu_ref, gb_ref, o_ref, st_ref) = refs
    c = GLA_CHUNK

    @pl.when(pl.program_id(1) == 0)
    def _():
        st_ref[...] = jnp.zeros_like(st_ref)

    z = jnp.dot(r_ref[0], gu_ref[...], preferred_element_type=F32) + gb_ref[...]
    la = (jnp.minimum(z, 0.0) - jnp.log(1.0 + jnp.exp(-jnp.abs(z)))) * (1.0 / GLA_TAU)
    ri = lax.broadcasted_iota(jnp.int32, (c, c), 0)
    ci = lax.broadcasted_iota(jnp.int32, (c, c), 1)
    tri = (ci >= ri) if reverse else (ci <= ri)
    trib = jnp.where(tri, 1.0, 0.0).astype(BF16)
    la3 = jnp.concatenate(_split_bf16(la, 3), axis=1)
    gw = la.shape[1]
    mid = c // 2 if reverse else c // 2 - 1
    last = 0 if reverse else c - 1
    scale = GLA_DK ** -0.5
    n_chunks = q_ref.shape[1] // c
    order = range(n_chunks - 1, -1, -1) if reverse else range(n_chunks)
    state = [st_ref[h] for h in range(GLA_HEADS)]
    heads = range(GLA_HEADS)
    hsl = [slice(h * GLA_DK_PAD, (h + 1) * GLA_DK_PAD) for h in heads]
    vsl = [slice(h * GLA_DV_PAD, (h + 1) * GLA_DV_PAD) for h in heads]
    nt_dims = (((1,), (1,)), ((), ()))
    tn_dims = (((0,), (0,)), ((), ()))

    def prepare(ch):
        sl = slice(ch * c, (ch + 1) * c)
        b3 = jnp.dot(trib, la3[sl], preferred_element_type=F32)
        bcum = b3[:, :gw] + b3[:, gw:2 * gw] + b3[:, 2 * gw:]
        b_mid = bcum[mid:mid + 1]
        b_last = bcum[last:last + 1]
        qc = q_ref[0, sl, :].astype(F32) * scale
        kc = k_ref[0, sl, :].astype(F32)
        return dict(sl=sl, vc=v_ref[0, sl, :],
                    qe=(qc * jnp.exp(bcum - b_mid)).astype(BF16), ke=(kc * jnp.exp(b_mid - bcum)).astype(BF16),
                    qs=(qc * jnp.exp(bcum)).astype(BF16), ks=(kc * jnp.exp(b_last - bcum)).astype(BF16),
                    dec=jnp.exp(b_last))

    order = list(order)
    nxt = prepare(order[0])
    for pos_in_step, ch in enumerate(order):
        cur = nxt
        if pos_in_step + 1 < len(order):
            nxt = prepare(order[pos_in_step + 1])
        sl, vc, dec = cur["sl"], cur["vc"], cur["dec"]
        scores = [lax.dot_general(cur["qe"][:, hsl[h]], cur["ke"][:, hsl[h]], nt_dims, preferred_element_type=F32)
                  for h in heads]
        kv_t = [lax.dot_general(vc[:, vsl[h]], cur["ks"][:, hsl[h]], tn_dims, preferred_element_type=F32)
                for h in heads]
        o_inter = [lax.dot_general(cur["qs"][:, hsl[h]], state[h].astype(BF16), nt_dims, preferred_element_type=F32)
                   for h in heads]
        masked = [jnp.where(tri, scores[h], 0.0).astype(BF16) for h in heads]
        o_intra = [jnp.dot(masked[h], vc[:, vsl[h]], preferred_element_type=F32) for h in heads]
        for h in heads:
            hs, vs = hsl[h], vsl[h]
            o = o_intra[h] + o_inter[h]
            state[h] = state[h] * dec[:, hs] + kv_t[h]
            if final:
                tot = ofwd_ref[0, sl, vs].astype(F32) + o
                ms = jnp.sum(tot * tot, axis=-1, keepdims=True) * (1.0 / GLA_DV)
                y = tot * lax.rsqrt(ms + NORM_EPS) * onorm_ref[:, vs]
                g = g_ref[0, sl, vs].astype(F32)
                o_ref[0, sl, vs] = (y * (g / (1.0 + jnp.exp(-g)))).astype(o_ref.dtype)
            else:
                o_ref[0, sl, vs] = o.astype(o_ref.dtype)
    for h in range(GLA_HEADS):
        st_ref[h] = state[h]


def _gla_direction(u, gu, gb, *, reverse, tb, ofwd=None, onorm=None):
    b, t, _ = u.shape
    nt = t // tb
    final = ofwd is not None
    tix = (lambda ti: nt - 1 - ti) if reverse else (lambda ti: ti)
    qw = GLA_HEADS * GLA_DK_PAD
    vw = GLA_HEADS * GLA_DV_PAD
    in_specs = [pl.BlockSpec((1, tb, qw), lambda bi, ti: (bi, tix(ti), GLA_Q_OFF // qw)),
                pl.BlockSpec((1, tb, qw), lambda bi, ti: (bi, tix(ti), GLA_K_OFF // qw)),
                pl.BlockSpec((1, tb, vw), lambda bi, ti: (bi, tix(ti), GLA_V_OFF // vw)),
                pl.BlockSpec((1, tb, GLA_R_PAD), lambda bi, ti: (bi, tix(ti), GLA_R_OFF // GLA_R_PAD)),
                pl.BlockSpec((GLA_R_PAD, qw), lambda bi, ti: (0, 0)),
                pl.BlockSpec((1, qw), lambda bi, ti: (0, 0))]
    args = [u, u, u, u, gu, gb]
    if final:
        in_specs += [pl.BlockSpec((1, tb, vw), lambda bi, ti: (bi, tix(ti), 0)),
                     pl.BlockSpec((1, tb, vw), lambda bi, ti: (bi, tix(ti), GLA_G_OFF // vw)),
                     pl.BlockSpec((1, vw), lambda bi, ti: (0, 0))]
        args += [ofwd, u, onorm]
    return pl.pallas_call(
        functools.partial(_gla_kernel, reverse=reverse, final=final),
        grid=(b, nt),
        in_specs=in_specs,
        out_specs=pl.BlockSpec((1, tb, vw), lambda bi, ti: (bi, tix(ti), 0)),
        out_shape=jax.ShapeDtypeStruct((b, t, vw), BF16),
        scratch_shapes=[pltpu.VMEM((GLA_HEADS, GLA_DV_PAD, GLA_DK_PAD), F32)],
        compiler_params=_cparams(("parallel", "arbitrary")),
        name="gla_bwd_final" if final else "gla_fwd",
    )(*args)


def _gla_weights(w_in, gate_up, gate_bias, out_norm):
    d = w_in.shape[0]
    kd = GLA_HEADS * GLA_DK
    wq, wk, wv, wg, wr, wmq = jnp.split(
        w_in, np.cumsum([kd, kd, MIX_DIM, MIX_DIM, 2 * GLA_GATE_RANK]), axis=1)

    def pad_heads(w, dh, dh_pad):
        w = w.reshape(w.shape[0], GLA_HEADS, dh)
        w = jnp.pad(w, ((0, 0), (0, 0), (0, dh_pad - dh)))
        return w.reshape(w.shape[0], GLA_HEADS * dh_pad)

    w_pad = jnp.concatenate([
        pad_heads(wq, GLA_DK, GLA_DK_PAD), pad_heads(wk, GLA_DK, GLA_DK_PAD),
        pad_heads(wv, GLA_DV, GLA_DV_PAD), pad_heads(wg, GLA_DV, GLA_DV_PAD),
        jnp.pad(wr, ((0, 0), (0, GLA_R_PAD - 2 * GLA_GATE_RANK))), wmq], axis=1)
    gu = []
    for di in range(2):
        up = pad_heads(gate_up[di], GLA_DK, GLA_DK_PAD)
        gu.append(jnp.pad(up, ((di * GLA_GATE_RANK, GLA_R_PAD - (di + 1) * GLA_GATE_RANK), (0, 0))))
    gb = [pad_heads(gate_bias[di][None, :], GLA_DK, GLA_DK_PAD) for di in range(2)]
    onorm = pad_heads(out_norm[None, :].repeat(GLA_HEADS, 0).reshape(1, MIX_DIM), GLA_DV, GLA_DV_PAD)
    return w_pad.astype(BF16), [g.astype(BF16) for g in gu], [x.astype(F32) for x in gb], onorm.astype(F32)


def _post_kernel(x_ref, mix_ref, mq_ref, kbd_ref, vbd_ref, wom_ref, woa_ref, g_ref, wrt_ref, brt_ref,
                 x1_ref, rt_ref, base_ref):
    @pl.when((pl.program_id(0) == 0) & (pl.program_id(1) == 0))
    def _():
        base_ref[...] = jnp.zeros_like(base_ref)

    tm = x_ref.shape[1]
    hm = tm // POST_SUBTILES
    groups = [slice(i * hm, (i + 1) * hm) for i in range(POST_SUBTILES)]
    n_mem = kbd_ref.shape[2] // MEM_HEADS
    lane = lax.broadcasted_iota(jnp.int32, (hm, V7X_LANES), 1).astype(F32)
    neg = -jnp.inf
    big = 1e9

    s = [jnp.dot(mq_ref[0, g, :], kbd_ref[0], preferred_element_type=F32) * (HEAD_DIM ** -0.5)
         for g in groups]
    mixed = [jnp.dot(mix_ref[0, g, :], wom_ref[...], preferred_element_type=F32) for g in groups]
    probs = []
    for sg in s:
        ps = []
        for h in range(MEM_HEADS):
            seg = sg[:, h * n_mem:(h + 1) * n_mem]
            e = jnp.exp(seg - jnp.max(seg, axis=-1, keepdims=True))
            ps.append((e * (1.0 / jnp.sum(e, axis=-1, keepdims=True))).astype(BF16))
        probs.append(jnp.concatenate(ps, axis=1))
    att = [jnp.dot(p, vbd_ref[0], preferred_element_type=F32).astype(BF16) for p in probs]
    logits = []
    for i, g in enumerate(groups):
        x1 = x_ref[0, g, :] + mixed[i] + jnp.dot(att[i], woa_ref[...], preferred_element_type=F32)
        _store_row_tiles(x1_ref.at[pl.ds(i * hm * ROW_TILE, hm * ROW_TILE)], x1)
        h2 = _rms_norm_rows(x1, g_ref[...])
        h_hi, h_lo = _split_bf16(h2, 2)
        hw = jnp.dot(h_hi, wrt_ref[...], preferred_element_type=F32)
        logits.append(hw[:, :V7X_LANES] + hw[:, V7X_LANES:]
                      + jnp.dot(h_lo, wrt_ref[:, :V7X_LANES], preferred_element_type=F32) + brt_ref[...])
    routed = []
    for lg in logits:
        gl = jnp.where(lane < N_GROUPS, lg, neg)
        gm = jnp.max(gl, axis=-1, keepdims=True)
        g_gate = 1.0 / jnp.sum(jnp.exp(gl - gm), axis=-1, keepdims=True)
        g_idx = jnp.min(jnp.where(gl == gm, lane, big), axis=-1, keepdims=True)
        lo = N_GROUPS + EXPERTS_PER_GROUP * g_idx
        v1 = jnp.where((lane >= lo) & (lane < lo + EXPERTS_PER_GROUP), lg, neg)
        m1 = jnp.max(v1, axis=-1, keepdims=True)
        i1 = jnp.min(jnp.where(v1 == m1, lane, big), axis=-1, keepdims=True)
        v2 = jnp.where(lane == i1, neg, v1)
        m2 = jnp.max(v2, axis=-1, keepdims=True)
        i2 = jnp.min(jnp.where(v2 == m2, lane, big), axis=-1, keepdims=True)
        t = jnp.exp(m2 - m1)
        routed.append((i1 - N_GROUPS, i2 - N_GROUPS, g_gate / (1.0 + t), g_gate * t / (1.0 + t)))
    ri = lax.broadcasted_iota(jnp.int32, (hm, hm), 0)
    ci = lax.broadcasted_iota(jnp.int32, (hm, hm), 1)
    before = jnp.where(ci < ri, 1.0, 0.0).astype(BF16)
    onehots = [jnp.where(lane == e1, 1.0, 0.0) + jnp.where(lane == e2, 1.0, 0.0)
               for e1, e2, _, _ in routed]
    earlier = [jnp.dot(before, oh.astype(BF16), preferred_element_type=F32) for oh in onehots]
    for i, g in enumerate(groups):
        e1, e2, w1, w2 = routed[i]
        ahead = earlier[i] + base_ref[...]
        pos1 = jnp.sum(jnp.where(lane == e1, ahead, 0.0), axis=-1, keepdims=True)
        pos2 = jnp.sum(jnp.where(lane == e2, ahead, 0.0), axis=-1, keepdims=True)
        base_ref[...] = base_ref[...] + jnp.sum(onehots[i], axis=0, keepdims=True)
        rt_ref[0, g, :] = jnp.where(
            lane == 0, e1, jnp.where(lane == 1, e2, jnp.where(lane == 2, w1, jnp.where(
                lane == 3, w2, jnp.where(lane == 4, pos1, jnp.where(lane == 5, pos2, 0.0))))))


def _post_mixer(x, mix, u, mq_off, kbd, vbd, w_out, g, w_rt, b_rt, *, tm):
    b, t, d = x.shape
    mixw = mix.shape[2]
    wom = w_out[0]
    woa = w_out[1]
    return pl.pallas_call(
        _post_kernel,
        grid=(b, t // tm),
        in_specs=[pl.BlockSpec((1, tm, d), lambda bi, ti: (bi, ti, 0)),
                  pl.BlockSpec((1, tm, mixw), lambda bi, ti: (bi, ti, 0)),
                  pl.BlockSpec((1, tm, MEM_DIM), lambda bi, ti: (bi, ti, mq_off // MEM_DIM)),
                  pl.BlockSpec((1,) + kbd.shape[1:], lambda bi, ti: (bi, 0, 0)),
                  pl.BlockSpec((1,) + vbd.shape[1:], lambda bi, ti: (bi, 0, 0)),
                  pl.BlockSpec(wom.shape, lambda bi, ti: (0, 0)),
                  pl.BlockSpec(woa.shape, lambda bi, ti: (0, 0)),
                  pl.BlockSpec((1, d), lambda bi, ti: (0, 0)),
                  pl.BlockSpec(w_rt.shape, lambda bi, ti: (0, 0)),
                  pl.BlockSpec((1, V7X_LANES), lambda bi, ti: (0, 0))],
        out_specs=[pl.BlockSpec((tm * ROW_TILE, V7X_LANES), lambda bi, ti: (bi * (t // tm) + ti, 0)),
                   pl.BlockSpec((1, tm, V7X_LANES), lambda bi, ti: (bi, ti, 0))],
        out_shape=[jax.ShapeDtypeStruct((b * t * ROW_TILE, V7X_LANES), F32),
                   jax.ShapeDtypeStruct((b, t, V7X_LANES), F32)],
        scratch_shapes=[pltpu.VMEM((1, V7X_LANES), F32)],
        compiler_params=_cparams(("arbitrary", "arbitrary")),
        name="post_mixer",
    )(x, mix, u, kbd, vbd, wom, woa, g.reshape(1, d), w_rt, b_rt)


def _memory_kv_blockdiag(mkv):
    k, v = jnp.split(mkv, 2, axis=-1)
    head_of = np.arange(MEM_DIM) // HEAD_DIM
    sel = jnp.asarray(head_of[None, :] == np.arange(MEM_HEADS)[:, None], F32)
    kbd = jnp.einsum('bmd,hd->bdhm', k, sel).reshape(k.shape[0], MEM_DIM, -1)
    vbd = jnp.einsum('bmd,hd->bhmd', v, sel).reshape(v.shape[0], -1, MEM_DIM)
    return kbd.astype(BF16), vbd.astype(BF16)


FFN_X_SLOTS = 4
FFN_Y_SLOTS = 2
FFN_W_SLOTS = 2


def _moe_kernel(nblk_ref, bexp_ref, bfirst_ref, nexp_ref, wslot_ref, g_ref, w1_ref, w3_ref, w2_ref, xs_ref, y_ref,
                wf1, wf3, wf2, wb1, wb3, wb2, xbuf, ybuf, wsem, xsem, ysem, *, layer):
    del xs_ref
    n_blk = nblk_ref[0]

    def rows_of(blk):
        return _row_slice(blk * MOE_BLOCK, MOE_BLOCK)

    def fetch(blk):
        slot = blk % FFN_X_SLOTS
        return pltpu.make_async_copy(y_ref.at[rows_of(blk)], xbuf.at[slot], xsem.at[slot])

    def writeback(blk):
        slot = blk % FFN_Y_SLOTS
        return pltpu.make_async_copy(ybuf.at[slot], y_ref.at[rows_of(blk)], ysem.at[slot])

    def weight_copies(expert, slot):
        return [pltpu.make_async_copy(src.at[layer, expert], dst.at[slot], wsem.at[slot, j])
                for j, (src, dst) in enumerate(((w1_ref, wf1), (w3_ref, wf3), (w2_ref, wf2)))]

    for cp in weight_copies(bexp_ref[0], 0):
        cp.start()
    for ahead in range(FFN_X_SLOTS - 1):
        @pl.when(ahead < n_blk)
        def _(ahead=ahead):
            fetch(ahead).start()

    def per_block(blk, carry):
        @pl.when(blk + FFN_X_SLOTS - 1 < n_blk)
        def _():
            fetch(blk + FFN_X_SLOTS - 1).start()

        @pl.when(bfirst_ref[blk] != 0)
        def _():
            slot = wslot_ref[blk]

            @pl.when(nexp_ref[blk] >= 0)
            def _():
                for cp in weight_copies(nexp_ref[blk], 1 - slot):
                    cp.start()

            for cp in weight_copies(bexp_ref[blk], slot):
                cp.wait()
            wb1[...] = wf1[slot].astype(BF16)
            wb3[...] = wf3[slot].astype(BF16)
            wb2[...] = wf2[slot].astype(BF16)

        fetch(blk).wait()

        @pl.when(blk >= FFN_Y_SLOTS)
        def _():
            writeback(blk - FFN_Y_SLOTS).wait()

        xb = _rms_norm_rows(_load_row_tiles(xbuf.at[blk % FFN_X_SLOTS], MOE_BLOCK), g_ref[...]).astype(BF16)
        half = wb1.shape[1] // 2
        up = [(jnp.dot(xb, wb1[:, hf * half:(hf + 1) * half], preferred_element_type=F32),
               jnp.dot(xb, wb3[:, hf * half:(hf + 1) * half], preferred_element_type=F32)) for hf in range(2)]
        y = None
        for hf, (a, c) in enumerate(up):
            hid = (a / (1.0 + jnp.exp(-a)) * c).astype(BF16)
            part = jnp.dot(hid, wb2[hf * half:(hf + 1) * half, :], preferred_element_type=F32)
            y = part if y is None else y + part
        _store_row_tiles(ybuf.at[blk % FFN_Y_SLOTS], y)
        writeback(blk).start()
        return carry

    lax.fori_loop(0, n_blk, per_block, 0)
    for back in range(FFN_Y_SLOTS, 0, -1):
        @pl.when(n_blk >= back)
        def _(back=back):
            writeback(n_blk - back).wait()


def _moe_ffn(xs, sched, g, w1, w3, w2, layer):
    d = w1.shape[2]
    ff = w1.shape[3]
    blk_rows = MOE_BLOCK * ROW_TILE
    any_spec = pl.BlockSpec(memory_space=pl.ANY)
    grid_spec = pltpu.PrefetchScalarGridSpec(
        num_scalar_prefetch=5,
        grid=(1,),
        in_specs=[pl.BlockSpec((1, d), lambda i, *_: (0, 0)), any_spec, any_spec, any_spec, any_spec],
        out_specs=any_spec,
        scratch_shapes=[pltpu.VMEM((FFN_W_SLOTS, d, ff), F32), pltpu.VMEM((FFN_W_SLOTS, d, ff), F32),
                        pltpu.VMEM((FFN_W_SLOTS, ff, d), F32),
                        pltpu.VMEM((d, ff), BF16), pltpu.VMEM((d, ff), BF16), pltpu.VMEM((ff, d), BF16),
                        pltpu.VMEM((FFN_X_SLOTS, blk_rows, V7X_LANES), F32),
                        pltpu.VMEM((FFN_Y_SLOTS, blk_rows, V7X_LANES), F32),
                        pltpu.SemaphoreType.DMA((FFN_W_SLOTS, 3)),
                        pltpu.SemaphoreType.DMA((FFN_X_SLOTS,)),
                        pltpu.SemaphoreType.DMA((FFN_Y_SLOTS,))],
    )
    return pl.pallas_call(
        functools.partial(_moe_kernel, layer=layer),
        grid_spec=grid_spec,
        out_shape=jax.ShapeDtypeStruct(xs.shape, F32),
        input_output_aliases={9: 0},
        compiler_params=_cparams(("arbitrary",)),
        name="moe_ffn",
    )(sched["nblk"], sched["bexp"], sched["bfirst"], sched["nexp"], sched["wslot"], g.reshape(1, d), w1, w3, w2, xs)


def _row_slice(row, n_rows):
    return pl.ds(pl.multiple_of(row * ROW_TILE, ROW_TILE), n_rows * ROW_TILE)


def _pow2_pieces(n, largest, act):
    k = largest
    while k >= 1:
        shift = k.bit_length()

        @pl.when((n & k) != 0)
        def _(k=k, shift=shift):
            act((n >> shift) << shift, k)
        k //= 2


def _run_copies(runs_ref, make_copy, act):
    def per_expert(e, carry):
        n = runs_ref[0, 0, e]
        local_row = runs_ref[0, 1, e]
        sorted_row = runs_ref[0, 2, e]

        def chunk(i, c):
            act(make_copy(local_row + 8 * i, sorted_row + 8 * i, 8))
            return c

        lax.fori_loop(0, n >> 3, chunk, 0)
        _pow2_pieces(n, 4, lambda off, k: act(make_copy(local_row + off, sorted_row + off, k)))
        return carry

    lax.fori_loop(0, N_EXPERTS, per_expert, 0)


def _dispatch_kernel(seg_ref, lpos_ref, runs_ref, h_ref, xs_ref, local, zbuf, run_sem, sem):
    tm = h_ref.shape[0] // ROW_TILE
    n_rows = xs_ref.shape[0] // ROW_TILE
    zrows = zbuf.shape[0] // ROW_TILE
    step = pl.program_id(0)
    cur = step % 2

    def place(tok, carry):
        row = h_ref[_row_slice(tok, 1), :]
        for k in range(TOP_K):
            local[cur, _row_slice(lpos_ref[0, 0, k * tm + tok], 1), :] = row
        return carry

    lax.fori_loop(0, tm, place, 0, unroll=8)

    def to_sorted(local_row, sorted_row, n):
        return pltpu.make_async_copy(local.at[cur, _row_slice(local_row, n)],
                                     xs_ref.at[_row_slice(sorted_row, n)], run_sem.at[cur])

    def wait_slot(slot):
        pltpu.make_async_copy(local.at[slot], xs_ref.at[pl.ds(0, local.shape[1])], run_sem.at[slot]).wait()

    _run_copies(runs_ref, to_sorted, lambda cp: cp.start())

    @pl.when(step > 0)
    def _():
        wait_slot(1 - cur)

    @pl.when(step == pl.num_programs(0) - 1)
    def _():
        wait_slot(cur)

    @pl.when(pl.program_id(0) == pl.num_programs(0) - 1)
    def _():
        zbuf[...] = jnp.zeros_like(zbuf)

        def zero_copy(first_row, k):
            return pltpu.make_async_copy(zbuf.at[_row_slice(0, k)], xs_ref.at[_row_slice(first_row, k)], sem)

        def pad_copies(act):
            def per_expert(e, carry):
                cnt = seg_ref[0, e]
                first = seg_ref[1, e] + cnt
                _pow2_pieces((-cnt) & (MOE_BLOCK - 1), zrows, lambda off, k: act(zero_copy(first + off, k)))
                return carry

            lax.fori_loop(0, N_EXPERTS, per_expert, 0)
            used = seg_ref[2, 0]

            def tail(i, carry):
                act(zero_copy(used + i * zrows, zrows))
                return carry

            lax.fori_loop(0, (n_rows - used) // zrows, tail, 0)

        pad_copies(lambda cp: cp.start())
        pad_copies(lambda cp: cp.wait())


def _dispatch(x_tiles, seg, lpos, runs, n_rows, *, tm):
    n_tok = x_tiles.shape[0] // ROW_TILE
    nt = n_tok // tm
    grid_spec = pltpu.PrefetchScalarGridSpec(
        num_scalar_prefetch=1,
        grid=(nt,),
        in_specs=[pl.BlockSpec((1, 1, TOP_K * tm), lambda i, c: (i, 0, 0), memory_space=pltpu.SMEM),
                  pl.BlockSpec((1, 3, N_EXPERTS), lambda i, c: (i, 0, 0), memory_space=pltpu.SMEM),
                  pl.BlockSpec((tm * ROW_TILE, V7X_LANES), lambda i, c: (i, 0))],
        out_specs=pl.BlockSpec(memory_space=pl.ANY),
        scratch_shapes=[pltpu.VMEM((2, TOP_K * tm * ROW_TILE, V7X_LANES), F32),
                        pltpu.VMEM((MOE_BLOCK // 2 * ROW_TILE, V7X_LANES), F32),
                        pltpu.SemaphoreType.DMA((2,)),
                        pltpu.SemaphoreType.DMA(())],
    )
    return pl.pallas_call(
        _dispatch_kernel,
        grid_spec=grid_spec,
        out_shape=jax.ShapeDtypeStruct((n_rows * ROW_TILE, V7X_LANES), F32),
        compiler_params=_cparams(("arbitrary",)),
        name="moe_dispatch",
    )(seg, lpos, runs, x_tiles)


def _combine_kernel(*refs, final):
    if final:
        lpos_ref, gate_ref, runs_ref, next_runs_ref, x1_ref, yb_ref, g_ref, o_ref, local, acc, sem = refs
    else:
        lpos_ref, gate_ref, runs_ref, next_runs_ref, x1_ref, yb_ref, o_ref, local, acc, sem = refs
    tm = x1_ref.shape[0] // ROW_TILE
    step = pl.program_id(0)
    cur = step % 2

    def from_sorted(slot):
        def make(local_row, sorted_row, n):
            return pltpu.make_async_copy(yb_ref.at[_row_slice(sorted_row, n)],
                                         local.at[slot, _row_slice(local_row, n)], sem.at[slot])
        return make

    @pl.when(step == 0)
    def _():
        _run_copies(runs_ref, from_sorted(cur), lambda cp: cp.start())

    @pl.when(step + 1 < pl.num_programs(0))
    def _():
        _run_copies(next_runs_ref, from_sorted(1 - cur), lambda cp: cp.start())

    pltpu.make_async_copy(yb_ref.at[pl.ds(0, local.shape[1])], local.at[cur], sem.at[cur]).wait()

    def mix(tok, carry):
        y = jnp.zeros((ROW_TILE, V7X_LANES), F32)
        for k in range(TOP_K):
            idx = k * tm + tok
            y = y + gate_ref[0, 0, idx] * local[cur, _row_slice(lpos_ref[0, 0, idx], 1), :]
        acc[_row_slice(tok, 1), :] = x1_ref[_row_slice(tok, 1), :] + y
        return carry

    lax.fori_loop(0, tm, mix, 0, unroll=8)
    x2 = _load_row_tiles(acc, tm)
    if final:
        x2 = _rms_norm_rows(x2, g_ref[...])
    o_ref[...] = x2


def _combine(x1_tiles, yb, lpos, gates, runs, *, tm, final_g=None):
    n_tok, d = x1_tiles.shape[0] // ROW_TILE, D_MODEL
    nt = n_tok // tm
    final = final_g is not None
    in_specs = [pl.BlockSpec((1, 1, TOP_K * tm), lambda i: (i, 0, 0), memory_space=pltpu.SMEM),
                pl.BlockSpec((1, 1, TOP_K * tm), lambda i: (i, 0, 0), memory_space=pltpu.SMEM),
                pl.BlockSpec((1, 3, N_EXPERTS), lambda i: (i, 0, 0), memory_space=pltpu.SMEM),
                pl.BlockSpec((1, 3, N_EXPERTS), lambda i: (jnp.minimum(i + 1, nt - 1), 0, 0),
                             memory_space=pltpu.SMEM),
                pl.BlockSpec((tm * ROW_TILE, V7X_LANES), lambda i: (i, 0)),
                pl.BlockSpec(memory_space=pl.ANY)]
    args = [lpos, gates, runs, runs, x1_tiles, yb]
    if final:
        in_specs.append(pl.BlockSpec((1, d), lambda i: (0, 0)))
        args.append(final_g.reshape(1, d))
    return pl.pallas_call(
        functools.partial(_combine_kernel, final=final),
        grid=(nt,),
        in_specs=in_specs,
        out_specs=pl.BlockSpec((tm, d), lambda i: (i, 0)),
        out_shape=jax.ShapeDtypeStruct((n_tok, d), F32),
        scratch_shapes=[pltpu.VMEM((2, TOP_K * tm * ROW_TILE, V7X_LANES), F32),
                        pltpu.VMEM((tm * ROW_TILE, V7X_LANES), F32),
                        pltpu.SemaphoreType.DMA((2,))],
        compiler_params=_cparams(("arbitrary",)),
        name="moe_combine_final" if final else "moe_combine",
    )(*args)


def _moe_plan(rt, n_tok, tm):
    nt = n_tok // tm
    rec = rt[:, :ROW_TILE].T.reshape(ROW_TILE, nt, tm)
    expert = rec[0:TOP_K].astype(jnp.int32)
    gate = rec[TOP_K:2 * TOP_K]
    pos = rec[2 * TOP_K:3 * TOP_K].astype(jnp.int32)
    is_e = expert[None] == jnp.arange(N_EXPERTS, dtype=jnp.int32)[:, None, None, None]
    cnt_tile = jnp.sum(is_e, axis=(1, 3), dtype=jnp.int32).T
    seen_before = jnp.cumsum(cnt_tile, axis=0) - cnt_tile
    local_start = jnp.cumsum(cnt_tile, axis=1) - cnt_tile
    counts = jnp.sum(cnt_tile, axis=0)
    shift = (local_start - seen_before).T
    lpos = pos + jnp.sum(jnp.where(is_e, shift[:, None, :, None], 0), axis=0)
    by_tile = lambda a: a.transpose(1, 0, 2).reshape(nt, 1, TOP_K * tm)
    nblk = (counts + MOE_BLOCK - 1) // MOE_BLOCK
    bend = jnp.cumsum(nblk)
    seg_start = (bend - nblk) * MOE_BLOCK
    sorted_start = seg_start[None, :] + seen_before
    runs = jnp.stack([cnt_tile, local_start, sorted_start], axis=1)
    seg = jnp.stack([counts, seg_start, jnp.full((N_EXPERTS,), bend[-1] * MOE_BLOCK, jnp.int32)])
    n_steps = n_tok * TOP_K // MOE_BLOCK + N_EXPERTS
    step = jnp.arange(n_steps, dtype=jnp.int32)
    bexp = jnp.sum(jnp.minimum(step, bend[-1] - 1)[:, None] >= bend[None, :], axis=1).astype(jnp.int32)
    bfirst = jnp.concatenate([jnp.ones((1,), jnp.int32), (bexp[1:] != bexp[:-1]).astype(jnp.int32)])
    wslot = (jnp.cumsum(bfirst) - 1) % FFN_W_SLOTS
    switch_at = jnp.where(bfirst != 0, step, n_steps)
    next_switch = jnp.concatenate([lax.cummin(switch_at[::-1])[::-1][1:], jnp.full((1,), n_steps, jnp.int32)])
    nexp = jnp.sum(jnp.where(next_switch[:, None] == step[None, :], bexp[None, :] + 1, 0), axis=1) - 1
    sched = dict(nblk=bend[-1:].astype(jnp.int32), bexp=bexp, bfirst=bfirst,
                 nexp=nexp.astype(jnp.int32), wslot=wslot.astype(jnp.int32))
    return dict(seg=seg.astype(jnp.int32), n_rows=n_steps * MOE_BLOCK, sched=sched,
                lpos=by_tile(lpos), gates=by_tile(gate), runs=runs)


def _moe_layer(x1_tiles, rt, norm_g, w1, w3, w2, layer, *, final_g=None):
    n_tok = x1_tiles.shape[0] // ROW_TILE
    tm = 1024
    plan = _moe_plan(rt, n_tok, tm)
    xs = _dispatch(x1_tiles, plan["seg"], plan["lpos"], plan["runs"], plan["n_rows"], tm=tm)
    yb = _moe_ffn(xs, plan["sched"], norm_g, w1, w3, w2, layer)
    return _combine(x1_tiles, yb, plan["lpos"], plan["gates"], plan["runs"], tm=tm, final_g=final_g)


def kernel(x, mem, mem_norm, final_norm, norm_mix, norm_ffn, w_mem_kv, w_out, na_w_in, na_rpb, gla_w_in,
           gla_gate_up, gla_gate_bias, gla_out_norm, moe_w_group, moe_b_group, moe_w_router, moe_b_router,
           moe_w1, moe_w3, moe_w2):
    b, t, d = x.shape
    n = b * t
    depth = norm_mix.shape[0]
    n_mem = mem.shape[1]
    xf = x.reshape(n, d)
    for i in range(depth):
        j = i // 2
        mkv = _norm_matmul(mem.reshape(b * n_mem, d), mem_norm, w_mem_kv[i].astype(BF16),
                           tm=256, out_dtype=F32, name="mem_kv_proj").reshape(b, n_mem, 2 * MEM_DIM)
        kbd, vbd = _memory_kv_blockdiag(mkv)
        if i % 2 == 0:
            u = _norm_matmul(xf, norm_mix[i], na_w_in[j].astype(BF16), tm=512, out_dtype=BF16,
                             name="na_in_proj").reshape(b, t, -1)
            mix = _na_attention(u, na_rpb[j])
            mq_off = NA_MQ_OFF
            w_o = w_out[i].astype(BF16)
            w_o = (w_o[:MIX_DIM], w_o[MIX_DIM:])
        else:
            w_pad, gu, gb, onorm = _gla_weights(gla_w_in[j], gla_gate_up[j], gla_gate_bias[j], gla_out_norm[j])
            u = _norm_matmul(xf, norm_mix[i], w_pad, tm=512, out_dtype=BF16, name="gla_in_proj").reshape(b, t, -1)
            ofwd = _gla_direction(u, gu[0], gb[0], reverse=False, tb=GLA_TIME_BLOCK)
            mix = _gla_direction(u, gu[1], gb[1], reverse=True, tb=GLA_TIME_BLOCK, ofwd=ofwd, onorm=onorm)
            mq_off = GLA_MQ_OFF
            w_mix = w_out[i][:MIX_DIM].reshape(GLA_HEADS, GLA_DV, d)
            w_mix = jnp.pad(w_mix, ((0, 0), (0, GLA_DV_PAD - GLA_DV), (0, 0))).reshape(GLA_MIX_PAD, d)
            w_o = (w_mix.astype(BF16), w_out[i][MIX_DIM:].astype(BF16))
        w_rt = jnp.pad(jnp.concatenate([moe_w_group[i], moe_w_router[i]], axis=1),
                       ((0, 0), (0, V7X_LANES - N_GROUPS - N_EXPERTS))).astype(F32)
        w_rt = jnp.concatenate(_split_bf16(w_rt, 2), axis=1)
        b_rt = jnp.pad(jnp.concatenate([moe_b_group[i], moe_b_router[i]]),
                       (0, V7X_LANES - N_GROUPS - N_EXPERTS)).reshape(1, V7X_LANES)
        x1_tiles, rt = _post_mixer(xf.reshape(b, t, d), mix, u, mq_off, kbd, vbd, w_o, norm_ffn[i],
                                   w_rt, b_rt.astype(F32), tm=POST_TILE)
        xf = _moe_layer(x1_tiles, rt.reshape(n, V7X_LANES), norm_ffn[i],
                        moe_w1, moe_w3, moe_w2, i,
                        final_g=final_norm if i == depth - 1 else None)
    return xf.reshape(b, t, d)
```

```python
import functools

import numpy as np
import jax
import jax.numpy as jnp
from jax import lax
from jax.experimental import pallas as pl
from jax.experimental.pallas import tpu as pltpu

F32 = jnp.float32
BF16 = jnp.bfloat16

D_MODEL = 1024
GRID_W = 64
HEAD_DIM = 64
NORM_EPS = 1e-6
MEM_HEADS = 4
MEM_DIM = MEM_HEADS * HEAD_DIM
MIX_DIM = D_MODEL - MEM_DIM
NA_HEADS = MIX_DIM // HEAD_DIM
NA_WIN_H = 8
NA_WIN_W = 16
GLA_HEADS = 4
GLA_DK = MIX_DIM // 2 // GLA_HEADS
GLA_DV = MIX_DIM // GLA_HEADS
GLA_GATE_RANK = 16
GLA_TAU = 16.0
GLA_CHUNK = 64
N_GROUPS = 4
EXPERTS_PER_GROUP = 8
N_EXPERTS = N_GROUPS * EXPERTS_PER_GROUP
TOP_K = 2
EXPERT_FF = 512
MOE_BLOCK = 256

V7X_LANES = 128
V7X_MXU_DIM = 256
V7X_VMEM_LIMIT_BYTES = 56 * 1024 * 1024

GLA_DK_PAD = 128
GLA_DV_PAD = 256
GLA_Q_OFF = 0
GLA_K_OFF = GLA_Q_OFF + GLA_HEADS * GLA_DK_PAD
GLA_V_OFF = GLA_K_OFF + GLA_HEADS * GLA_DK_PAD
GLA_G_OFF = GLA_V_OFF + GLA_HEADS * GLA_DV_PAD
GLA_R_OFF = GLA_G_OFF + GLA_HEADS * GLA_DV_PAD
GLA_R_PAD = 256
GLA_MQ_OFF = GLA_R_OFF + GLA_R_PAD
GLA_IN_PAD = GLA_MQ_OFF + MEM_DIM
GLA_MIX_PAD = GLA_HEADS * GLA_DV_PAD
NA_MQ_OFF = 3 * MIX_DIM
NA_ROWS_PER_STEP = 8
IN_PROJ_TILE = 1024
MOVE_UNROLL = 16
GLA_TIME_BLOCK = 1024
POST_TILE = 1024
POST_SUBTILES = 4


def _cparams(semantics):
    return pltpu.CompilerParams(dimension_semantics=semantics, vmem_limit_bytes=V7X_VMEM_LIMIT_BYTES)


def _split_bf16(x, n_pieces):
    pieces = []
    rest = x
    for _ in range(n_pieces - 1):
        c = rest * 65537.0
        hi = c - (c - rest)
        pieces.append(hi.astype(BF16))
        rest = rest - hi
    pieces.append(rest.astype(BF16))
    return pieces


ROW_TILE = D_MODEL // V7X_LANES


def _store_row_tiles(ref, val):
    rows = val.shape[0]
    for s in range(ROW_TILE):
        ref[pl.ds(s, rows, stride=ROW_TILE), :] = val[:, s * V7X_LANES:(s + 1) * V7X_LANES]


def _load_row_tiles(ref, rows):
    return jnp.concatenate([ref[pl.ds(s, rows, stride=ROW_TILE), :] for s in range(ROW_TILE)], axis=1)


def _norm_matmul_kernel(x_ref, g_ref, w_ref, o_ref, *, col_chunk):
    x = x_ref[...]
    ms = jnp.mean(x * x, axis=-1, keepdims=True)
    y = (x * lax.rsqrt(ms + NORM_EPS) * g_ref[...]).astype(BF16)
    n_out = o_ref.shape[1]
    for c in range(0, n_out, col_chunk):
        o_ref[:, c:c + col_chunk] = jnp.dot(
            y, w_ref[:, c:c + col_chunk], preferred_element_type=F32).astype(o_ref.dtype)


def _norm_matmul(x, g, w, *, tm, out_dtype, name):
    n, d = x.shape
    n_out = w.shape[1]
    col_chunk = 512 if n_out % 512 == 0 else n_out
    return pl.pallas_call(
        functools.partial(_norm_matmul_kernel, col_chunk=col_chunk),
        grid=(n // tm,),
        in_specs=[pl.BlockSpec((tm, d), lambda i: (i, 0)),
                  pl.BlockSpec((1, d), lambda i: (0, 0)),
                  pl.BlockSpec((d, n_out), lambda i: (0, 0))],
        out_specs=pl.BlockSpec((tm, n_out), lambda i: (i, 0)),
        out_shape=jax.ShapeDtypeStruct((n, n_out), out_dtype),
        compiler_params=_cparams(("parallel",)),
        name=name,
    )(x, g.reshape(1, d), w)


def _na_kernel(q_ref, k_ref, v_ref, bias_ref, o_ref):
    rows = k_ref.shape[1] // GRID_W
    rows_per_step = q_ref.shape[1] // GRID_W
    n_keys = NA_WIN_H * GRID_W
    heads_per_slab = V7X_MXU_DIM // HEAD_DIM
    scale = HEAD_DIM ** -0.5
    lane_head = lax.broadcasted_iota(jnp.int32, (GRID_W, V7X_MXU_DIM), 1) // HEAD_DIM
    slabs = [slice(s * V7X_MXU_DIM, (s + 1) * V7X_MXU_DIM) for s in range(MIX_DIM // V7X_MXU_DIM)]

    def window_row(rr):
        r = pl.program_id(1) * rows_per_step + rr
        rs = jnp.clip(r - NA_WIN_H // 2, 0, rows - NA_WIN_H)
        return rs, rs - r + NA_WIN_H - 1

    def window_start(rr):
        return pl.multiple_of(window_row(rr)[0] * GRID_W, GRID_W)

    def scores(rr, cs):
        qq = q_ref[0, rr * GRID_W:(rr + 1) * GRID_W, cs] * scale
        kw = k_ref[0, pl.ds(window_start(rr), n_keys), cs]
        lhs = jnp.concatenate(
            [jnp.where(lane_head == i, qq, jnp.zeros_like(qq)) for i in range(heads_per_slab)], axis=0)
        sc = lax.dot_general(lhs, kw, (((1,), (1,)), ((), ())), preferred_element_type=F32)
        return sc + bias_ref[window_row(rr)[1], cs, :]

    units = [(rr, cs) for rr in range(rows_per_step) for cs in slabs]
    nxt = scores(*units[0])
    for ui, (rr, cs) in enumerate(units):
        sc = nxt
        if ui + 1 < len(units):
            nxt = scores(*units[ui + 1])
        vw = v_ref[0, pl.ds(window_start(rr), n_keys), cs]
        m = jnp.max(sc, axis=-1, keepdims=True)
        p = jnp.exp(sc - m)
        l = jnp.sum(p, axis=-1, keepdims=True)
        o = jnp.dot(p.astype(BF16), vw, preferred_element_type=F32)
        o = o * (1.0 / l)
        acc = jnp.zeros((GRID_W, V7X_MXU_DIM), F32)
        for i in range(heads_per_slab):
            acc = acc + jnp.where(lane_head == i, o[i * GRID_W:(i + 1) * GRID_W], 0.0)
        o_ref[0, rr * GRID_W:(rr + 1) * GRID_W, cs] = acc.astype(o_ref.dtype)


def _na_bias_table(rpb):
    qc = np.arange(GRID_W)[:, None]
    kc = np.arange(GRID_W)[None, :]
    cstart = np.clip(qc - NA_WIN_W // 2, 0, GRID_W - NA_WIN_W)
    col_in = (kc >= cstart) & (kc < cstart + NA_WIN_W)
    dcol = np.clip(kc - qc, 1 - NA_WIN_W, NA_WIN_W - 1) + NA_WIN_W - 1
    pick = jnp.asarray(dcol[None] == np.arange(2 * NA_WIN_W - 1)[:, None, None], F32)
    per_row = jnp.einsum('hrd,dqk->hrqk', rpb.astype(F32), pick, precision=lax.Precision.HIGHEST)
    per_row = jnp.where(col_in[None, None], per_row, -jnp.inf)
    tbl = jnp.stack([per_row[:, o:o + NA_WIN_H] for o in range(NA_WIN_H)])
    return tbl.transpose(0, 1, 3, 2, 4).reshape(NA_WIN_H, NA_HEADS * GRID_W, NA_WIN_H * GRID_W)


def _na_attention(u, rpb):
    b, t, _ = u.shape
    rows = t // GRID_W
    bias = _na_bias_table(rpb)

    qt = NA_ROWS_PER_STEP * GRID_W
    return pl.pallas_call(
        _na_kernel,
        grid=(b, rows // NA_ROWS_PER_STEP),
        in_specs=[pl.BlockSpec((1, qt, MIX_DIM), lambda bi, g: (bi, g, 0)),
                  pl.BlockSpec((1, t, MIX_DIM), lambda bi, g: (bi, 0, 1)),
                  pl.BlockSpec((1, t, MIX_DIM), lambda bi, g: (bi, 0, 2)),
                  pl.BlockSpec(bias.shape, lambda bi, g: (0, 0, 0), pipeline_mode=pl.Buffered(1))],
        out_specs=pl.BlockSpec((1, qt, MIX_DIM), lambda bi, g: (bi, g, 0)),
        out_shape=jax.ShapeDtypeStruct((b, t, MIX_DIM), BF16),
        compiler_params=_cparams(("parallel", "arbitrary")),
        name="na_attention",
    )(u, u, u, bias)


def _gla_kernel(*refs, reverse, final):
    if final:
        (q_ref, k_ref, v_ref, r_ref, gu_ref, gb_ref, ofwd_ref, g_ref, onorm_ref, o_ref, st_ref) = refs
    else:
        (q_ref, k_ref, v_ref, r_ref, gu_ref, gb_ref, o_ref, st_ref) = refs
    c = GLA_CHUNK

    @pl.when(pl.program_id(1) == 0)
    def _():
        st_ref[...] = jnp.zeros_like(st_ref)

    z = jnp.dot(r_ref[0], gu_ref[...], preferred_element_type=F32) + gb_ref[...]
    la = (jnp.minimum(z, 0.0) - jnp.log(1.0 + jnp.exp(-jnp.abs(z)))) * (1.0 / GLA_TAU)
    ri = lax.broadcasted_iota(jnp.int32, (c, c), 0)
    ci = lax.broadcasted_iota(jnp.int32, (c, c), 1)
    tri = (ci >= ri) if reverse else (ci <= ri)
    trib = jnp.where(tri, 1.0, 0.0).astype(BF16)
    la3 = jnp.concatenate(_split_bf16(la, 3), axis=1)
    gw = la.shape[1]
    mid = c // 2 if reverse else c // 2 - 1
    last = 0 if reverse else c - 1
    scale = GLA_DK ** -0.5
    n_chunks = q_ref.shape[1] // c
    order = range(n_chunks - 1, -1, -1) if reverse else range(n_chunks)
    state = [st_ref[h] for h in range(GLA_HEADS)]
    heads = range(GLA_HEADS)
    hsl = [slice(h * GLA_DK_PAD, (h + 1) * GLA_DK_PAD) for h in heads]
    vsl = [slice(h * GLA_DV_PAD, (h + 1) * GLA_DV_PAD) for h in heads]
    nt_dims = (((1,), (1,)), ((), ()))
    tn_dims = (((0,), (0,)), ((), ()))

    def prepare(ch):
        sl = slice(ch * c, (ch + 1) * c)
        b3 = jnp.dot(trib, la3[sl], preferred_element_type=F32)
        bcum = b3[:, :gw] + b3[:, gw:2 * gw] + b3[:, 2 * gw:]
        b_mid = bcum[mid:mid + 1]
        b_last = bcum[last:last + 1]
        qc = q_ref[0, sl, :].astype(F32) * scale
        kc = k_ref[0, sl, :].astype(F32)
        return dict(sl=sl, vc=v_ref[0, sl, :],
                    qe=(qc * jnp.exp(bcum - b_mid)).astype(BF16), ke=(kc * jnp.exp(b_mid - bcum)).astype(BF16),
                    qs=(qc * jnp.exp(bcum)).astype(BF16), ks=(kc * jnp.exp(b_last - bcum)).astype(BF16),
                    dec=jnp.exp(b_last))

    order = list(order)
    nxt = prepare(order[0])
    for pos_in_step, ch in enumerate(order):
        cur = nxt
        if pos_in_step + 1 < len(order):
            nxt = prepare(order[pos_in_step + 1])
        sl, vc, dec = cur["sl"], cur["vc"], cur["dec"]
        scores = [lax.dot_general(cur["qe"][:, hsl[h]], cur["ke"][:, hsl[h]], nt_dims, preferred_element_type=F32)
                  for h in heads]
        kv_t = [lax.dot_general(vc[:, vsl[h]], cur["ks"][:, hsl[h]], tn_dims, preferred_element_type=F32)
                for h in heads]
        o_inter = [lax.dot_general(cur["qs"][:, hsl[h]], state[h].astype(BF16), nt_dims, preferred_element_type=F32)
                   for h in heads]
        masked = [jnp.where(tri, scores[h], 0.0).astype(BF16) for h in heads]
        o_intra = [jnp.dot(masked[h], vc[:, vsl[h]], preferred_element_type=F32) for h in heads]
        for h in heads:
            hs, vs = hsl[h], vsl[h]
            o = o_intra[h] + o_inter[h]
            state[h] = state[h] * dec[:, hs] + kv_t[h]
            if final:
                tot = ofwd_ref[0, sl, vs] + o
                ms = jnp.sum(tot * tot, axis=-1, keepdims=True) * (1.0 / GLA_DV)
                y = tot * lax.rsqrt(ms + NORM_EPS) * onorm_ref[:, vs]
                g = g_ref[0, sl, vs].astype(F32)
                o_ref[0, sl, vs] = (y * (g / (1.0 + jnp.exp(-g)))).astype(o_ref.dtype)
            else:
                o_ref[0, sl, vs] = o
    for h in range(GLA_HEADS):
        st_ref[h] = state[h]


def _gla_direction(u, gu, gb, *, reverse, tb, ofwd=None, onorm=None):
    b, t, _ = u.shape
    nt = t // tb
    final = ofwd is not None
    tix = (lambda ti: nt - 1 - ti) if reverse else (lambda ti: ti)
    qw = GLA_HEADS * GLA_DK_PAD
    vw = GLA_HEADS * GLA_DV_PAD
    in_specs = [pl.BlockSpec((1, tb, qw), lambda bi, ti: (bi, tix(ti), GLA_Q_OFF // qw)),
                pl.BlockSpec((1, tb, qw), lambda bi, ti: (bi, tix(ti), GLA_K_OFF // qw)),
                pl.BlockSpec((1, tb, vw), lambda bi, ti: (bi, tix(ti), GLA_V_OFF // vw)),
                pl.BlockSpec((1, tb, GLA_R_PAD), lambda bi, ti: (bi, tix(ti), GLA_R_OFF // GLA_R_PAD)),
                pl.BlockSpec((GLA_R_PAD, qw), lambda bi, ti: (0, 0)),
                pl.BlockSpec((1, qw), lambda bi, ti: (0, 0))]
    args = [u, u, u, u, gu, gb]
    if final:
        in_specs += [pl.BlockSpec((1, tb, vw), lambda bi, ti: (bi, tix(ti), 0)),
                     pl.BlockSpec((1, tb, vw), lambda bi, ti: (bi, tix(ti), GLA_G_OFF // vw)),
                     pl.BlockSpec((1, vw), lambda bi, ti: (0, 0))]
        args += [ofwd, u, onorm]
    return pl.pallas_call(
        functools.partial(_gla_kernel, reverse=reverse, final=final),
        grid=(b, nt),
        in_specs=in_specs,
        out_specs=pl.BlockSpec((1, tb, vw), lambda bi, ti: (bi, tix(ti), 0)),
        out_shape=jax.ShapeDtypeStruct((b, t, vw), BF16 if final else F32),
        scratch_shapes=[pltpu.VMEM((GLA_HEADS, GLA_DV_PAD, GLA_DK_PAD), F32)],
        compiler_params=_cparams(("parallel", "arbitrary")),
        name="gla_bwd_final" if final else "gla_fwd",
    )(*args)


def _gla_weights(w_in, gate_up, gate_bias, out_norm):
    d = w_in.shape[0]
    kd = GLA_HEADS * GLA_DK
    wq, wk, wv, wg, wr, wmq = jnp.split(
        w_in, np.cumsum([kd, kd, MIX_DIM, MIX_DIM, 2 * GLA_GATE_RANK]), axis=1)

    def pad_heads(w, dh, dh_pad):
        w = w.reshape(w.shape[0], GLA_HEADS, dh)
        w = jnp.pad(w, ((0, 0), (0, 0), (0, dh_pad - dh)))
        return w.reshape(w.shape[0], GLA_HEADS * dh_pad)

    w_pad = jnp.concatenate([
        pad_heads(wq, GLA_DK, GLA_DK_PAD), pad_heads(wk, GLA_DK, GLA_DK_PAD),
        pad_heads(wv, GLA_DV, GLA_DV_PAD), pad_heads(wg, GLA_DV, GLA_DV_PAD),
        jnp.pad(wr, ((0, 0), (0, GLA_R_PAD - 2 * GLA_GATE_RANK))), wmq], axis=1)
    gu = []
    for di in range(2):
        up = pad_heads(gate_up[di], GLA_DK, GLA_DK_PAD)
        gu.append(jnp.pad(up, ((di * GLA_GATE_RANK, GLA_R_PAD - (di + 1) * GLA_GATE_RANK), (0, 0))))
    gb = [pad_heads(gate_bias[di][None, :], GLA_DK, GLA_DK_PAD) for di in range(2)]
    onorm = pad_heads(out_norm[None, :].repeat(GLA_HEADS, 0).reshape(1, MIX_DIM), GLA_DV, GLA_DV_PAD)
    return w_pad.astype(BF16), [g.astype(BF16) for g in gu], [x.astype(F32) for x in gb], onorm.astype(F32)


def _post_kernel(x_ref, mix_ref, mq_ref, kbd_ref, vbd_ref, wom_ref, woa_ref, g_ref, wrt_ref, brt_ref,
                 x1_ref, h2_ref, rt_ref, base_ref):
    @pl.when((pl.program_id(0) == 0) & (pl.program_id(1) == 0))
    def _():
        base_ref[...] = jnp.zeros_like(base_ref)

    tm = x_ref.shape[1]
    hm = tm // POST_SUBTILES
    groups = [slice(i * hm, (i + 1) * hm) for i in range(POST_SUBTILES)]
    n_mem = kbd_ref.shape[2] // MEM_HEADS
    lane = lax.broadcasted_iota(jnp.int32, (hm, V7X_LANES), 1).astype(F32)
    neg = -jnp.inf
    big = 1e9

    s = [jnp.dot(mq_ref[0, g, :], kbd_ref[0], preferred_element_type=F32) * (HEAD_DIM ** -0.5)
         for g in groups]
    mixed = [jnp.dot(mix_ref[0, g, :], wom_ref[...], preferred_element_type=F32) for g in groups]
    probs = []
    for sg in s:
        ps = []
        for h in range(MEM_HEADS):
            seg = sg[:, h * n_mem:(h + 1) * n_mem]
            e = jnp.exp(seg - jnp.max(seg, axis=-1, keepdims=True))
            ps.append((e * (1.0 / jnp.sum(e, axis=-1, keepdims=True))).astype(BF16))
        probs.append(jnp.concatenate(ps, axis=1))
    att = [jnp.dot(p, vbd_ref[0], preferred_element_type=F32).astype(BF16) for p in probs]
    logits = []
    for i, g in enumerate(groups):
        x1 = x_ref[0, g, :] + mixed[i] + jnp.dot(att[i], woa_ref[...], preferred_element_type=F32)
        x1_ref[0, g, :] = x1
        ms = jnp.mean(x1 * x1, axis=-1, keepdims=True)
        h2 = x1 * lax.rsqrt(ms + NORM_EPS) * g_ref[...]
        _store_row_tiles(h2_ref.at[pl.ds(i * hm * ROW_TILE, hm * ROW_TILE)], h2)
        h_hi, h_lo = _split_bf16(h2, 2)
        hw = jnp.dot(h_hi, wrt_ref[...], preferred_element_type=F32)
        logits.append(hw[:, :V7X_LANES] + hw[:, V7X_LANES:]
                      + jnp.dot(h_lo, wrt_ref[:, :V7X_LANES], preferred_element_type=F32) + brt_ref[...])
    routed = []
    for lg in logits:
        gl = jnp.where(lane < N_GROUPS, lg, neg)
        gm = jnp.max(gl, axis=-1, keepdims=True)
        g_gate = 1.0 / jnp.sum(jnp.exp(gl - gm), axis=-1, keepdims=True)
        g_idx = jnp.min(jnp.where(gl == gm, lane, big), axis=-1, keepdims=True)
        lo = N_GROUPS + EXPERTS_PER_GROUP * g_idx
        v1 = jnp.where((lane >= lo) & (lane < lo + EXPERTS_PER_GROUP), lg, neg)
        m1 = jnp.max(v1, axis=-1, keepdims=True)
        i1 = jnp.min(jnp.where(v1 == m1, lane, big), axis=-1, keepdims=True)
        v2 = jnp.where(lane == i1, neg, v1)
        m2 = jnp.max(v2, axis=-1, keepdims=True)
        i2 = jnp.min(jnp.where(v2 == m2, lane, big), axis=-1, keepdims=True)
        t = jnp.exp(m2 - m1)
        routed.append((i1 - N_GROUPS, i2 - N_GROUPS, g_gate / (1.0 + t), g_gate * t / (1.0 + t)))
    ri = lax.broadcasted_iota(jnp.int32, (hm, hm), 0)
    ci = lax.broadcasted_iota(jnp.int32, (hm, hm), 1)
    before = jnp.where(ci < ri, 1.0, 0.0).astype(BF16)
    onehots = [jnp.where(lane == e1, 1.0, 0.0) + jnp.where(lane == e2, 1.0, 0.0)
               for e1, e2, _, _ in routed]
    earlier = [jnp.dot(before, oh.astype(BF16), preferred_element_type=F32) for oh in onehots]
    for i, g in enumerate(groups):
        e1, e2, w1, w2 = routed[i]
        ahead = earlier[i] + base_ref[...]
        pos1 = jnp.sum(jnp.where(lane == e1, ahead, 0.0), axis=-1, keepdims=True)
        pos2 = jnp.sum(jnp.where(lane == e2, ahead, 0.0), axis=-1, keepdims=True)
        base_ref[...] = base_ref[...] + jnp.sum(onehots[i], axis=0, keepdims=True)
        rt_ref[0, g, :] = jnp.where(
            lane == 0, e1, jnp.where(lane == 1, e2, jnp.where(lane == 2, w1, jnp.where(
                lane == 3, w2, jnp.where(lane == 4, pos1, jnp.where(lane == 5, pos2, 0.0))))))


def _post_mixer(x, mix, u, mq_off, kbd, vbd, w_out, g, w_rt, b_rt, *, tm):
    b, t, d = x.shape
    mixw = mix.shape[2]
    wom = w_out[0]
    woa = w_out[1]
    return pl.pallas_call(
        _post_kernel,
        grid=(b, t // tm),
        in_specs=[pl.BlockSpec((1, tm, d), lambda bi, ti: (bi, ti, 0)),
                  pl.BlockSpec((1, tm, mixw), lambda bi, ti: (bi, ti, 0)),
                  pl.BlockSpec((1, tm, MEM_DIM), lambda bi, ti: (bi, ti, mq_off // MEM_DIM)),
                  pl.BlockSpec((1,) + kbd.shape[1:], lambda bi, ti: (bi, 0, 0)),
                  pl.BlockSpec((1,) + vbd.shape[1:], lambda bi, ti: (bi, 0, 0)),
                  pl.BlockSpec(wom.shape, lambda bi, ti: (0, 0)),
                  pl.BlockSpec(woa.shape, lambda bi, ti: (0, 0)),
                  pl.BlockSpec((1, d), lambda bi, ti: (0, 0)),
                  pl.BlockSpec(w_rt.shape, lambda bi, ti: (0, 0)),
                  pl.BlockSpec((1, V7X_LANES), lambda bi, ti: (0, 0))],
        out_specs=[pl.BlockSpec((1, tm, d), lambda bi, ti: (bi, ti, 0)),
                   pl.BlockSpec((tm * ROW_TILE, V7X_LANES), lambda bi, ti: (bi * (t // tm) + ti, 0)),
                   pl.BlockSpec((1, tm, V7X_LANES), lambda bi, ti: (bi, ti, 0))],
        out_shape=[jax.ShapeDtypeStruct((b, t, d), F32),
                   jax.ShapeDtypeStruct((b * t * ROW_TILE, V7X_LANES), F32),
                   jax.ShapeDtypeStruct((b, t, V7X_LANES), F32)],
        scratch_shapes=[pltpu.VMEM((1, V7X_LANES), F32)],
        compiler_params=_cparams(("arbitrary", "arbitrary")),
        name="post_mixer",
    )(x, mix, u, kbd, vbd, wom, woa, g.reshape(1, d), w_rt, b_rt)


def _memory_kv_blockdiag(mkv):
    k, v = jnp.split(mkv, 2, axis=-1)
    head_of = np.arange(MEM_DIM) // HEAD_DIM
    sel = jnp.asarray(head_of[None, :] == np.arange(MEM_HEADS)[:, None], F32)
    kbd = jnp.einsum('bmd,hd->bdhm', k, sel).reshape(k.shape[0], MEM_DIM, -1)
    vbd = jnp.einsum('bmd,hd->bhmd', v, sel).reshape(v.shape[0], -1, MEM_DIM)
    return kbd.astype(BF16), vbd.astype(BF16)


FFN_X_SLOTS = 4
FFN_Y_SLOTS = 2
FFN_W_SLOTS = 2


def _moe_kernel(nblk_ref, bexp_ref, bfirst_ref, nexp_ref, wslot_ref, w1_ref, w3_ref, w2_ref, xs_ref, y_ref,
                wf1, wf3, wf2, wb1, wb3, wb2, xbuf, ybuf, wsem, xsem, ysem, *, layer):
    del xs_ref
    n_blk = nblk_ref[0]

    def rows_of(blk):
        return _row_slice(blk * MOE_BLOCK, MOE_BLOCK)

    def fetch(blk):
        slot = blk % FFN_X_SLOTS
        return pltpu.make_async_copy(y_ref.at[rows_of(blk)], xbuf.at[slot], xsem.at[slot])

    def writeback(blk):
        slot = blk % FFN_Y_SLOTS
        return pltpu.make_async_copy(ybuf.at[slot], y_ref.at[rows_of(blk)], ysem.at[slot])

    def weight_copies(expert, slot):
        return [pltpu.make_async_copy(src.at[layer, expert], dst.at[slot], wsem.at[slot, j])
                for j, (src, dst) in enumerate(((w1_ref, wf1), (w3_ref, wf3), (w2_ref, wf2)))]

    for cp in weight_copies(bexp_ref[0], 0):
        cp.start()
    for ahead in range(FFN_X_SLOTS - 1):
        @pl.when(ahead < n_blk)
        def _(ahead=ahead):
            fetch(ahead).start()

    def per_block(blk, carry):
        @pl.when(blk + FFN_X_SLOTS - 1 < n_blk)
        def _():
            fetch(blk + FFN_X_SLOTS - 1).start()

        @pl.when(bfirst_ref[blk] != 0)
        def _():
            slot = wslot_ref[blk]

            @pl.when(nexp_ref[blk] >= 0)
            def _():
                for cp in weight_copies(nexp_ref[blk], 1 - slot):
                    cp.start()

            for cp in weight_copies(bexp_ref[blk], slot):
                cp.wait()
            wb1[...] = wf1[slot].astype(BF16)
            wb3[...] = wf3[slot].astype(BF16)
            wb2[...] = wf2[slot].astype(BF16)

        fetch(blk).wait()

        @pl.when(blk >= FFN_Y_SLOTS)
        def _():
            writeback(blk - FFN_Y_SLOTS).wait()

        xb = _load_row_tiles(xbuf.at[blk % FFN_X_SLOTS], MOE_BLOCK).astype(BF16)
        half = wb1.shape[1] // 2
        up = [(jnp.dot(xb, wb1[:, hf * half:(hf + 1) * half], preferred_element_type=F32),
               jnp.dot(xb, wb3[:, hf * half:(hf + 1) * half], preferred_element_type=F32)) for hf in range(2)]
        y = None
        for hf, (a, c) in enumerate(up):
            hid = (a / (1.0 + jnp.exp(-a)) * c).astype(BF16)
            part = jnp.dot(hid, wb2[hf * half:(hf + 1) * half, :], preferred_element_type=F32)
            y = part if y is None else y + part
        _store_row_tiles(ybuf.at[blk % FFN_Y_SLOTS], y)
        writeback(blk).start()
        return carry

    lax.fori_loop(0, n_blk, per_block, 0)
    for back in range(FFN_Y_SLOTS, 0, -1):
        @pl.when(n_blk >= back)
        def _(back=back):
            writeback(n_blk - back).wait()


def _moe_ffn(xs, sched, w1, w3, w2, layer):
    d = w1.shape[2]
    ff = w1.shape[3]
    blk_rows = MOE_BLOCK * ROW_TILE
    any_spec = pl.BlockSpec(memory_space=pl.ANY)
    grid_spec = pltpu.PrefetchScalarGridSpec(
        num_scalar_prefetch=5,
        grid=(1,),
        in_specs=[any_spec, any_spec, any_spec, any_spec],
        out_specs=any_spec,
        scratch_shapes=[pltpu.VMEM((FFN_W_SLOTS, d, ff), F32), pltpu.VMEM((FFN_W_SLOTS, d, ff), F32),
                        pltpu.VMEM((FFN_W_SLOTS, ff, d), F32),
                        pltpu.VMEM((d, ff), BF16), pltpu.VMEM((d, ff), BF16), pltpu.VMEM((ff, d), BF16),
                        pltpu.VMEM((FFN_X_SLOTS, blk_rows, V7X_LANES), F32),
                        pltpu.VMEM((FFN_Y_SLOTS, blk_rows, V7X_LANES), F32),
                        pltpu.SemaphoreType.DMA((FFN_W_SLOTS, 3)),
                        pltpu.SemaphoreType.DMA((FFN_X_SLOTS,)),
                        pltpu.SemaphoreType.DMA((FFN_Y_SLOTS,))],
    )
    return pl.pallas_call(
        functools.partial(_moe_kernel, layer=layer),
        grid_spec=grid_spec,
        out_shape=jax.ShapeDtypeStruct(xs.shape, F32),
        input_output_aliases={8: 0},
        compiler_params=_cparams(("arbitrary",)),
        name="moe_ffn",
    )(sched["nblk"], sched["bexp"], sched["bfirst"], sched["nexp"], sched["wslot"], w1, w3, w2, xs)


def _row_slice(row, n_rows):
    return pl.ds(pl.multiple_of(row * ROW_TILE, ROW_TILE), n_rows * ROW_TILE)


def _pow2_pieces(n, largest, act):
    k = largest
    while k >= 1:
        shift = k.bit_length()

        @pl.when((n & k) != 0)
        def _(k=k, shift=shift):
            act((n >> shift) << shift, k)
        k //= 2


def _run_copies(runs_ref, make_copy, act):
    def per_expert(e, carry):
        n = runs_ref[0, 0, e]
        local_row = runs_ref[0, 1, e]
        sorted_row = runs_ref[0, 2, e]

        def chunk(i, c):
            act(make_copy(local_row + 8 * i, sorted_row + 8 * i, 8))
            return c

        lax.fori_loop(0, n >> 3, chunk, 0)
        _pow2_pieces(n, 4, lambda off, k: act(make_copy(local_row + off, sorted_row + off, k)))
        return carry

    lax.fori_loop(0, N_EXPERTS, per_expert, 0)


def _dispatch_kernel(seg_ref, lpos_ref, runs_ref, h_ref, xs_ref, local, zbuf, sem):
    tm = h_ref.shape[0] // ROW_TILE
    n_rows = xs_ref.shape[0] // ROW_TILE
    zrows = zbuf.shape[0] // ROW_TILE

    def place(tok, carry):
        row = h_ref[_row_slice(tok, 1), :]
        for k in range(TOP_K):
            local[_row_slice(lpos_ref[0, 0, k * tm + tok], 1), :] = row
        return carry

    lax.fori_loop(0, tm, place, 0, unroll=MOVE_UNROLL)

    def to_sorted(local_row, sorted_row, n):
        return pltpu.make_async_copy(local.at[_row_slice(local_row, n)],
                                     xs_ref.at[_row_slice(sorted_row, n)], sem)

    _run_copies(runs_ref, to_sorted, lambda cp: cp.start())
    pltpu.make_async_copy(local, xs_ref.at[pl.ds(0, local.shape[0])], sem).wait()

    @pl.when(pl.program_id(0) == pl.num_programs(0) - 1)
    def _():
        zbuf[...] = jnp.zeros_like(zbuf)

        def zero_copy(first_row, k):
            return pltpu.make_async_copy(zbuf.at[_row_slice(0, k)], xs_ref.at[_row_slice(first_row, k)], sem)

        def pad_copies(act):
            def per_expert(e, carry):
                cnt = seg_ref[0, e]
                first = seg_ref[1, e] + cnt
                _pow2_pieces((-cnt) & (MOE_BLOCK - 1), zrows, lambda off, k: act(zero_copy(first + off, k)))
                return carry

            lax.fori_loop(0, N_EXPERTS, per_expert, 0)
            used = seg_ref[2, 0]

            def tail(i, carry):
                act(zero_copy(used + i * zrows, zrows))
                return carry

            lax.fori_loop(0, (n_rows - used) // zrows, tail, 0)

        pad_copies(lambda cp: cp.start())
        pad_copies(lambda cp: cp.wait())


def _dispatch(h2_tiles, seg, lpos, runs, n_rows, *, tm):
    n_tok = h2_tiles.shape[0] // ROW_TILE
    nt = n_tok // tm
    grid_spec = pltpu.PrefetchScalarGridSpec(
        num_scalar_prefetch=1,
        grid=(nt,),
        in_specs=[pl.BlockSpec((1, 1, TOP_K * tm), lambda i, c: (i, 0, 0), memory_space=pltpu.SMEM),
                  pl.BlockSpec((1, 3, N_EXPERTS), lambda i, c: (i, 0, 0), memory_space=pltpu.SMEM),
                  pl.BlockSpec((tm * ROW_TILE, V7X_LANES), lambda i, c: (i, 0))],
        out_specs=pl.BlockSpec(memory_space=pl.ANY),
        scratch_shapes=[pltpu.VMEM((TOP_K * tm * ROW_TILE, V7X_LANES), F32),
                        pltpu.VMEM((MOE_BLOCK // 2 * ROW_TILE, V7X_LANES), F32),
                        pltpu.SemaphoreType.DMA(())],
    )
    return pl.pallas_call(
        _dispatch_kernel,
        grid_spec=grid_spec,
        out_shape=jax.ShapeDtypeStruct((n_rows * ROW_TILE, V7X_LANES), F32),
        compiler_params=_cparams(("arbitrary",)),
        name="moe_dispatch",
    )(seg, lpos, runs, h2_tiles)


def _combine_kernel(*refs, final):
    if final:
        lpos_ref, gate_ref, runs_ref, next_runs_ref, x1_ref, yb_ref, g_ref, o_ref, local, acc, sem = refs
    else:
        lpos_ref, gate_ref, runs_ref, next_runs_ref, x1_ref, yb_ref, o_ref, local, acc, sem = refs
    tm = x1_ref.shape[0]
    step = pl.program_id(0)
    cur = step % 2

    def from_sorted(slot):
        def make(local_row, sorted_row, n):
            return pltpu.make_async_copy(yb_ref.at[_row_slice(sorted_row, n)],
                                         local.at[slot, _row_slice(local_row, n)], sem.at[slot])
        return make

    @pl.when(step == 0)
    def _():
        _run_copies(runs_ref, from_sorted(cur), lambda cp: cp.start())

    @pl.when(step + 1 < pl.num_programs(0))
    def _():
        _run_copies(next_runs_ref, from_sorted(1 - cur), lambda cp: cp.start())

    pltpu.make_async_copy(yb_ref.at[pl.ds(0, local.shape[1])], local.at[cur], sem.at[cur]).wait()

    def mix(tok, carry):
        y = jnp.zeros((ROW_TILE, V7X_LANES), F32)
        for k in range(TOP_K):
            idx = k * tm + tok
            y = y + gate_ref[0, 0, idx] * local[cur, _row_slice(lpos_ref[0, 0, idx], 1), :]
        acc[_row_slice(tok, 1), :] = y
        return carry

    lax.fori_loop(0, tm, mix, 0, unroll=MOVE_UNROLL)
    x2 = x1_ref[...] + _load_row_tiles(acc, tm)
    if final:
        ms = jnp.mean(x2 * x2, axis=-1, keepdims=True)
        x2 = x2 * lax.rsqrt(ms + NORM_EPS) * g_ref[...]
    o_ref[...] = x2


def _combine(x1, yb, lpos, gates, runs, *, tm, final_g=None):
    n_tok, d = x1.shape
    nt = n_tok // tm
    final = final_g is not None
    in_specs = [pl.BlockSpec((1, 1, TOP_K * tm), lambda i: (i, 0, 0), memory_space=pltpu.SMEM),
                pl.BlockSpec((1, 1, TOP_K * tm), lambda i: (i, 0, 0), memory_space=pltpu.SMEM),
                pl.BlockSpec((1, 3, N_EXPERTS), lambda i: (i, 0, 0), memory_space=pltpu.SMEM),
                pl.BlockSpec((1, 3, N_EXPERTS), lambda i: (jnp.minimum(i + 1, nt - 1), 0, 0),
                             memory_space=pltpu.SMEM),
                pl.BlockSpec((tm, d), lambda i: (i, 0)),
                pl.BlockSpec(memory_space=pl.ANY)]
    args = [lpos, gates, runs, runs, x1, yb]
    if final:
        in_specs.append(pl.BlockSpec((1, d), lambda i: (0, 0)))
        args.append(final_g.reshape(1, d))
    return pl.pallas_call(
        functools.partial(_combine_kernel, final=final),
        grid=(nt,),
        in_specs=in_specs,
        out_specs=pl.BlockSpec((tm, d), lambda i: (i, 0)),
        out_shape=jax.ShapeDtypeStruct((n_tok, d), F32),
        scratch_shapes=[pltpu.VMEM((2, TOP_K * tm * ROW_TILE, V7X_LANES), F32),
                        pltpu.VMEM((tm * ROW_TILE, V7X_LANES), F32),
                        pltpu.SemaphoreType.DMA((2,))],
        compiler_params=_cparams(("arbitrary",)),
        name="moe_combine_final" if final else "moe_combine",
    )(*args)


def _moe_plan(rt, n_tok, tm):
    nt = n_tok // tm
    rec = rt[:, :ROW_TILE].T.reshape(ROW_TILE, nt, tm)
    expert = rec[0:TOP_K].astype(jnp.int32)
    gate = rec[TOP_K:2 * TOP_K]
    pos = rec[2 * TOP_K:3 * TOP_K].astype(jnp.int32)
    is_e = expert[None] == jnp.arange(N_EXPERTS, dtype=jnp.int32)[:, None, None, None]
    cnt_tile = jnp.sum(is_e, axis=(1, 3), dtype=jnp.int32).T
    seen_before = jnp.cumsum(cnt_tile, axis=0) - cnt_tile
    local_start = jnp.cumsum(cnt_tile, axis=1) - cnt_tile
    counts = jnp.sum(cnt_tile, axis=0)
    shift = (local_start - seen_before).T
    lpos = pos + jnp.sum(jnp.where(is_e, shift[:, None, :, None], 0), axis=0)
    by_tile = lambda a: a.transpose(1, 0, 2).reshape(nt, 1, TOP_K * tm)
    nblk = (counts + MOE_BLOCK - 1) // MOE_BLOCK
    bend = jnp.cumsum(nblk)
    seg_start = (bend - nblk) * MOE_BLOCK
    sorted_start = seg_start[None, :] + seen_before
    runs = jnp.stack([cnt_tile, local_start, sorted_start], axis=1)
    seg = jnp.stack([counts, seg_start, jnp.full((N_EXPERTS,), bend[-1] * MOE_BLOCK, jnp.int32)])
    n_steps = n_tok * TOP_K // MOE_BLOCK + N_EXPERTS
    step = jnp.arange(n_steps, dtype=jnp.int32)
    bexp = jnp.sum(jnp.minimum(step, bend[-1] - 1)[:, None] >= bend[None, :], axis=1).astype(jnp.int32)
    bfirst = jnp.concatenate([jnp.ones((1,), jnp.int32), (bexp[1:] != bexp[:-1]).astype(jnp.int32)])
    wslot = (jnp.cumsum(bfirst) - 1) % FFN_W_SLOTS
    switch_at = jnp.where(bfirst != 0, step, n_steps)
    next_switch = jnp.concatenate([lax.cummin(switch_at[::-1])[::-1][1:], jnp.full((1,), n_steps, jnp.int32)])
    nexp = jnp.sum(jnp.where(next_switch[:, None] == step[None, :], bexp[None, :] + 1, 0), axis=1) - 1
    sched = dict(nblk=bend[-1:].astype(jnp.int32), bexp=bexp, bfirst=bfirst,
                 nexp=nexp.astype(jnp.int32), wslot=wslot.astype(jnp.int32))
    return dict(seg=seg.astype(jnp.int32), n_rows=n_steps * MOE_BLOCK, sched=sched,
                lpos=by_tile(lpos), gates=by_tile(gate), runs=runs)


def _moe_layer(x1, h2_tiles, rt, w1, w3, w2, layer, *, final_g=None):
    n_tok = x1.shape[0]
    tm = 1024
    plan = _moe_plan(rt, n_tok, tm)
    xs = _dispatch(h2_tiles, plan["seg"], plan["lpos"], plan["runs"], plan["n_rows"], tm=tm)
    yb = _moe_ffn(xs, plan["sched"], w1, w3, w2, layer)
    return _combine(x1, yb, plan["lpos"], plan["gates"], plan["runs"], tm=tm, final_g=final_g)


def kernel(x, mem, mem_norm, final_norm, norm_mix, norm_ffn, w_mem_kv, w_out, na_w_in, na_rpb, gla_w_in,
           gla_gate_up, gla_gate_bias, gla_out_norm, moe_w_group, moe_b_group, moe_w_router, moe_b_router,
           moe_w1, moe_w3, moe_w2):
    b, t, d = x.shape
    n = b * t
    depth = norm_mix.shape[0]
    n_mem = mem.shape[1]
    xf = x.reshape(n, d)
    for i in range(depth):
        j = i // 2
        mkv = _norm_matmul(mem.reshape(b * n_mem, d), mem_norm, w_mem_kv[i].astype(BF16),
                           tm=256, out_dtype=F32, name="mem_kv_proj").reshape(b, n_mem, 2 * MEM_DIM)
        kbd, vbd = _memory_kv_blockdiag(mkv)
        if i % 2 == 0:
            u = _norm_matmul(xf, norm_mix[i], na_w_in[j].astype(BF16), tm=IN_PROJ_TILE, out_dtype=BF16,
                             name="na_in_proj").reshape(b, t, -1)
            mix = _na_attention(u, na_rpb[j])
            mq_off = NA_MQ_OFF
            w_o = w_out[i].astype(BF16)
            w_o = (w_o[:MIX_DIM], w_o[MIX_DIM:])
        else:
            w_pad, gu, gb, onorm = _gla_weights(gla_w_in[j], gla_gate_up[j], gla_gate_bias[j], gla_out_norm[j])
            u = _norm_matmul(xf, norm_mix[i], w_pad, tm=IN_PROJ_TILE, out_dtype=BF16, name="gla_in_proj").reshape(b, t, -1)
            ofwd = _gla_direction(u, gu[0], gb[0], reverse=False, tb=GLA_TIME_BLOCK)
            mix = _gla_direction(u, gu[1], gb[1], reverse=True, tb=GLA_TIME_BLOCK, ofwd=ofwd, onorm=onorm)
            mq_off = GLA_MQ_OFF
            w_mix = w_out[i][:MIX_DIM].reshape(GLA_HEADS, GLA_DV, d)
            w_mix = jnp.pad(w_mix, ((0, 0), (0, GLA_DV_PAD - GLA_DV), (0, 0))).reshape(GLA_MIX_PAD, d)
            w_o = (w_mix.astype(BF16), w_out[i][MIX_DIM:].astype(BF16))
        w_rt = jnp.pad(jnp.concatenate([moe_w_group[i], moe_w_router[i]], axis=1),
                       ((0, 0), (0, V7X_LANES - N_GROUPS - N_EXPERTS))).astype(F32)
        w_rt = jnp.concatenate(_split_bf16(w_rt, 2), axis=1)
        b_rt = jnp.pad(jnp.concatenate([moe_b_group[i], moe_b_router[i]]),
                       (0, V7X_LANES - N_GROUPS - N_EXPERTS)).reshape(1, V7X_LANES)
        x1, h2_tiles, rt = _post_mixer(xf.reshape(b, t, d), mix, u, mq_off, kbd, vbd, w_o, norm_ffn[i],
                                       w_rt, b_rt.astype(F32), tm=POST_TILE)
        xf = _moe_layer(x1.reshape(n, d), h2_tiles, rt.reshape(n, V7X_LANES),
                        moe_w1, moe_w3, moe_w2, i,
                        final_g=final_norm if i == depth - 1 else None)
    return xf.reshape(b, t, d)
```
